```python
import jax, jax.numpy as jnp
from jax import lax
import numpy as np

D_MODEL = 4096
BATCH = 4
SEQ = 4096
DEPTH = 1

PLE_DIM = 256
EPS = 1e-6
M_HEADS = 8
M_QK_HD = 128
M_V_HD = 256
M_QK = M_HEADS * M_QK_HD
M_V = M_HEADS * M_V_HD
M_CHUNK = 64
A_HEADS = 16
A_NOPE = 128
A_ROPE = 64
A_QK_HD = A_NOPE + A_ROPE
A_V_HD = 128
A_V = A_HEADS * A_V_HD
Q_LORA = 1024
KV_LORA = 512
ROPE_THETA = 10000.0
Q_BLOCK = 128
IN_SIZES = (M_QK, M_QK, M_V, M_V, 4 * M_HEADS, Q_LORA, KV_LORA, A_ROPE, 2 * D_MODEL)
N_IN = 2 * M_QK + 2 * M_V + 4 * M_HEADS + Q_LORA + KV_LORA + A_ROPE + 2 * D_MODEL
N_GROUPS = 8
E_PER_GROUP = 8
N_EXPERTS = N_GROUPS * E_PER_GROUP
TOP_K = 2
D_EXPERT = 512
MOE_BLOCK = 128

kernel_name = "hybrid_mlstm_mla_hier_moe_ple"


def _split_cols(z, sizes):
    idx, acc = [], 0
    for s in sizes[:-1]:
        acc += s
        idx.append(acc)
    return jnp.split(z, idx, axis=-1)


def rmsnorm(x, g):
    xf = x.astype(jnp.float32)
    y = xf * lax.rsqrt(jnp.mean(xf * xf, axis=-1, keepdims=True) + EPS)
    return (y * g.astype(jnp.float32)).astype(x.dtype)


def rope_tables(positions):
    inv = ROPE_THETA ** (-jnp.arange(0, A_ROPE, 2, dtype=jnp.float32) / A_ROPE)
    ang = positions.astype(jnp.float32)[..., None] * inv
    return jnp.cos(ang)[:, :, None, :], jnp.sin(ang)[:, :, None, :]


def apply_rope(t, cos, sin):
    t1, t2 = jnp.split(t.astype(jnp.float32), 2, axis=-1)
    return jnp.concatenate([t1 * cos - t2 * sin, t2 * cos + t1 * sin], axis=-1).astype(t.dtype)


def mlstm_scan(q, k, v, i_pre, f_pre):
    B, H, S, dk = q.shape
    dv = v.shape[-1]
    nc = S // M_CHUNK

    def chunks(t):
        return jnp.moveaxis(t.reshape((B, H, nc, M_CHUNK) + t.shape[3:]), 2, 0)

    logf = jax.nn.log_sigmoid(f_pre)
    xs = (chunks(q), chunks(k), chunks(v), chunks(i_pre), chunks(logf))
    tri = jnp.tril(jnp.ones((M_CHUNK, M_CHUNK), dtype=bool))

    def step(carry, inp):
        C, n, m = carry
        qc, kc, vc, ic, lfc = inp
        b = jnp.cumsum(lfc, axis=-1)
        a = b + m[..., None]
        D = jnp.where(tri, b[..., :, None] - b[..., None, :] + ic[..., None, :], -jnp.inf)
        m_t = jnp.maximum(a, jnp.max(D, axis=-1))
        w_intra = jnp.exp(D - m_t[..., None]) * jnp.einsum('bhtd,bhsd->bhts', qc, kc)
        w_inter = jnp.exp(a - m_t)
        num = (jnp.einsum('bhts,bhsv->bhtv', w_intra, vc)
               + w_inter[..., None] * jnp.einsum('bhtd,bhdv->bhtv', qc, C))
        den = jnp.sum(w_intra, axis=-1) + w_inter * jnp.einsum('bhtd,bhd->bht', qc, n)
        h = num / jnp.maximum(jnp.abs(den), jnp.exp(-m_t))[..., None]
        g = b[..., -1]
        w = g[..., None] - b + ic
        m_new = jnp.maximum(g + m, jnp.max(w, axis=-1))
        decay = jnp.exp(g + m - m_new)
        ws = jnp.exp(w - m_new[..., None])
        C_new = decay[..., None, None] * C + jnp.einsum('bhs,bhsd,bhsv->bhdv', ws, kc, vc)
        n_new = decay[..., None] * n + jnp.einsum('bhs,bhsd->bhd', ws, kc)
        return (C_new, n_new, m_new), h

    init = (jnp.zeros((B, H, dk, dv), q.dtype), jnp.zeros((B, H, dk), q.dtype),
            jnp.zeros((B, H), q.dtype))
    _, hs = lax.scan(step, init, xs)
    return jnp.moveaxis(hs, 0, 2).reshape(B, H, S, dv)


def mla_attention(q, k, v):
    B, S, H, dq = q.shape
    nb = S // Q_BLOCK
    qb = jnp.moveaxis(q.reshape(B, nb, Q_BLOCK, H, dq), 1, 0)
    scale = dq ** -0.5

    def block(qi):
        s = jnp.einsum('bqhd,bkhd->bhqk', qi, k, preferred_element_type=jnp.float32) * scale
        pr = jax.nn.softmax(s, axis=-1).astype(v.dtype)
        return jnp.einsum('bhqk,bkhv->bqhv', pr, v)

    out = lax.map(block, qb)
    return jnp.moveaxis(out, 0, 1).reshape(B, S, H * v.shape[-1])


def hier_moe(hn, w_rg, b_rg, w_re, b_re, w1, w3, w2):
    B, S, D = hn.shape
    T = B * S
    xf = hn.reshape(T, D)
    lg = (xf @ w_rg).astype(jnp.float32) + b_rg.astype(jnp.float32)
    pg = jax.nn.softmax(lg, axis=-1)
    g_idx = jnp.argmax(lg, axis=-1).astype(jnp.int32)
    g_w = jnp.take_along_axis(pg, g_idx[:, None], axis=-1)
    le = ((xf @ w_re).astype(jnp.float32) + b_re.astype(jnp.float32)).reshape(T, N_GROUPS, E_PER_GROUP)
    le_g = le[jnp.arange(T), g_idx]
    top_v, top_i = lax.top_k(le_g, TOP_K)
    w_tok = jax.nn.softmax(top_v, axis=-1) * g_w
    eid = (g_idx[:, None] * E_PER_GROUP + top_i.astype(jnp.int32)).reshape(-1)
    tok = jnp.broadcast_to(jnp.arange(T, dtype=jnp.int32)[:, None], (T, TOP_K)).reshape(-1)
    wts = w_tok.reshape(-1)
    A = T * TOP_K
    order = jnp.argsort(eid)
    e_s, tok_s, w_s = eid[order], tok[order], wts[order]
    counts = jnp.zeros((N_EXPERTS,), jnp.int32).at[eid].add(1)
    starts = jnp.cumsum(counts) - counts
    padded = (counts + MOE_BLOCK - 1) // MOE_BLOCK * MOE_BLOCK
    pends = jnp.cumsum(padded)
    pstarts = pends - padded
    dest = pstarts[e_s] + (jnp.arange(A, dtype=jnp.int32) - starts[e_s])
    P = A + N_EXPERTS * MOE_BLOCK
    nb = P // MOE_BLOCK
    buf_tok = jnp.zeros((P,), jnp.int32).at[dest].set(tok_s)
    buf_w = jnp.zeros((P,), wts.dtype).at[dest].set(w_s)
    blk_start = jnp.arange(nb, dtype=jnp.int32) * MOE_BLOCK
    blk_e = jnp.minimum(jnp.sum(blk_start[:, None] >= pends[None, :], axis=1), N_EXPERTS - 1)

    def expert_block(args):
        t_idx, wb, e = args
        xb = xf[t_idx]
        hdn = jax.nn.silu(xb @ w1[e]) * (xb @ w3[e])
        return (hdn @ w2[e]) * wb[:, None].astype(xb.dtype)

    yb = lax.map(expert_block, (buf_tok.reshape(nb, MOE_BLOCK), buf_w.reshape(nb, MOE_BLOCK), blk_e))
    out = jnp.zeros((T, D), hn.dtype).at[buf_tok].add(yb.reshape(P, D))
    return out.reshape(B, S, D)


def setup_inputs(seed: int = 0) -> dict:
    key = jax.random.key(seed)
    ks = jax.random.split(key, 32)
    f32 = jnp.float32

    def dense(k, shape, fan_in):
        return jax.random.normal(k, shape, f32) * fan_in ** -0.5

    def gain(k, shape):
        return 1.0 + 0.02 * jax.random.normal(k, shape, f32)

    x = jax.random.normal(ks[0], (BATCH, SEQ, D_MODEL), f32)
    p = jax.random.normal(ks[1], (DEPTH, BATCH, SEQ, PLE_DIM), f32)
    positions = jnp.broadcast_to(jnp.arange(SEQ, dtype=jnp.int32), (BATCH, SEQ))
    f_bias = jnp.linspace(3.0, 6.0, M_HEADS, dtype=f32)
    zero_h = jnp.zeros((M_HEADS,), f32)
    base = jnp.stack([zero_h, f_bias, zero_h, f_bias])[None]
    b_mgate = (base + 0.05 * jax.random.normal(ks[2], (DEPTH, 4, M_HEADS), f32)).reshape(DEPTH, 4 * M_HEADS)
    return {
        "x": x,
        "p": p,
        "positions": positions,
        "g_mix": gain(ks[3], (DEPTH, D_MODEL)),
        "w_in": dense(ks[4], (DEPTH, D_MODEL, N_IN), D_MODEL),
        "b_mgate": b_mgate,
        "g_mhead": gain(ks[5], (DEPTH, M_V)),
        "g_cq": gain(ks[6], (DEPTH, Q_LORA)),
        "w_uq": dense(ks[7], (DEPTH, Q_LORA, A_HEADS * A_QK_HD), Q_LORA),
        "g_ckv": gain(ks[8], (DEPTH, KV_LORA)),
        "w_ukv": dense(ks[9], (DEPTH, KV_LORA, A_HEADS * (A_NOPE + A_V_HD)), KV_LORA),
        "g_qn": gain(ks[10], (DEPTH, A_QK_HD)),
        "g_kn": gain(ks[11], (DEPTH, A_QK_HD)),
        "w_bm": dense(ks[12], (DEPTH, M_V, D_MODEL), M_V),
        "w_ba": dense(ks[13], (DEPTH, A_V, D_MODEL), A_V),
        "w_out": dense(ks[14], (DEPTH, D_MODEL, D_MODEL), D_MODEL),
        "g_ffn": gain(ks[15], (DEPTH, D_MODEL)),
        "w_rg": dense(ks[16], (DEPTH, D_MODEL, N_GROUPS), D_MODEL),
        "b_rg": 0.01 * jax.random.normal(ks[17], (DEPTH, N_GROUPS), f32),
        "w_re": dense(ks[18], (DEPTH, D_MODEL, N_EXPERTS), D_MODEL),
        "b_re": 0.01 * jax.random.normal(ks[19], (DEPTH, N_EXPERTS), f32),
        "w_e1": dense(ks[20], (DEPTH, N_EXPERTS, D_MODEL, D_EXPERT), D_MODEL),
        "w_e3": dense(ks[21], (DEPTH, N_EXPERTS, D_MODEL, D_EXPERT), D_MODEL),
        "w_e2": dense(ks[22], (DEPTH, N_EXPERTS, D_EXPERT, D_MODEL), D_EXPERT),
        "g_ple": gain(ks[23], (DEPTH, D_MODEL)),
        "w_pg": dense(ks[24], (DEPTH, D_MODEL, D_MODEL), D_MODEL),
        "w_ple": dense(ks[25], (DEPTH, PLE_DIM, D_MODEL), PLE_DIM),
    }


def reference(x, p, positions, g_mix, w_in, b_mgate, g_mhead, g_cq, w_uq, g_ckv, w_ukv,
              g_qn, g_kn, w_bm, w_ba, w_out, g_ffn, w_rg, b_rg, w_re, b_re,
              w_e1, w_e3, w_e2, g_ple, w_pg, w_ple):
    B, S, D = x.shape
    f32 = jnp.float32
    cos, sin = rope_tables(positions)
    for l in range(DEPTH):
        h = rmsnorm(x, g_mix[l])
        z = h @ w_in[l]
        q_m, k_m, v_m, o_m, gate_pre, c_q, c_kv, k_pe, br = _split_cols(z, IN_SIZES)

        qm = q_m.reshape(B, S, M_HEADS, M_QK_HD).transpose(0, 2, 1, 3).astype(f32) * (M_QK_HD ** -0.5)
        km = k_m.reshape(B, S, M_HEADS, M_QK_HD).transpose(0, 2, 1, 3).astype(f32)
        vm = v_m.reshape(B, S, M_HEADS, M_V_HD).transpose(0, 2, 1, 3).astype(f32)
        gp = (gate_pre.astype(f32) + b_mgate[l].astype(f32)).reshape(B, S, 4, M_HEADS).transpose(2, 0, 3, 1)
        h_fwd = mlstm_scan(qm, km, vm, gp[0], gp[1])
        h_bwd = mlstm_scan(qm[:, :, ::-1], km[:, :, ::-1], vm[:, :, ::-1],
                           gp[2][:, :, ::-1], gp[3][:, :, ::-1])[:, :, ::-1]
        hm = (h_fwd + h_bwd).transpose(0, 2, 1, 3).astype(x.dtype)
        hm = rmsnorm(hm, g_mhead[l].reshape(M_HEADS, M_V_HD)).reshape(B, S, M_V) * jax.nn.sigmoid(o_m)

        qa = (rmsnorm(c_q, g_cq[l]) @ w_uq[l]).reshape(B, S, A_HEADS, A_QK_HD)
        kv = (rmsnorm(c_kv, g_ckv[l]) @ w_ukv[l]).reshape(B, S, A_HEADS, A_NOPE + A_V_HD)
        k_nope, va = kv[..., :A_NOPE], kv[..., A_NOPE:]
        ka = jnp.concatenate([k_nope, jnp.broadcast_to(k_pe[:, :, None, :], (B, S, A_HEADS, A_ROPE))], axis=-1)
        qa = rmsnorm(qa, g_qn[l])
        ka = rmsnorm(ka, g_kn[l])
        qa = jnp.concatenate([qa[..., :A_NOPE], apply_rope(qa[..., A_NOPE:], cos, sin)], axis=-1)
        ka = jnp.concatenate([ka[..., :A_NOPE], apply_rope(ka[..., A_NOPE:], cos, sin)], axis=-1)
        ha = mla_attention(qa, ka, va)

        gates = jax.nn.sigmoid(br)
        mix = gates[..., :D] * (hm @ w_bm[l]) + gates[..., D:] * (ha @ w_ba[l])
        x = x + mix @ w_out[l]

        x = x + hier_moe(rmsnorm(x, g_ffn[l]), w_rg[l], b_rg[l], w_re[l], b_re[l],
                         w_e1[l], w_e3[l], w_e2[l])

        ple_gate = jax.nn.sigmoid(rmsnorm(x, g_ple[l]) @ w_pg[l])
        x = x + ple_gate * (p[l] @ w_ple[l])
    return x
```

```python
import functools

import jax
import jax.numpy as jnp
from jax import lax
from jax.experimental import pallas as pl
from jax.experimental.pallas import tpu as pltpu

F32 = jnp.float32
BF16 = jnp.bfloat16

EPS = 1e-6
M_HEADS = 8
M_QK_HD = 128
M_V_HD = 256
A_HEADS = 16
A_NOPE = 128
A_ROPE = 64
A_QK_HD = A_NOPE + A_ROPE
A_V_HD = 128
A_QK_PAD = 256
Q_LORA = 1024
KV_LORA = 512
ROPE_THETA = 10000.0
N_GROUPS = 8
E_PER_GROUP = 8
N_EXPERTS = N_GROUPS * E_PER_GROUP
D_EXPERT = 512
LANES = 128

M_CHUNK = 256
MOE_BLK = 256
ATT_TQ = 512
ATT_TK = 512
VMEM_LIMIT = 56 * 1024 * 1024
NEG = -1e30


def _cparams(sem):
    return pltpu.CompilerParams(dimension_semantics=sem, vmem_limit_bytes=VMEM_LIMIT)


def _sigmoid(x):
    return 1.0 / (1.0 + jnp.exp(-x))


def _log_sigmoid(x):
    return jnp.minimum(x, 0.0) - jnp.log(1.0 + jnp.exp(-jnp.abs(x)))


def _rms(xf, g):
    return xf * lax.rsqrt(jnp.mean(xf * xf, axis=-1, keepdims=True) + EPS) * g


def _norm_kernel(x_ref, g_ref, o_ref):
    o_ref[...] = _rms(x_ref[...], g_ref[...]).astype(o_ref.dtype)


def _rmsnorm(x, g, tm=256):
    M, D = x.shape
    tm = min(tm, M)
    return pl.pallas_call(
        _norm_kernel, grid=(M // tm,),
        in_specs=[pl.BlockSpec((tm, D), lambda i: (i, 0)),
                  pl.BlockSpec((1, D), lambda i: (0, 0))],
        out_specs=pl.BlockSpec((tm, D), lambda i: (i, 0)),
        out_shape=jax.ShapeDtypeStruct((M, D), BF16),
        compiler_params=_cparams(("parallel",)), name="rmsnorm")(x, g.reshape(1, D))


def _mm_kernel(a_ref, b_ref, o_ref):
    o_ref[...] = jnp.dot(a_ref[...], b_ref[...],
                         preferred_element_type=F32).astype(o_ref.dtype)


def _matmul(a, b, out_dtype, tm=1024, tn=1024, name="matmul"):
    M, K = a.shape
    N = b.shape[1]
    tm, tn = min(tm, M), min(tn, N)
    return pl.pallas_call(
        _mm_kernel, grid=(N // tn, M // tm),
        in_specs=[pl.BlockSpec((tm, K), lambda j, i: (i, 0)),
                  pl.BlockSpec((K, tn), lambda j, i: (0, j))],
        out_specs=pl.BlockSpec((tm, tn), lambda j, i: (i, j)),
        out_shape=jax.ShapeDtypeStruct((M, N), out_dtype),
        compiler_params=_cparams(("parallel", "parallel")), name=name)(a, b)


def _inproj_small_kernel(h_ref, wq_ref, wkv_ref, ws_ref, gq_ref, gkv_ref,
                         cq_ref, ckv_ref, sm_ref):
    h = h_ref[...]
    cq = jnp.dot(h, wq_ref[...], preferred_element_type=F32)
    cq_ref[...] = _rms(cq, gq_ref[...]).astype(cq_ref.dtype)
    ckv = jnp.dot(h, wkv_ref[...], preferred_element_type=F32)
    ckv_ref[...] = _rms(ckv, gkv_ref[...]).astype(ckv_ref.dtype)
    sm_ref[...] = jnp.dot(h, ws_ref[...], preferred_element_type=F32)


def _inproj_small(h, w_cq, w_ckv, w_sm, g_cq, g_ckv, tm=512):
    M, D = h.shape
    tm = min(tm, M)
    nq, nkv, ns = w_cq.shape[1], w_ckv.shape[1], w_sm.shape[1]
    full = lambda n: pl.BlockSpec((D, n), lambda i: (0, 0))
    row = lambda n: pl.BlockSpec((tm, n), lambda i: (i, 0))
    return pl.pallas_call(
        _inproj_small_kernel, grid=(M // tm,),
        in_specs=[row(D), full(nq), full(nkv), full(ns),
                  pl.BlockSpec((1, nq), lambda i: (0, 0)),
                  pl.BlockSpec((1, nkv), lambda i: (0, 0))],
        out_specs=[row(nq), row(nkv), row(ns)],
        out_shape=[jax.ShapeDtypeStruct((M, nq), BF16),
                   jax.ShapeDtypeStruct((M, nkv), BF16),
                   jax.ShapeDtypeStruct((M, ns), F32)],
        compiler_params=_cparams(("parallel",)), name="inproj_small")(
            h, w_cq, w_ckv, w_sm, g_cq.reshape(1, nq), g_ckv.reshape(1, nkv))


def _mlstm_kernel(*refs, reverse, finish, L, goff):
    if finish:
        (q_ref, k_ref, v_ref, gc_ref, gr_ref, hb_ref, o_ref, gm_ref,
         out_ref, C_ref, n_ref, m_ref) = refs
    else:
        q_ref, k_ref, v_ref, gc_ref, gr_ref, out_ref, C_ref, n_ref, m_ref = refs
    H, dk, dv = M_HEADS, M_QK_HD, M_V_HD
    scale = dk ** -0.5

    @pl.when(pl.program_id(1) == 0)
    def _():
        C_ref[...] = jnp.zeros_like(C_ref)
        n_ref[...] = jnp.zeros_like(n_ref)
        m_ref[...] = jnp.zeros_like(m_ref)

    t_idx = lax.broadcasted_iota(jnp.int32, (L, L), 0)
    s_idx = lax.broadcasted_iota(jnp.int32, (L, L), 1)
    if reverse:
        mask, mask_t = s_idx >= t_idx, t_idx >= s_idx
    else:
        mask, mask_t = s_idx <= t_idx, t_idx <= s_idx

    for h in range(H):
        i_col = gc_ref[:, goff + h:goff + h + 1]
        f_col = gc_ref[:, goff + H + h:goff + H + h + 1]
        i_row = gr_ref[goff + h:goff + h + 1, :]
        f_row = gr_ref[goff + H + h:goff + H + h + 1, :]
        lf_col = _log_sigmoid(f_col)
        lf_row = _log_sigmoid(f_row)
        b_col = jnp.sum(jnp.where(mask, lf_row, 0.0), axis=1, keepdims=True)
        b_row = jnp.sum(jnp.where(mask_t, lf_col, 0.0), axis=0, keepdims=True)
        g = jnp.sum(lf_row, axis=1, keepdims=True)
        m_prev = m_ref[h][:, 0:1]
        dmat = jnp.where(mask, b_col - b_row + i_row, NEG)
        a = b_col + m_prev
        m_t = jnp.maximum(a, jnp.max(dmat, axis=1, keepdims=True))

        q = q_ref[:, h * dk:(h + 1) * dk]
        k = k_ref[:, h * dk:(h + 1) * dk]
        v = v_ref[:, h * dv:(h + 1) * dv]
        qk = lax.dot_general(q, k, (((1,), (1,)), ((), ())),
                             preferred_element_type=F32) * scale
        w_intra = jnp.exp(dmat - m_t) * qk
        w_inter = jnp.exp(a - m_t)
        c_state = C_ref[h]
        n_state = n_ref[h]
        q_c = jnp.dot(q, c_state.astype(BF16), preferred_element_type=F32) * scale
        num = jnp.dot(w_intra.astype(BF16), v, preferred_element_type=F32) + w_inter * q_c
        q_n = jnp.sum(q.astype(F32) * n_state, axis=1, keepdims=True) * scale
        den = jnp.sum(w_intra, axis=1, keepdims=True) + w_inter * q_n
        hh = num / jnp.maximum(jnp.abs(den), jnp.exp(-m_t))

        w_col = g - b_col + i_col
        m_new = jnp.maximum(g + m_prev, jnp.max(w_col, axis=0, keepdims=True))
        decay = jnp.exp(g + m_prev - m_new)
        kw = k.astype(F32) * jnp.exp(w_col - m_new)
        C_ref[h] = decay * c_state + lax.dot_general(
            kw.astype(BF16), v, (((0,), (0,)), ((), ())), preferred_element_type=F32)
        n_ref[h] = decay * n_state + jnp.sum(kw, axis=0, keepdims=True)
        m_ref[h] = jnp.broadcast_to(m_new, (1, LANES))

        if finish:
            tot = hh + hb_ref[:, h * dv:(h + 1) * dv]
            y = _rms(tot, gm_ref[h:h + 1, :])
            gate = _sigmoid(o_ref[:, h * dv:(h + 1) * dv].astype(F32))
            out_ref[:, h * dv:(h + 1) * dv] = (y * gate).astype(out_ref.dtype)
        else:
            out_ref[:, h * dv:(h + 1) * dv] = hh


def _mlstm(z_a, g_col, g_row, B, S, *, reverse, hb=None, g_mhead=None):
    H, dk, dv = M_HEADS, M_QK_HD, M_V_HD
    L = min(M_CHUNK, S)
    nc = S // L
    T = B * S
    finish = not reverse
    if reverse:
        cidx = lambda b, c: b * nc + (nc - 1 - c)
        ridx = lambda b, c: nc - 1 - c
    else:
        cidx = lambda b, c: b * nc + c
        ridx = lambda b, c: c
    mqk, mv = H * dk, H * dv
    in_specs = [
        pl.BlockSpec((L, mqk), lambda b, c: (cidx(b, c), 0)),
        pl.BlockSpec((L, mqk), lambda b, c: (cidx(b, c), 1)),
        pl.BlockSpec((L, mv), lambda b, c: (cidx(b, c), 1)),
        pl.BlockSpec((L, 4 * H), lambda b, c: (cidx(b, c), 0)),
        pl.BlockSpec((None, 4 * H, L), lambda b, c: (b, 0, ridx(b, c))),
    ]
    args = [z_a, z_a, z_a, g_col, g_row]
    if finish:
        in_specs += [
            pl.BlockSpec((L, mv), lambda b, c: (cidx(b, c), 0)),
            pl.BlockSpec((L, mv), lambda b, c: (cidx(b, c), 2)),
            pl.BlockSpec((H, dv), lambda b, c: (0, 0)),
        ]
        args += [hb, z_a, g_mhead.reshape(H, dv)]
    kern = functools.partial(_mlstm_kernel, reverse=reverse, finish=finish, L=L,
                             goff=2 * H if reverse else 0)
    return pl.pallas_call(
        kern, grid=(B, nc), in_specs=in_specs,
        out_specs=pl.BlockSpec((L, mv), lambda b, c: (cidx(b, c), 0)),
        out_shape=jax.ShapeDtypeStruct((T, mv), BF16 if finish else F32),
        scratch_shapes=[pltpu.VMEM((H, dk, dv), F32), pltpu.VMEM((H, 1, dk), F32),
                        pltpu.VMEM((H, 1, LANES), F32)],
        compiler_params=_cparams(("parallel", "arbitrary")),
        name="mlstm_bwd" if reverse else "mlstm_fwd")(*args)


def _rope(r, pos_ref, inv_ref):
    ang = pos_ref[...].astype(F32) * inv_ref[...]
    lane = lax.broadcasted_iota(jnp.int32, r.shape, 1)
    half = A_ROPE // 2
    cos = jnp.cos(ang)
    sin = jnp.where(lane < half, -jnp.sin(ang), jnp.sin(ang))
    swapped = jnp.where(lane < half, pltpu.roll(r, LANES - half, 1), pltpu.roll(r, half, 1))
    return r * cos + swapped * sin


def _q_prep_kernel(c_ref, w_ref, g_ref, pos_ref, inv_ref, o_ref):
    acc = jnp.dot(c_ref[...], w_ref[...], preferred_element_type=F32)
    ss = jnp.sum(acc * acc, axis=1, keepdims=True)
    y = acc * lax.rsqrt(ss / A_QK_HD + EPS) * g_ref[...] * (A_QK_HD ** -0.5)
    o_ref[:, 0:A_NOPE] = y[:, 0:A_NOPE].astype(o_ref.dtype)
    o_ref[:, A_NOPE:] = _rope(y[:, A_NOPE:], pos_ref, inv_ref).astype(o_ref.dtype)


def _q_prep(cq_n, w_uq_p, g_qn_p, pos_col, inv_row, tm=1024):
    T, R = cq_n.shape
    tm = min(tm, T)
    H = A_HEADS
    return pl.pallas_call(
        _q_prep_kernel, grid=(T // tm, H),
        in_specs=[pl.BlockSpec((tm, R), lambda i, h: (i, 0)),
                  pl.BlockSpec((None, R, A_QK_PAD), lambda i, h: (h, 0, 0)),
                  pl.BlockSpec((1, A_QK_PAD), lambda i, h: (0, 0)),
                  pl.BlockSpec((tm, 1), lambda i, h: (i, 0)),
                  pl.BlockSpec((1, LANES), lambda i, h: (0, 0))],
        out_specs=pl.BlockSpec((None, tm, A_QK_PAD), lambda i, h: (h, i, 0)),
        out_shape=jax.ShapeDtypeStruct((H, T, A_QK_PAD), BF16),
        compiler_params=_cparams(("parallel", "parallel")), name="mla_q_prep")(
            cq_n, w_uq_p, g_qn_p, pos_col, inv_row)


def _kv_prep_kernel(c_ref, w_ref, sm_ref, g_ref, pos_ref, inv_ref, kt_ref, v_ref):
    acc = jnp.dot(c_ref[...], w_ref[...], preferred_element_type=F32)
    k_nope = acc[:, 0:A_NOPE]
    lane = lax.broadcasted_iota(jnp.int32, k_nope.shape, 1)
    k_pe = jnp.where(lane < A_ROPE, sm_ref[...], 0.0)
    ss = (jnp.sum(k_nope * k_nope, axis=1, keepdims=True)
          + jnp.sum(k_pe * k_pe, axis=1, keepdims=True))
    r = lax.rsqrt(ss / A_QK_HD + EPS)
    kn = k_nope * r * g_ref[:, 0:A_NOPE]
    kr = _rope(k_pe * r * g_ref[:, A_NOPE:], pos_ref, inv_ref)
    kt_ref[0:A_NOPE, :] = kn.T.astype(kt_ref.dtype)
    kt_ref[A_NOPE:, :] = kr.T.astype(kt_ref.dtype)
    v_ref[...] = acc[:, A_NOPE:].astype(v_ref.dtype)


def _kv_prep(ckv_n, w_ukv_h, small, g_kn_p, pos_col, inv_row, tm=1024):
    T, R = ckv_n.shape
    tm = min(tm, T)
    H = A_HEADS
    return pl.pallas_call(
        _kv_prep_kernel, grid=(T // tm, H),
        in_specs=[pl.BlockSpec((tm, R), lambda i, h: (i, 0)),
                  pl.BlockSpec((None, R, A_NOPE + A_V_HD), lambda i, h: (h, 0, 0)),
                  pl.BlockSpec((tm, LANES), lambda i, h: (i, 0)),
                  pl.BlockSpec((1, A_QK_PAD), lambda i, h: (0, 0)),
                  pl.BlockSpec((tm, 1), lambda i, h: (i, 0)),
                  pl.BlockSpec((1, LANES), lambda i, h: (0, 0))],
        out_specs=[pl.BlockSpec((None, A_QK_PAD, tm), lambda i, h: (h, 0, i)),
                   pl.BlockSpec((None, tm, A_V_HD), lambda i, h: (h, i, 0))],
        out_shape=[jax.ShapeDtypeStruct((H, A_QK_PAD, T), BF16),
                   jax.ShapeDtypeStruct((H, T, A_V_HD), BF16)],
        compiler_params=_cparams(("parallel", "parallel")), name="mla_kv_prep")(
            ckv_n, w_ukv_h, small, g_kn_p, pos_col, inv_row)


def _attn_kernel(q_ref, kt_ref, v_ref, o_ref, *, tk):
    q = q_ref[...]
    S = kt_ref.shape[1]
    m = l = acc = None
    for j in range(S // tk):
        s = jnp.dot(q, kt_ref[:, j * tk:(j + 1) * tk], preferred_element_type=F32)
        mj = jnp.max(s, axis=1, keepdims=True)
        vj = v_ref[j * tk:(j + 1) * tk, :]
        if j == 0:
            m = mj
            p = jnp.exp(s - m)
            l = jnp.sum(p, axis=1, keepdims=True)
            acc = jnp.dot(p.astype(BF16), vj, preferred_element_type=F32)
        else:
            m_new = jnp.maximum(m, mj)
            alpha = jnp.exp(m - m_new)
            p = jnp.exp(s - m_new)
            l = alpha * l + jnp.sum(p, axis=1, keepdims=True)
            acc = alpha * acc + jnp.dot(p.astype(BF16), vj, preferred_element_type=F32)
            m = m_new
    o_ref[...] = (acc / l).astype(o_ref.dtype)


def _attention(q, kt, v, B, S):
    H, T, dq = q.shape
    tq, tk = min(ATT_TQ, S), min(ATT_TK, S)
    nq = S // tq
    return pl.pallas_call(
        functools.partial(_attn_kernel, tk=tk), grid=(B, H, nq),
        in_specs=[pl.BlockSpec((None, tq, dq), lambda b, h, i: (h, b * nq + i, 0)),
                  pl.BlockSpec((None, dq, S), lambda b, h, i: (h, 0, b)),
                  pl.BlockSpec((None, S, A_V_HD), lambda b, h, i: (h, b, 0))],
        out_specs=pl.BlockSpec((tq, A_V_HD), lambda b, h, i: (b * nq + i, h)),
        out_shape=jax.ShapeDtypeStruct((T, H * A_V_HD), BF16),
        compiler_params=_cparams(("parallel", "parallel", "parallel")), name="mla_attention")(
            q, kt, v)


def _merge_kernel(hm_ref, ha_ref, wm_ref, wa_ref, bm_ref, ba_ref, o_ref):
    ym = jnp.dot(hm_ref[...], wm_ref[...], preferred_element_type=F32)
    ya = jnp.dot(ha_ref[...], wa_ref[...], preferred_element_type=F32)
    o_ref[...] = (_sigmoid(bm_ref[...].astype(F32)) * ym
                  + _sigmoid(ba_ref[...].astype(F32)) * ya).astype(o_ref.dtype)


def _merge(hm, ha, w_bm, w_ba, z_a, br_off, tm=512, tn=1024):
    T, K = hm.shape
    D = w_bm.shape[1]
    tm, tn = min(tm, T), min(tn, D)
    ob = br_off // tn
    return pl.pallas_call(
        _merge_kernel, grid=(D // tn, T // tm),
        in_specs=[pl.BlockSpec((tm, K), lambda j, i: (i, 0)),
                  pl.BlockSpec((tm, K), lambda j, i: (i, 0)),
                  pl.BlockSpec((K, tn), lambda j, i: (0, j)),
                  pl.BlockSpec((K, tn), lambda j, i: (0, j)),
                  pl.BlockSpec((tm, tn), lambda j, i: (i, ob + j)),
                  pl.BlockSpec((tm, tn), lambda j, i: (i, ob + D // tn + j))],
        out_specs=pl.BlockSpec((tm, tn), lambda j, i: (i, j)),
        out_shape=jax.ShapeDtypeStruct((T, D), BF16),
        compiler_params=_cparams(("parallel", "parallel")), name="merge")(
            hm, ha, w_bm, w_ba, z_a, z_a)


def _mm_res_kernel(a_ref, b_ref, r_ref, o_ref):
    o_ref[...] = r_ref[...] + jnp.dot(a_ref[...], b_ref[...], preferred_element_type=F32)


def _matmul_residual(a, b, res, tm=512, tn=1024):
    M, K = a.shape
    N = b.shape[1]
    tm, tn = min(tm, M), min(tn, N)
    return pl.pallas_call(
        _mm_res_kernel, grid=(N // tn, M // tm),
        in_specs=[pl.BlockSpec((tm, K), lambda j, i: (i, 0)),
                  pl.BlockSpec((K, tn), lambda j, i: (0, j)),
                  pl.BlockSpec((tm, tn), lambda j, i: (i, j))],
        out_specs=pl.BlockSpec((tm, tn), lambda j, i: (i, j)),
        out_shape=jax.ShapeDtypeStruct((M, N), F32),
        compiler_params=_cparams(("parallel", "parallel")), name="out_proj")(a, b, res)


def _router_kernel(x_ref, g_ref, w_ref, b_ref, o_ref):
    xn = _rms(x_ref[...], g_ref[...]).astype(BF16)
    logits = jnp.dot(xn, w_ref[...], preferred_element_type=F32) + b_ref[...]
    lane = lax.broadcasted_iota(jnp.int32, logits.shape, 1)
    big = jnp.int32(1 << 20)
    lg = jnp.where(lane < N_GROUPS, logits, NEG)
    gmax = jnp.max(lg, axis=1, keepdims=True)
    g_idx = jnp.min(jnp.where(lg == gmax, lane, big), axis=1, keepdims=True)
    g_w = 1.0 / jnp.sum(jnp.exp(lg - gmax), axis=1, keepdims=True)
    lo = N_GROUPS + g_idx * E_PER_GROUP
    le = jnp.where((lane >= lo) & (lane < lo + E_PER_GROUP), logits, NEG)
    v1 = jnp.max(le, axis=1, keepdims=True)
    i1 = jnp.min(jnp.where(le == v1, lane, big), axis=1, keepdims=True)
    le2 = jnp.where(lane == i1, NEG, le)
    v2 = jnp.max(le2, axis=1, keepdims=True)
    i2 = jnp.min(jnp.where(le2 == v2, lane, big), axis=1, keepdims=True)
    e21 = jnp.exp(v2 - v1)
    w1 = g_w / (1.0 + e21)
    w2 = g_w * e21 / (1.0 + e21)
    o_ref[...] = jnp.where(
        lane == 0, (i1 - N_GROUPS).astype(F32),
        jnp.where(lane == 1, (i2 - N_GROUPS).astype(F32),
                  jnp.where(lane == 2, w1, jnp.where(lane == 3, w2, 0.0))))


def _router(x, g, w_r, b_r, tm=256):
    T, D = x.shape
    tm = min(tm, T)
    return pl.pallas_call(
        _router_kernel, grid=(T // tm,),
        in_specs=[pl.BlockSpec((tm, D), lambda i: (i, 0)),
                  pl.BlockSpec((1, D), lambda i: (0, 0)),
                  pl.BlockSpec((D, LANES), lambda i: (0, 0)),
                  pl.BlockSpec((1, LANES), lambda i: (0, 0))],
        out_specs=pl.BlockSpec((tm, LANES), lambda i: (i, 0)),
        out_shape=jax.ShapeDtypeStruct((T, LANES), F32),
        compiler_params=_cparams(("parallel",)), name="router")(x, g.reshape(1, D), w_r, b_r)


def _row_copy(src_hbm, row, dst, slot, sem):
    return pltpu.make_async_copy(src_hbm.at[pl.ds(row, 1), :], dst.at[pl.ds(slot, 1), :], sem)


def _expert_kernel(blk_e_ref, nused_ref, tok_ref, x_hbm, w1_ref, w3_ref, w2_ref, bw_ref, g_ref,
                   y_ref, xbuf, sem, *, blk):
    i = pl.program_id(0)

    @pl.when(i < nused_ref[0])
    def _():
        base = i * blk

        def issue(r, carry):
            _row_copy(x_hbm, tok_ref[base + r], xbuf, r, sem).start()
            return carry

        lax.fori_loop(0, blk, issue, 0)

        def drain(r, carry):
            _row_copy(x_hbm, tok_ref[base + r], xbuf, r, sem).wait()
            return carry

        lax.fori_loop(0, blk, drain, 0)
        xn = _rms(xbuf[...], g_ref[...]).astype(BF16)
        h1 = jnp.dot(xn, w1_ref[...], preferred_element_type=F32)
        h3 = jnp.dot(xn, w3_ref[...], preferred_element_type=F32)
        hdn = (h1 * _sigmoid(h1) * h3).astype(BF16)
        y_ref[...] = jnp.dot(hdn, w2_ref[...], preferred_element_type=F32) * bw_ref[...]

    @pl.when(i >= nused_ref[0])
    def _():
        y_ref[...] = jnp.zeros_like(y_ref)


def _experts(x, g, w1, w3, w2, blk_e, nused, buf_tok, buf_w, blk):
    T, D = x.shape
    E, _, De = w1.shape
    P = buf_tok.shape[0]
    nb = P // blk
    grid_spec = pltpu.PrefetchScalarGridSpec(
        num_scalar_prefetch=3, grid=(nb,),
        in_specs=[pl.BlockSpec(memory_space=pl.ANY),
                  pl.BlockSpec((None, D, De), lambda i, be, nu, tk: (be[i], 0, 0)),
                  pl.BlockSpec((None, D, De), lambda i, be, nu, tk: (be[i], 0, 0)),
                  pl.BlockSpec((None, De, D), lambda i, be, nu, tk: (be[i], 0, 0)),
                  pl.BlockSpec((blk, 1), lambda i, be, nu, tk: (i, 0)),
                  pl.BlockSpec((1, D), lambda i, be, nu, tk: (0, 0))],
        out_specs=pl.BlockSpec((blk, D), lambda i, be, nu, tk: (i, 0)),
        scratch_shapes=[pltpu.VMEM((blk, D), F32), pltpu.SemaphoreType.DMA(())])
    return pl.pallas_call(
        functools.partial(_expert_kernel, blk=blk), grid_spec=grid_spec,
        out_shape=jax.ShapeDtypeStruct((P, D), F32),
        compiler_params=_cparams(("arbitrary",)), name="experts")(
            blk_e, nused, buf_tok, x, w1, w3, w2, buf_w.reshape(P, 1), g.reshape(1, D))


def _combine_kernel(pos_ref, y_hbm, x_ref, g_ref, x2_ref, xn_ref, ybuf, sem, *, tc):
    i = pl.program_id(0)
    base = i * tc * 2

    def issue(r, carry):
        _row_copy(y_hbm, pos_ref[base + r], ybuf, r, sem).start()
        return carry

    lax.fori_loop(0, 2 * tc, issue, 0)

    def drain(r, carry):
        _row_copy(y_hbm, pos_ref[base + r], ybuf, r, sem).wait()
        return carry

    lax.fori_loop(0, 2 * tc, drain, 0)
    x2 = x_ref[...] + ybuf[0:tc, :] + ybuf[tc:2 * tc, :]
    x2_ref[...] = x2
    xn_ref[...] = _rms(x2, g_ref[...]).astype(xn_ref.dtype)


def _combine(y, x, g, pos, tc=128):
    T, D = x.shape
    tc = min(tc, T)
    grid_spec = pltpu.PrefetchScalarGridSpec(
        num_scalar_prefetch=1, grid=(T // tc,),
        in_specs=[pl.BlockSpec(memory_space=pl.ANY),
                  pl.BlockSpec((tc, D), lambda i, ps: (i, 0)),
                  pl.BlockSpec((1, D), lambda i, ps: (0, 0))],
        out_specs=[pl.BlockSpec((tc, D), lambda i, ps: (i, 0)),
                   pl.BlockSpec((tc, D), lambda i, ps: (i, 0))],
        scratch_shapes=[pltpu.VMEM((2 * tc, D), F32), pltpu.SemaphoreType.DMA(())])
    pos_tiled = pos.reshape(T // tc, tc, 2).transpose(0, 2, 1).reshape(-1)
    return pl.pallas_call(
        functools.partial(_combine_kernel, tc=tc), grid_spec=grid_spec,
        out_shape=[jax.ShapeDtypeStruct((T, D), F32), jax.ShapeDtypeStruct((T, D), BF16)],
        compiler_params=_cparams(("arbitrary",)), name="combine")(
            pos_tiled, y, x, g.reshape(1, D))


def _ple_kernel(xn_ref, wg_ref, p_ref, wp_ref, x_ref, o_ref):
    gate = _sigmoid(jnp.dot(xn_ref[...], wg_ref[...], preferred_element_type=F32))
    emb = jnp.dot(p_ref[...].astype(BF16), wp_ref[...], preferred_element_type=F32)
    o_ref[...] = x_ref[...] + gate * emb


def _ple(xn, w_pg, p, w_ple, x, tm=512, tn=1024):
    T, D = xn.shape
    Pd = p.shape[1]
    tm, tn = min(tm, T), min(tn, D)
    return pl.pallas_call(
        _ple_kernel, grid=(D // tn, T // tm),
        in_specs=[pl.BlockSpec((tm, D), lambda j, i: (i, 0)),
                  pl.BlockSpec((D, tn), lambda j, i: (0, j)),
                  pl.BlockSpec((tm, Pd), lambda j, i: (i, 0)),
                  pl.BlockSpec((Pd, tn), lambda j, i: (0, j)),
                  pl.BlockSpec((tm, tn), lambda j, i: (i, j))],
        out_specs=pl.BlockSpec((tm, tn), lambda j, i: (i, j)),
        out_shape=jax.ShapeDtypeStruct((T, D), F32),
        compiler_params=_cparams(("parallel", "parallel")), name="ple")(xn, w_pg, p, w_ple, x)


def _dispatch_tables(route, blk):
    T = route.shape[0]
    eid = route[:, 0:2].astype(jnp.int32).reshape(-1)
    wts = route[:, 2:4].reshape(-1)
    A = 2 * T
    tok = jnp.arange(A, dtype=jnp.int32) // 2
    order = jnp.argsort(eid)
    e_s = eid[order]
    counts = jnp.zeros((N_EXPERTS,), jnp.int32).at[eid].add(1)
    starts = jnp.cumsum(counts) - counts
    padded = (counts + blk - 1) // blk * blk
    pends = jnp.cumsum(padded)
    pstarts = pends - padded
    dest = pstarts[e_s] + (jnp.arange(A, dtype=jnp.int32) - starts[e_s])
    P = A + N_EXPERTS * blk
    nb = P // blk
    buf_tok = jnp.zeros((P,), jnp.int32).at[dest].set(tok[order])
    buf_w = jnp.zeros((P,), F32).at[dest].set(wts[order])
    blk_start = jnp.arange(nb, dtype=jnp.int32) * blk
    blk_e = jnp.minimum(jnp.sum(blk_start[:, None] >= pends[None, :], axis=1),
                        N_EXPERTS - 1).astype(jnp.int32)
    nused = (pends[-1] // blk).astype(jnp.int32).reshape(1)
    pos = jnp.zeros((A,), jnp.int32).at[order].set(dest).reshape(T, 2)
    return blk_e, nused, buf_tok, buf_w, pos


def _layer(x2d, p2d, pos_col, B, S, g_mix, w_in, b_mgate, g_mhead, g_cq, w_uq, g_ckv, w_ukv,
           g_qn, g_kn, w_bm, w_ba, w_out, g_ffn, w_rg, b_rg, w_re, b_re, w_e1, w_e3, w_e2,
           g_ple, w_pg, w_ple):
    T, D = x2d.shape
    H = M_HEADS
    mqk, mv = H * M_QK_HD, H * M_V_HD
    o_gate = 2 * mqk + 2 * mv
    o_cq = o_gate + 4 * H
    o_ckv = o_cq + Q_LORA
    o_kpe = o_ckv + KV_LORA
    o_br = o_kpe + A_ROPE
    w_a = jnp.concatenate([w_in[:, :o_gate], w_in[:, o_br:]], axis=1).astype(BF16)
    w_cq = w_in[:, o_cq:o_ckv].astype(BF16)
    w_ckv = w_in[:, o_ckv:o_kpe].astype(BF16)
    w_sm = jnp.concatenate(
        [w_in[:, o_kpe:o_br], w_in[:, o_gate:o_cq],
         jnp.zeros((D, LANES - A_ROPE - 4 * H), F32)], axis=1).astype(BF16)

    h = _rmsnorm(x2d, g_mix)
    z_a = _matmul(h, w_a, BF16, name="inproj_main")
    cq_n, ckv_n, small = _inproj_small(h, w_cq, w_ckv, w_sm, g_cq, g_ckv)

    g_col = small[:, A_ROPE:A_ROPE + 4 * H] + b_mgate[None, :]
    g_row = g_col.reshape(B, S, 4 * H).transpose(0, 2, 1)
    hb = _mlstm(z_a, g_col, g_row, B, S, reverse=True)
    hm = _mlstm(z_a, g_col, g_row, B, S, reverse=False, hb=hb, g_mhead=g_mhead)

    pad_q = A_QK_PAD - A_QK_HD
    w_uq_p = jnp.pad(w_uq.reshape(Q_LORA, A_HEADS, A_QK_HD).transpose(1, 0, 2),
                     ((0, 0), (0, 0), (0, pad_q))).astype(BF16)
    w_ukv_h = w_ukv.reshape(KV_LORA, A_HEADS, A_NOPE + A_V_HD).transpose(1, 0, 2).astype(BF16)
    g_qn_p = jnp.pad(g_qn, (0, pad_q)).reshape(1, A_QK_PAD)
    g_kn_p = jnp.pad(g_kn, (0, pad_q)).reshape(1, A_QK_PAD)
    inv = ROPE_THETA ** (-jnp.arange(0, A_ROPE, 2, dtype=F32) / A_ROPE)
    inv_row = jnp.concatenate([inv, inv, jnp.zeros((LANES - A_ROPE,), F32)]).reshape(1, LANES)
    q = _q_prep(cq_n, w_uq_p, g_qn_p, pos_col, inv_row)
    kt, v = _kv_prep(ckv_n, w_ukv_h, small, g_kn_p, pos_col, inv_row)
    ha = _attention(q, kt, v, B, S)

    mix = _merge(hm, ha, w_bm.astype(BF16), w_ba.astype(BF16), z_a, o_gate)
    x1 = _matmul_residual(mix, w_out.astype(BF16), x2d)

    w_r = jnp.concatenate([w_rg, w_re, jnp.zeros((D, LANES - N_GROUPS - N_EXPERTS), F32)],
                          axis=1).astype(BF16)
    b_r = jnp.concatenate([b_rg, b_re, jnp.zeros((LANES - N_GROUPS - N_EXPERTS,), F32)]
                          ).reshape(1, LANES)
    route = _router(x1, g_ffn, w_r, b_r)
    blk = min(MOE_BLK, T)
    blk_e, nused, buf_tok, buf_w, pos = _dispatch_tables(route, blk)
    y = _experts(x1, g_ffn, w_e1.astype(BF16), w_e3.astype(BF16), w_e2.astype(BF16),
                 blk_e, nused, buf_tok, buf_w, blk)
    x2, xn2 = _combine(y, x1, g_ple, pos)

    return _ple(xn2, w_pg.astype(BF16), p2d, w_ple.astype(BF16), x2)


def kernel(x, p, positions, g_mix, w_in, b_mgate, g_mhead, g_cq, w_uq, g_ckv, w_ukv, g_qn, g_kn,
           w_bm, w_ba, w_out, g_ffn, w_rg, b_rg, w_re, b_re, w_e1, w_e3, w_e2, g_ple, w_pg, w_ple):
    B, S, D = x.shape
    T = B * S
    x2d = x.reshape(T, D)
    pos_col = positions.reshape(T, 1).astype(jnp.int32)
    for l in range(p.shape[0]):
        x2d = _layer(x2d, p[l].reshape(T, -1), pos_col, B, S, g_mix[l], w_in[l], b_mgate[l],
                     g_mhead[l], g_cq[l], w_uq[l], g_ckv[l], w_ukv[l], g_qn[l], g_kn[l],
                     w_bm[l], w_ba[l], w_out[l], g_ffn[l], w_rg[l], b_rg[l], w_re[l], b_re[l],
                     w_e1[l], w_e3[l], w_e2[l], g_ple[l], w_pg[l], w_ple[l])
    return x2d.reshape(B, S, D)
```

```python
import functools

import jax
import jax.numpy as jnp
from jax import lax
from jax.experimental import pallas as pl
from jax.experimental.pallas import tpu as pltpu

F32 = jnp.float32
BF16 = jnp.bfloat16

EPS = 1e-6
M_HEADS = 8
M_QK_HD = 128
M_V_HD = 256
A_HEADS = 16
A_NOPE = 128
A_ROPE = 64
A_QK_HD = A_NOPE + A_ROPE
A_V_HD = 128
A_QK_PAD = 256
Q_LORA = 1024
KV_LORA = 512
ROPE_THETA = 10000.0
N_GROUPS = 8
E_PER_GROUP = 8
N_EXPERTS = N_GROUPS * E_PER_GROUP
D_EXPERT = 512
LANES = 128

M_CHUNK = 256
MOE_BLK = 256
ATT_TQ = 512
ATT_TK = 512
VMEM_LIMIT = 56 * 1024 * 1024
NEG = -1e30
LOG2E = 1.4426950408889634


def _cparams(sem):
    return pltpu.CompilerParams(dimension_semantics=sem, vmem_limit_bytes=VMEM_LIMIT)


def _sigmoid(x):
    return 1.0 / (1.0 + jnp.exp(-x))


def _log_sigmoid(x):
    return jnp.minimum(x, 0.0) - jnp.log(1.0 + jnp.exp(-jnp.abs(x)))


def _rms(xf, g):
    return xf * lax.rsqrt(jnp.mean(xf * xf, axis=-1, keepdims=True) + EPS) * g


def _norm_kernel(x_ref, g_ref, o_ref):
    o_ref[...] = _rms(x_ref[...], g_ref[...]).astype(o_ref.dtype)


def _rmsnorm(x, g, tm=256):
    M, D = x.shape
    tm = min(tm, M)
    return pl.pallas_call(
        _norm_kernel, grid=(M // tm,),
        in_specs=[pl.BlockSpec((tm, D), lambda i: (i, 0)),
                  pl.BlockSpec((1, D), lambda i: (0, 0))],
        out_specs=pl.BlockSpec((tm, D), lambda i: (i, 0)),
        out_shape=jax.ShapeDtypeStruct((M, D), BF16),
        compiler_params=_cparams(("parallel",)), name="rmsnorm")(x, g.reshape(1, D))


def _mm_kernel(a_ref, b_ref, o_ref):
    o_ref[...] = jnp.dot(a_ref[...], b_ref[...],
                         preferred_element_type=F32).astype(o_ref.dtype)


def _matmul(a, b, out_dtype, tm=1024, tn=1024, name="matmul"):
    M, K = a.shape
    N = b.shape[1]
    tm, tn = min(tm, M), min(tn, N)
    return pl.pallas_call(
        _mm_kernel, grid=(N // tn, M // tm),
        in_specs=[pl.BlockSpec((tm, K), lambda j, i: (i, 0)),
                  pl.BlockSpec((K, tn), lambda j, i: (0, j))],
        out_specs=pl.BlockSpec((tm, tn), lambda j, i: (i, j)),
        out_shape=jax.ShapeDtypeStruct((M, N), out_dtype),
        compiler_params=_cparams(("parallel", "parallel")), name=name)(a, b)


def _inproj_small_kernel(h_ref, wq_ref, wkv_ref, ws_ref, gq_ref, gkv_ref,
                         cq_ref, ckv_ref, sm_ref):
    h = h_ref[...]
    cq = jnp.dot(h, wq_ref[...], preferred_element_type=F32)
    cq_ref[...] = _rms(cq, gq_ref[...]).astype(cq_ref.dtype)
    ckv = jnp.dot(h, wkv_ref[...], preferred_element_type=F32)
    ckv_ref[...] = _rms(ckv, gkv_ref[...]).astype(ckv_ref.dtype)
    sm_ref[...] = jnp.dot(h, ws_ref[...], preferred_element_type=F32)


def _inproj_small(h, w_cq, w_ckv, w_sm, g_cq, g_ckv, tm=512):
    M, D = h.shape
    tm = min(tm, M)
    nq, nkv, ns = w_cq.shape[1], w_ckv.shape[1], w_sm.shape[1]
    full = lambda n: pl.BlockSpec((D, n), lambda i: (0, 0))
    row = lambda n: pl.BlockSpec((tm, n), lambda i: (i, 0))
    return pl.pallas_call(
        _inproj_small_kernel, grid=(M // tm,),
        in_specs=[row(D), full(nq), full(nkv), full(ns),
                  pl.BlockSpec((1, nq), lambda i: (0, 0)),
                  pl.BlockSpec((1, nkv), lambda i: (0, 0))],
        out_specs=[row(nq), row(nkv), row(ns)],
        out_shape=[jax.ShapeDtypeStruct((M, nq), BF16),
                   jax.ShapeDtypeStruct((M, nkv), BF16),
                   jax.ShapeDtypeStruct((M, ns), F32)],
        compiler_params=_cparams(("parallel",)), name="inproj_small")(
            h, w_cq, w_ckv, w_sm, g_cq.reshape(1, nq), g_ckv.reshape(1, nkv))


def _mlstm_kernel(*refs, reverse, finish, L, goff):
    if finish:
        (q_ref, k_ref, v_ref, gc_ref, gr_ref, hb_ref, o_ref, gm_ref,
         out_ref, C_ref, n_ref, m_ref) = refs
    else:
        q_ref, k_ref, v_ref, gc_ref, gr_ref, out_ref, C_ref, n_ref, m_ref = refs
    H, dk, dv = M_HEADS, M_QK_HD, M_V_HD
    scale = dk ** -0.5

    @pl.when(pl.program_id(1) == 0)
    def _():
        C_ref[...] = jnp.zeros_like(C_ref)
        n_ref[...] = jnp.zeros_like(n_ref)
        m_ref[...] = jnp.zeros_like(m_ref)

    t_idx = lax.broadcasted_iota(jnp.int32, (L, L), 0)
    s_idx = lax.broadcasted_iota(jnp.int32, (L, L), 1)
    if reverse:
        mask, mask_t = s_idx >= t_idx, t_idx >= s_idx
    else:
        mask, mask_t = s_idx <= t_idx, t_idx <= s_idx

    for h in range(H):
        i_col = gc_ref[:, goff + h:goff + h + 1]
        f_col = gc_ref[:, goff + H + h:goff + H + h + 1]
        i_row = gr_ref[goff + h:goff + h + 1, :]
        f_row = gr_ref[goff + H + h:goff + H + h + 1, :]
        lf_col = _log_sigmoid(f_col)
        lf_row = _log_sigmoid(f_row)
        b_col = jnp.sum(jnp.where(mask, lf_row, 0.0), axis=1, keepdims=True)
        b_row = jnp.sum(jnp.where(mask_t, lf_col, 0.0), axis=0, keepdims=True)
        g = jnp.sum(lf_row, axis=1, keepdims=True)
        m_prev = m_ref[h][:, 0:1]
        dmat = jnp.where(mask, b_col - b_row + i_row, NEG)
        a = b_col + m_prev
        m_t = jnp.maximum(a, jnp.max(dmat, axis=1, keepdims=True))

        q = q_ref[:, h * dk:(h + 1) * dk]
        k = k_ref[:, h * dk:(h + 1) * dk]
        v = v_ref[:, h * dv:(h + 1) * dv]
        qk = lax.dot_general(q, k, (((1,), (1,)), ((), ())),
                             preferred_element_type=F32) * scale
        w_intra = jnp.exp(dmat - m_t) * qk
        w_inter = jnp.exp(a - m_t)
        c_state = C_ref[h]
        n_state = n_ref[h]
        q_c = jnp.dot(q, c_state.astype(BF16), preferred_element_type=F32) * scale
        num = jnp.dot(w_intra.astype(BF16), v, preferred_element_type=F32) + w_inter * q_c
        q_n = jnp.sum(q.astype(F32) * n_state, axis=1, keepdims=True) * scale
        den = jnp.sum(w_intra, axis=1, keepdims=True) + w_inter * q_n
        hh = num / jnp.maximum(jnp.abs(den), jnp.exp(-m_t))

        w_col = g - b_col + i_col
        m_new = jnp.maximum(g + m_prev, jnp.max(w_col, axis=0, keepdims=True))
        decay = jnp.exp(g + m_prev - m_new)
        kw = k.astype(F32) * jnp.exp(w_col - m_new)
        C_ref[h] = decay * c_state + lax.dot_general(
            kw.astype(BF16), v, (((0,), (0,)), ((), ())), preferred_element_type=F32)
        n_ref[h] = decay * n_state + jnp.sum(kw, axis=0, keepdims=True)
        m_ref[h] = jnp.broadcast_to(m_new, (1, LANES))

        if finish:
            tot = hh + hb_ref[:, h * dv:(h + 1) * dv]
            y = _rms(tot, gm_ref[h:h + 1, :])
            gate = _sigmoid(o_ref[:, h * dv:(h + 1) * dv].astype(F32))
            out_ref[:, h * dv:(h + 1) * dv] = (y * gate).astype(out_ref.dtype)
        else:
            out_ref[:, h * dv:(h + 1) * dv] = hh


def _mlstm(z_a, g_col, g_row, B, S, *, reverse, hb=None, g_mhead=None):
    H, dk, dv = M_HEADS, M_QK_HD, M_V_HD
    L = min(M_CHUNK, S)
    nc = S // L
    T = B * S
    finish = not reverse
    if reverse:
        cidx = lambda b, c: b * nc + (nc - 1 - c)
        ridx = lambda b, c: nc - 1 - c
    else:
        cidx = lambda b, c: b * nc + c
        ridx = lambda b, c: c
    mqk, mv = H * dk, H * dv
    in_specs = [
        pl.BlockSpec((L, mqk), lambda b, c: (cidx(b, c), 0)),
        pl.BlockSpec((L, mqk), lambda b, c: (cidx(b, c), 1)),
        pl.BlockSpec((L, mv), lambda b, c: (cidx(b, c), 1)),
        pl.BlockSpec((L, 4 * H), lambda b, c: (cidx(b, c), 0)),
        pl.BlockSpec((None, 4 * H, L), lambda b, c: (b, 0, ridx(b, c))),
    ]
    args = [z_a, z_a, z_a, g_col, g_row]
    if finish:
        in_specs += [
            pl.BlockSpec((L, mv), lambda b, c: (cidx(b, c), 0)),
            pl.BlockSpec((L, mv), lambda b, c: (cidx(b, c), 2)),
            pl.BlockSpec((H, dv), lambda b, c: (0, 0)),
        ]
        args += [hb, z_a, g_mhead.reshape(H, dv)]
    kern = functools.partial(_mlstm_kernel, reverse=reverse, finish=finish, L=L,
                             goff=2 * H if reverse else 0)
    return pl.pallas_call(
        kern, grid=(B, nc), in_specs=in_specs,
        out_specs=pl.BlockSpec((L, mv), lambda b, c: (cidx(b, c), 0)),
        out_shape=jax.ShapeDtypeStruct((T, mv), BF16 if finish else F32),
        scratch_shapes=[pltpu.VMEM((H, dk, dv), F32), pltpu.VMEM((H, 1, dk), F32),
                        pltpu.VMEM((H, 1, LANES), F32)],
        compiler_params=_cparams(("parallel", "arbitrary")),
        name="mlstm_bwd" if reverse else "mlstm_fwd")(*args)


def _rope_table_kernel(pos_ref, inv_ref, cos_ref, sin_ref):
    ang = pos_ref[...].astype(F32) * inv_ref[...]
    lane = lax.broadcasted_iota(jnp.int32, ang.shape, 1)
    cos_ref[...] = jnp.cos(ang)
    sin_ref[...] = jnp.where(lane < A_ROPE // 2, -jnp.sin(ang), jnp.sin(ang))


def _rope_tables(pos_col, inv_row, tm=1024):
    T = pos_col.shape[0]
    tm = min(tm, T)
    return pl.pallas_call(
        _rope_table_kernel, grid=(T // tm,),
        in_specs=[pl.BlockSpec((tm, 1), lambda i: (i, 0)),
                  pl.BlockSpec((1, LANES), lambda i: (0, 0))],
        out_specs=[pl.BlockSpec((tm, LANES), lambda i: (i, 0))] * 2,
        out_shape=[jax.ShapeDtypeStruct((T, LANES), F32)] * 2,
        compiler_params=_cparams(("parallel",)), name="rope_tables")(pos_col, inv_row)


def _rope(r, cos_ref, sin_ref):
    lane = lax.broadcasted_iota(jnp.int32, r.shape, 1)
    half = A_ROPE // 2
    swapped = jnp.where(lane < half, pltpu.roll(r, LANES - half, 1), pltpu.roll(r, half, 1))
    return r * cos_ref[...] + swapped * sin_ref[...]


PREP_HEADS = 4


def _q_prep_kernel(c_ref, w_ref, g_ref, cos_ref, sin_ref, o_ref):
    c = c_ref[...]
    for h in range(w_ref.shape[0]):
        acc = jnp.dot(c, w_ref[h], preferred_element_type=F32)
        ss = jnp.sum(acc * acc, axis=1, keepdims=True)
        y = acc * lax.rsqrt(ss / A_QK_HD + EPS) * g_ref[...] * (A_QK_HD ** -0.5 * LOG2E)
        o_ref[h, :, 0:A_NOPE] = y[:, 0:A_NOPE].astype(o_ref.dtype)
        o_ref[h, :, A_NOPE:] = _rope(y[:, A_NOPE:], cos_ref, sin_ref).astype(o_ref.dtype)


def _q_prep(cq_n, w_uq_p, g_qn_p, cos_t, sin_t, tm=1024):
    T, R = cq_n.shape
    tm = min(tm, T)
    H, hb = A_HEADS, PREP_HEADS
    return pl.pallas_call(
        _q_prep_kernel, grid=(T // tm, H // hb),
        in_specs=[pl.BlockSpec((tm, R), lambda i, h: (i, 0)),
                  pl.BlockSpec((hb, R, A_QK_PAD), lambda i, h: (h, 0, 0)),
                  pl.BlockSpec((1, A_QK_PAD), lambda i, h: (0, 0)),
                  pl.BlockSpec((tm, LANES), lambda i, h: (i, 0)),
                  pl.BlockSpec((tm, LANES), lambda i, h: (i, 0))],
        out_specs=pl.BlockSpec((hb, tm, A_QK_PAD), lambda i, h: (h, i, 0)),
        out_shape=jax.ShapeDtypeStruct((H, T, A_QK_PAD), BF16),
        compiler_params=_cparams(("parallel", "parallel")), name="mla_q_prep")(
            cq_n, w_uq_p, g_qn_p, cos_t, sin_t)


def _kv_prep_kernel(c_ref, w_ref, sm_ref, g_ref, cos_ref, sin_ref, kt_ref, v_ref):
    c = c_ref[...]
    lane = lax.broadcasted_iota(jnp.int32, sm_ref.shape, 1)
    k_pe = jnp.where(lane < A_ROPE, sm_ref[...], 0.0)
    ss_pe = jnp.sum(k_pe * k_pe, axis=1, keepdims=True)
    for h in range(w_ref.shape[0]):
        acc = jnp.dot(c, w_ref[h], preferred_element_type=F32)
        k_nope = acc[:, 0:A_NOPE]
        ss = jnp.sum(k_nope * k_nope, axis=1, keepdims=True) + ss_pe
        r = lax.rsqrt(ss / A_QK_HD + EPS)
        kn = k_nope * r * g_ref[:, 0:A_NOPE]
        kr = _rope(k_pe * r * g_ref[:, A_NOPE:], cos_ref, sin_ref)
        kt_ref[h, 0:A_NOPE, :] = kn.T.astype(kt_ref.dtype)
        kt_ref[h, A_NOPE:, :] = kr.T.astype(kt_ref.dtype)
        v_ref[h, :, 0:A_V_HD] = acc[:, A_NOPE:].astype(v_ref.dtype)
        v_ref[h, :, A_V_HD:] = jnp.ones((acc.shape[0], A_V_HD), v_ref.dtype)


def _kv_prep(ckv_n, w_ukv_h, small, g_kn_p, cos_t, sin_t, tm=1024):
    T, R = ckv_n.shape
    tm = min(tm, T)
    H, hb = A_HEADS, PREP_HEADS
    return pl.pallas_call(
        _kv_prep_kernel, grid=(T // tm, H // hb),
        in_specs=[pl.BlockSpec((tm, R), lambda i, h: (i, 0)),
                  pl.BlockSpec((hb, R, A_NOPE + A_V_HD), lambda i, h: (h, 0, 0)),
                  pl.BlockSpec((tm, LANES), lambda i, h: (i, 0)),
                  pl.BlockSpec((1, A_QK_PAD), lambda i, h: (0, 0)),
                  pl.BlockSpec((tm, LANES), lambda i, h: (i, 0)),
                  pl.BlockSpec((tm, LANES), lambda i, h: (i, 0))],
        out_specs=[pl.BlockSpec((hb, A_QK_PAD, tm), lambda i, h: (h, 0, i)),
                   pl.BlockSpec((hb, tm, 2 * A_V_HD), lambda i, h: (h, i, 0))],
        out_shape=[jax.ShapeDtypeStruct((H, A_QK_PAD, T), BF16),
                   jax.ShapeDtypeStruct((H, T, 2 * A_V_HD), BF16)],
        compiler_params=_cparams(("parallel", "parallel")), name="mla_kv_prep")(
            ckv_n, w_ukv_h, small, g_kn_p, cos_t, sin_t)


def _attn_kernel(q_ref, kt_ref, v_ref, o_ref, *, tk):
    q = q_ref[...]
    S = kt_ref.shape[1]
    m = acc = None
    for j in range(S // tk):
        s = jnp.dot(q, kt_ref[:, j * tk:(j + 1) * tk], preferred_element_type=F32)
        mj = jnp.max(s, axis=1, keepdims=True)
        vj = v_ref[j * tk:(j + 1) * tk, :]
        if j == 0:
            m = mj
            acc = jnp.dot(jnp.exp2(s - m).astype(BF16), vj, preferred_element_type=F32)
        else:
            m_new = jnp.maximum(m, mj)
            acc = (jnp.exp2(m - m_new) * acc
                   + jnp.dot(jnp.exp2(s - m_new).astype(BF16), vj, preferred_element_type=F32))
            m = m_new
    o_ref[...] = (acc[:, 0:A_V_HD] / acc[:, A_V_HD:A_V_HD + 1]).astype(o_ref.dtype)


def _attention(q, kt, v, B, S):
    H, T, dq = q.shape
    tq, tk = min(ATT_TQ, S), min(ATT_TK, S)
    nq = S // tq
    return pl.pallas_call(
        functools.partial(_attn_kernel, tk=tk), grid=(B, H, nq),
        in_specs=[pl.BlockSpec((None, tq, dq), lambda b, h, i: (h, b * nq + i, 0)),
                  pl.BlockSpec((None, dq, S), lambda b, h, i: (h, 0, b)),
                  pl.BlockSpec((None, S, 2 * A_V_HD), lambda b, h, i: (h, b, 0))],
        out_specs=pl.BlockSpec((tq, A_V_HD), lambda b, h, i: (b * nq + i, h)),
        out_shape=jax.ShapeDtypeStruct((T, H * A_V_HD), BF16),
        compiler_params=_cparams(("parallel", "parallel", "parallel")), name="mla_attention")(
            q, kt, v)


def _merge_kernel(hm_ref, ha_ref, wm_ref, wa_ref, bm_ref, ba_ref, o_ref):
    ym = jnp.dot(hm_ref[...], wm_ref[...], preferred_element_type=F32)
    ya = jnp.dot(ha_ref[...], wa_ref[...], preferred_element_type=F32)
    o_ref[...] = (_sigmoid(bm_ref[...].astype(F32)) * ym
                  + _sigmoid(ba_ref[...].astype(F32)) * ya).astype(o_ref.dtype)


def _merge(hm, ha, w_bm, w_ba, z_a, br_off, tm=512, tn=1024):
    T, K = hm.shape
    D = w_bm.shape[1]
    tm, tn = min(tm, T), min(tn, D)
    ob = br_off // tn
    return pl.pallas_call(
        _merge_kernel, grid=(D // tn, T // tm),
        in_specs=[pl.BlockSpec((tm, K), lambda j, i: (i, 0)),
                  pl.BlockSpec((tm, K), lambda j, i: (i, 0)),
                  pl.BlockSpec((K, tn), lambda j, i: (0, j)),
                  pl.BlockSpec((K, tn), lambda j, i: (0, j)),
                  pl.BlockSpec((tm, tn), lambda j, i: (i, ob + j)),
                  pl.BlockSpec((tm, tn), lambda j, i: (i, ob + D // tn + j))],
        out_specs=pl.BlockSpec((tm, tn), lambda j, i: (i, j)),
        out_shape=jax.ShapeDtypeStruct((T, D), BF16),
        compiler_params=_cparams(("parallel", "parallel")), name="merge")(
            hm, ha, w_bm, w_ba, z_a, z_a)


def _mm_res_kernel(a_ref, b_ref, r_ref, o_ref):
    o_ref[...] = r_ref[...] + jnp.dot(a_ref[...], b_ref[...], preferred_element_type=F32)


def _matmul_residual(a, b, res, tm=512, tn=1024):
    M, K = a.shape
    N = b.shape[1]
    tm, tn = min(tm, M), min(tn, N)
    return pl.pallas_call(
        _mm_res_kernel, grid=(N // tn, M // tm),
        in_specs=[pl.BlockSpec((tm, K), lambda j, i: (i, 0)),
                  pl.BlockSpec((K, tn), lambda j, i: (0, j)),
                  pl.BlockSpec((tm, tn), lambda j, i: (i, j))],
        out_specs=pl.BlockSpec((tm, tn), lambda j, i: (i, j)),
        out_shape=jax.ShapeDtypeStruct((M, N), F32),
        compiler_params=_cparams(("parallel", "parallel")), name="out_proj")(a, b, res)


def _router_kernel(x_ref, g_ref, w_ref, b_ref, o_ref, xn3_ref):
    xf = _rms(x_ref[...], g_ref[...])
    nch = xf.shape[1] // LANES
    tm = xf.shape[0]
    pitch = xn3_ref.shape[0] // tm
    for c in range(pitch):
        xn3_ref[pl.ds(c, tm, stride=pitch), :] = (
            xf[:, c * LANES:(c + 1) * LANES] if c < nch else jnp.zeros((tm, LANES), F32))
    xn = xf.astype(BF16)
    logits = jnp.dot(xn, w_ref[...], preferred_element_type=F32) + b_ref[...]
    lane = lax.broadcasted_iota(jnp.int32, logits.shape, 1)
    big = jnp.int32(1 << 20)
    lg = jnp.where(lane < N_GROUPS, logits, NEG)
    gmax = jnp.max(lg, axis=1, keepdims=True)
    g_idx = jnp.min(jnp.where(lg == gmax, lane, big), axis=1, keepdims=True)
    g_w = 1.0 / jnp.sum(jnp.exp(lg - gmax), axis=1, keepdims=True)
    lo = N_GROUPS + g_idx * E_PER_GROUP
    le = jnp.where((lane >= lo) & (lane < lo + E_PER_GROUP), logits, NEG)
    v1 = jnp.max(le, axis=1, keepdims=True)
    i1 = jnp.min(jnp.where(le == v1, lane, big), axis=1, keepdims=True)
    le2 = jnp.where(lane == i1, NEG, le)
    v2 = jnp.max(le2, axis=1, keepdims=True)
    i2 = jnp.min(jnp.where(le2 == v2, lane, big), axis=1, keepdims=True)
    e21 = jnp.exp(v2 - v1)
    w1 = g_w / (1.0 + e21)
    w2 = g_w * e21 / (1.0 + e21)
    o_ref[...] = jnp.where(
        lane == 0, (i1 - N_GROUPS).astype(F32),
        jnp.where(lane == 1, (i2 - N_GROUPS).astype(F32),
                  jnp.where(lane == 2, w1, jnp.where(lane == 3, w2, 0.0))))


def _router(x, g, w_r, b_r, tm=256):
    T, D = x.shape
    tm = min(tm, T)
    pitch = _row_pitch(D // LANES)
    return pl.pallas_call(
        _router_kernel, grid=(T // tm,),
        in_specs=[pl.BlockSpec((tm, D), lambda i: (i, 0)),
                  pl.BlockSpec((1, D), lambda i: (0, 0)),
                  pl.BlockSpec((D, LANES), lambda i: (0, 0)),
                  pl.BlockSpec((1, LANES), lambda i: (0, 0))],
        out_specs=[pl.BlockSpec((tm, LANES), lambda i: (i, 0)),
                   pl.BlockSpec((tm * pitch, LANES), lambda i: (i, 0))],
        out_shape=[jax.ShapeDtypeStruct((T, LANES), F32),
                   jax.ShapeDtypeStruct((T * pitch, LANES), F32)],
        compiler_params=_cparams(("parallel",)), name="router")(x, g.reshape(1, D), w_r, b_r)


def _row_pitch(nch):
    return (nch + 7) // 8 * 8 + 8


def _gather_rows(start_one, wait_one, n, unroll=8):
    def start():
        def body(r, carry):
            start_one(r)
            return carry
        lax.fori_loop(0, n, body, 0, unroll=unroll)

    def wait():
        def body(r, carry):
            wait_one(r)
            return carry
        lax.fori_loop(0, n, body, 0, unroll=unroll)
    return start, wait


def _expert_kernel(blk_e_ref, nused_ref, src_ref, tok_ref, x_hbm, w1_ref, w3_ref, w2_ref,
                   y_ref, xbuf, sem, *, blk, n_assign, nch, pitch, ych):
    i = pl.program_id(0)
    nused = nused_ref[0]

    def row_copy(step, r, slot):
        idx = jnp.minimum(src_ref[step] + r, n_assign - 1)
        return pltpu.make_async_copy(
            x_hbm.at[pl.ds(pl.multiple_of(tok_ref[idx] * pitch, 8), nch), :],
            xbuf.at[slot, pl.ds(pl.multiple_of(r * pitch, 8), nch), :], sem.at[slot])

    def gather(step, slot):
        return _gather_rows(lambda r: row_copy(step, r, slot).start(),
                            lambda r: row_copy(step, r, slot).wait(), blk)

    @pl.when(i == 0)
    def _():
        gather(0, 0)[0]()

    @pl.when(i + 1 < nused)
    def _():
        gather(i + 1, (i + 1) % 2)[0]()

    @pl.when(i < nused)
    def _():
        slot = i % 2
        gather(i, slot)[1]()
        xn = jnp.concatenate([xbuf[slot, pl.ds(c, blk, stride=pitch), :].astype(BF16)
                              for c in range(nch)], axis=1)
        h1 = jnp.dot(xn, w1_ref[...], preferred_element_type=F32)
        h3 = jnp.dot(xn, w3_ref[...], preferred_element_type=F32)
        hdn = (h1 * _sigmoid(h1) * h3).astype(BF16)
        for j in range(nch * LANES // ych):
            yj = jnp.dot(hdn, w2_ref[:, j * ych:(j + 1) * ych], preferred_element_type=F32)
            for c in range(ych // LANES):
                y_ref[pl.ds(j * (ych // LANES) + c, blk, stride=pitch), :] = (
                    yj[:, c * LANES:(c + 1) * LANES])
        for c in range(nch, pitch):
            y_ref[pl.ds(c, blk, stride=pitch), :] = jnp.zeros((blk, LANES), F32)

    @pl.when(i >= nused)
    def _():
        y_ref[...] = jnp.zeros_like(y_ref)


def _experts(xn3, w1, w3, w2, blk_e, nused, src_start, tok_sorted, blk):
    E, D, De = w1.shape
    nch = D // LANES
    pitch = _row_pitch(nch)
    nb = blk_e.shape[0]
    n_assign = tok_sorted.shape[0]
    wmap = lambda i, be, nu, sr, tk: (be[i], 0, 0)
    grid_spec = pltpu.PrefetchScalarGridSpec(
        num_scalar_prefetch=4, grid=(nb,),
        in_specs=[pl.BlockSpec(memory_space=pl.ANY),
                  pl.BlockSpec((None, D, De), wmap),
                  pl.BlockSpec((None, D, De), wmap),
                  pl.BlockSpec((None, De, D), wmap)],
        out_specs=pl.BlockSpec((blk * pitch, LANES), lambda i, be, nu, sr, tk: (i, 0)),
        scratch_shapes=[pltpu.VMEM((2, blk * pitch, LANES), F32), pltpu.SemaphoreType.DMA((2,))])
    return pl.pallas_call(
        functools.partial(_expert_kernel, blk=blk, n_assign=n_assign, nch=nch, pitch=pitch,
                          ych=min(1024, D)),
        grid_spec=grid_spec,
        out_shape=jax.ShapeDtypeStruct((nb * blk * pitch, LANES), F32),
        compiler_params=_cparams(("arbitrary",)), name="experts")(
            blk_e, nused, src_start, tok_sorted, xn3, w1, w3, w2)


def _combine_kernel(pos_ref, y_hbm, x_ref, rt_ref, g_ref, x2_ref, xn_ref, ybuf, sem, *, tc, nch,
                    pitch):
    i = pl.program_id(0)
    n = pl.num_programs(0)

    def row_copy(step, r, slot):
        return pltpu.make_async_copy(
            y_hbm.at[pl.ds(pl.multiple_of(pos_ref[step * 2 * tc + r] * pitch, 8), nch), :],
            ybuf.at[slot, pl.ds(pl.multiple_of(r * pitch, 8), nch), :], sem.at[slot])

    def gather(step, slot):
        return _gather_rows(lambda r: row_copy(step, r, slot).start(),
                            lambda r: row_copy(step, r, slot).wait(), 2 * tc)

    @pl.when(i == 0)
    def _():
        gather(0, 0)[0]()

    @pl.when(i + 1 < n)
    def _():
        gather(i + 1, (i + 1) % 2)[0]()

    slot = i % 2
    gather(i, slot)[1]()
    y0 = jnp.concatenate([ybuf[slot, pl.ds(c, tc, stride=pitch), :] for c in range(nch)], axis=1)
    y1 = jnp.concatenate([ybuf[slot, pl.ds(tc * pitch + c, tc, stride=pitch), :]
                          for c in range(nch)], axis=1)
    x2 = x_ref[...] + rt_ref[:, 2:3] * y0 + rt_ref[:, 3:4] * y1
    x2_ref[...] = x2
    xn_ref[...] = _rms(x2, g_ref[...]).astype(xn_ref.dtype)


def _combine(y, x, route, g, pos, tc=128):
    T, D = x.shape
    tc = min(tc, T)
    nch = D // LANES
    pitch = _row_pitch(nch)
    grid_spec = pltpu.PrefetchScalarGridSpec(
        num_scalar_prefetch=1, grid=(T // tc,),
        in_specs=[pl.BlockSpec(memory_space=pl.ANY),
                  pl.BlockSpec((tc, D), lambda i, ps: (i, 0)),
                  pl.BlockSpec((tc, LANES), lambda i, ps: (i, 0)),
                  pl.BlockSpec((1, D), lambda i, ps: (0, 0))],
        out_specs=[pl.BlockSpec((tc, D), lambda i, ps: (i, 0)),
                   pl.BlockSpec((tc, D), lambda i, ps: (i, 0))],
        scratch_shapes=[pltpu.VMEM((2, 2 * tc * pitch, LANES), F32),
                        pltpu.SemaphoreType.DMA((2,))])
    pos_tiled = pos.reshape(T // tc, tc, 2).transpose(0, 2, 1).reshape(-1)
    return pl.pallas_call(
        functools.partial(_combine_kernel, tc=tc, nch=nch, pitch=pitch), grid_spec=grid_spec,
        out_shape=[jax.ShapeDtypeStruct((T, D), F32), jax.ShapeDtypeStruct((T, D), BF16)],
        compiler_params=_cparams(("arbitrary",)), name="combine")(
            pos_tiled, y, x, route, g.reshape(1, D))


def _ple_kernel(xn_ref, wg_ref, p_ref, wp_ref, x_ref, o_ref):
    gate = _sigmoid(jnp.dot(xn_ref[...], wg_ref[...], preferred_element_type=F32))
    emb = jnp.dot(p_ref[...].astype(BF16), wp_ref[...], preferred_element_type=F32)
    o_ref[...] = x_ref[...] + gate * emb


def _ple(xn, w_pg, p, w_ple, x, tm=512, tn=1024):
    T, D = xn.shape
    Pd = p.shape[1]
    tm, tn = min(tm, T), min(tn, D)
    return pl.pallas_call(
        _ple_kernel, grid=(D // tn, T // tm),
        in_specs=[pl.BlockSpec((tm, D), lambda j, i: (i, 0)),
                  pl.BlockSpec((D, tn), lambda j, i: (0, j)),
                  pl.BlockSpec((tm, Pd), lambda j, i: (i, 0)),
                  pl.BlockSpec((Pd, tn), lambda j, i: (0, j)),
                  pl.BlockSpec((tm, tn), lambda j, i: (i, j))],
        out_specs=pl.BlockSpec((tm, tn), lambda j, i: (i, j)),
        out_shape=jax.ShapeDtypeStruct((T, D), F32),
        compiler_params=_cparams(("parallel", "parallel")), name="ple")(xn, w_pg, p, w_ple, x)


def _dispatch_tables(route, blk):
    T = route.shape[0]
    A = 2 * T
    eid = route[:, 0:2].astype(jnp.int32).reshape(-1)
    iota = jnp.arange(A, dtype=jnp.int32)
    experts = jnp.arange(N_EXPERTS, dtype=jnp.int32)
    _, order = lax.sort((eid, iota), num_keys=1)
    _, inv = lax.sort((order, iota), num_keys=1)
    onehot = eid[:, None] == experts[None, :]
    counts = jnp.sum(onehot, axis=0, dtype=jnp.int32)
    starts = jnp.cumsum(counts) - counts
    nblk = (counts + blk - 1) // blk
    bends = jnp.cumsum(nblk)
    bstarts = bends - nblk
    nb = (A + N_EXPERTS * blk) // blk
    b = jnp.arange(nb, dtype=jnp.int32)
    blk_e = jnp.minimum(jnp.sum(b[:, None] >= bends[None, :], axis=1),
                        N_EXPERTS - 1).astype(jnp.int32)
    shift = starts - bstarts * blk
    src_start = b * blk + jnp.sum(
        jnp.where(blk_e[:, None] == experts[None, :], shift[None, :], 0), axis=1)
    nused = bends[-1].astype(jnp.int32).reshape(1)
    pos = inv - jnp.sum(jnp.where(onehot, shift[None, :], 0), axis=1)
    return blk_e, nused, src_start.astype(jnp.int32), order // 2, pos.reshape(T, 2)


def _layer(x2d, p2d, pos_col, B, S, g_mix, w_in, b_mgate, g_mhead, g_cq, w_uq, g_ckv, w_ukv,
           g_qn, g_kn, w_bm, w_ba, w_out, g_ffn, w_rg, b_rg, w_re, b_re, w_e1, w_e3, w_e2,
           g_ple, w_pg, w_ple):
    T, D = x2d.shape
    H = M_HEADS
    mqk, mv = H * M_QK_HD, H * M_V_HD
    o_gate = 2 * mqk + 2 * mv
    o_cq = o_gate + 4 * H
    o_ckv = o_cq + Q_LORA
    o_kpe = o_ckv + KV_LORA
    o_br = o_kpe + A_ROPE
    w_a = jnp.concatenate([w_in[:, :o_gate], w_in[:, o_br:]], axis=1).astype(BF16)
    w_cq = w_in[:, o_cq:o_ckv].astype(BF16)
    w_ckv = w_in[:, o_ckv:o_kpe].astype(BF16)
    w_sm = jnp.concatenate(
        [w_in[:, o_kpe:o_br], w_in[:, o_gate:o_cq],
         jnp.zeros((D, LANES - A_ROPE - 4 * H), F32)], axis=1).astype(BF16)

    h = _rmsnorm(x2d, g_mix)
    z_a = _matmul(h, w_a, BF16, name="inproj_main")
    cq_n, ckv_n, small = _inproj_small(h, w_cq, w_ckv, w_sm, g_cq, g_ckv)

    g_col = small[:, A_ROPE:A_ROPE + 4 * H] + b_mgate[None, :]
    g_row = g_col.reshape(B, S, 4 * H).transpose(0, 2, 1)
    hb = _mlstm(z_a, g_col, g_row, B, S, reverse=True)
    hm = _mlstm(z_a, g_col, g_row, B, S, reverse=False, hb=hb, g_mhead=g_mhead)

    pad_q = A_QK_PAD - A_QK_HD
    w_uq_p = jnp.pad(w_uq.reshape(Q_LORA, A_HEADS, A_QK_HD).transpose(1, 0, 2),
                     ((0, 0), (0, 0), (0, pad_q))).astype(BF16)
    w_ukv_h = w_ukv.reshape(KV_LORA, A_HEADS, A_NOPE + A_V_HD).transpose(1, 0, 2).astype(BF16)
    g_qn_p = jnp.pad(g_qn, (0, pad_q)).reshape(1, A_QK_PAD)
    g_kn_p = jnp.pad(g_kn, (0, pad_q)).reshape(1, A_QK_PAD)
    inv = ROPE_THETA ** (-jnp.arange(0, A_ROPE, 2, dtype=F32) / A_ROPE)
    inv_row = jnp.concatenate([inv, inv, jnp.zeros((LANES - A_ROPE,), F32)]).reshape(1, LANES)
    cos_t, sin_t = _rope_tables(pos_col, inv_row)
    q = _q_prep(cq_n, w_uq_p, g_qn_p, cos_t, sin_t)
    kt, v = _kv_prep(ckv_n, w_ukv_h, small, g_kn_p, cos_t, sin_t)
    ha = _attention(q, kt, v, B, S)

    mix = _merge(hm, ha, w_bm.astype(BF16), w_ba.astype(BF16), z_a, o_gate)
    x1 = _matmul_residual(mix, w_out.astype(BF16), x2d)

    w_r = jnp.concatenate([w_rg, w_re, jnp.zeros((D, LANES - N_GROUPS - N_EXPERTS), F32)],
                          axis=1).astype(BF16)
    b_r = jnp.concatenate([b_rg, b_re, jnp.zeros((LANES - N_GROUPS - N_EXPERTS,), F32)]
                          ).reshape(1, LANES)
    route, xn3 = _router(x1, g_ffn, w_r, b_r)
    blk = min(MOE_BLK, T)
    blk_e, nused, src_start, tok_sorted, pos = _dispatch_tables(route, blk)
    y = _experts(xn3, w_e1.astype(BF16), w_e3.astype(BF16), w_e2.astype(BF16),
                 blk_e, nused, src_start, tok_sorted, blk)
    x2, xn2 = _combine(y, x1, route, g_ple, pos)

    return _ple(xn2, w_pg.astype(BF16), p2d, w_ple.astype(BF16), x2)


def kernel(x, p, positions, g_mix, w_in, b_mgate, g_mhead, g_cq, w_uq, g_ckv, w_ukv, g_qn, g_kn,
           w_bm, w_ba, w_out, g_ffn, w_rg, b_rg, w_re, b_re, w_e1, w_e3, w_e2, g_ple, w_pg, w_ple):
    B, S, D = x.shape
    T = B * S
    x2d = x.reshape(T, D)
    pos_col = positions.reshape(T, 1).astype(jnp.int32)
    for l in range(p.shape[0]):
        x2d = _layer(x2d, p[l].reshape(T, -1), pos_col, B, S, g_mix[l], w_in[l], b_mgate[l],
                     g_mhead[l], g_cq[l], w_uq[l], g_ckv[l], w_ukv[l], g_qn[l], g_kn[l],
                     w_bm[l], w_ba[l], w_out[l], g_ffn[l], w_rg[l], b_rg[l], w_re[l], b_re[l],
                     w_e1[l], w_e3[l], w_e2[l], g_ple[l], w_pg[l], w_ple[l])
    return x2d.reshape(B, S, D)
```

```python
import functools
import math

import jax
import jax.numpy as jnp
from jax import lax
from jax.experimental import pallas as pl
from jax.experimental.pallas import tpu as pltpu

F32 = jnp.float32
BF16 = jnp.bfloat16

EPS = 1e-6
M_HEADS = 8
M_QK_HD = 128
M_V_HD = 256
A_HEADS = 16
A_NOPE = 128
A_ROPE = 64
A_QK_HD = A_NOPE + A_ROPE
A_V_HD = 128
A_QK_PAD = 256
Q_LORA = 1024
KV_LORA = 512
ROPE_THETA = 10000.0
N_GROUPS = 8
E_PER_GROUP = 8
N_EXPERTS = N_GROUPS * E_PER_GROUP
D_EXPERT = 512
LANES = 128

M_CHUNK = 256
MOE_BLK = 256
ATT_TQ = 512
ATT_TK = 512
VMEM_LIMIT = 56 * 1024 * 1024
NEG = -1e30
LOG2E = 1.4426950408889634


def _cparams(sem):
    return pltpu.CompilerParams(dimension_semantics=sem, vmem_limit_bytes=VMEM_LIMIT)


def _sigmoid(x):
    return 1.0 / (1.0 + jnp.exp(-x))


def _log_sigmoid(x):
    return jnp.minimum(x, 0.0) - jnp.log(1.0 + jnp.exp(-jnp.abs(x)))


def _rms(xf, g):
    return xf * lax.rsqrt(jnp.mean(xf * xf, axis=-1, keepdims=True) + EPS) * g


def _norm_kernel(x_ref, g_ref, o_ref):
    o_ref[...] = _rms(x_ref[...], g_ref[...]).astype(o_ref.dtype)


def _rmsnorm(x, g, tm=256):
    M, D = x.shape
    tm = min(tm, M)
    return pl.pallas_call(
        _norm_kernel, grid=(M // tm,),
        in_specs=[pl.BlockSpec((tm, D), lambda i: (i, 0)),
                  pl.BlockSpec((1, D), lambda i: (0, 0))],
        out_specs=pl.BlockSpec((tm, D), lambda i: (i, 0)),
        out_shape=jax.ShapeDtypeStruct((M, D), BF16),
        compiler_params=_cparams(("parallel",)), name="rmsnorm")(x, g.reshape(1, D))


def _mm_kernel(a_ref, b_ref, o_ref):
    o_ref[...] = jnp.dot(a_ref[...], b_ref[...],
                         preferred_element_type=F32).astype(o_ref.dtype)


def _matmul(a, b, out_dtype, tm=1024, tn=1024, name="matmul"):
    M, K = a.shape
    N = b.shape[1]
    tm, tn = min(tm, M), min(tn, N)
    return pl.pallas_call(
        _mm_kernel, grid=(N // tn, M // tm),
        in_specs=[pl.BlockSpec((tm, K), lambda j, i: (i, 0)),
                  pl.BlockSpec((K, tn), lambda j, i: (0, j))],
        out_specs=pl.BlockSpec((tm, tn), lambda j, i: (i, j)),
        out_shape=jax.ShapeDtypeStruct((M, N), out_dtype),
        compiler_params=_cparams(("parallel", "parallel")), name=name)(a, b)


def _inproj_small_kernel(h_ref, wq_ref, wkv_ref, ws_ref, gq_ref, gkv_ref,
                         cq_ref, ckv_ref, sm_ref):
    h = h_ref[...]
    cq = jnp.dot(h, wq_ref[...], preferred_element_type=F32)
    cq_ref[...] = _rms(cq, gq_ref[...]).astype(cq_ref.dtype)
    ckv = jnp.dot(h, wkv_ref[...], preferred_element_type=F32)
    ckv_ref[...] = _rms(ckv, gkv_ref[...]).astype(ckv_ref.dtype)
    sm_ref[...] = jnp.dot(h, ws_ref[...], preferred_element_type=F32)


def _inproj_small(h, w_cq, w_ckv, w_sm, g_cq, g_ckv, tm=512):
    M, D = h.shape
    tm = min(tm, M)
    nq, nkv, ns = w_cq.shape[1], w_ckv.shape[1], w_sm.shape[1]
    full = lambda n: pl.BlockSpec((D, n), lambda i: (0, 0))
    row = lambda n: pl.BlockSpec((tm, n), lambda i: (i, 0))
    return pl.pallas_call(
        _inproj_small_kernel, grid=(M // tm,),
        in_specs=[row(D), full(nq), full(nkv), full(ns),
                  pl.BlockSpec((1, nq), lambda i: (0, 0)),
                  pl.BlockSpec((1, nkv), lambda i: (0, 0))],
        out_specs=[row(nq), row(nkv), row(ns)],
        out_shape=[jax.ShapeDtypeStruct((M, nq), BF16),
                   jax.ShapeDtypeStruct((M, nkv), BF16),
                   jax.ShapeDtypeStruct((M, ns), F32)],
        compiler_params=_cparams(("parallel",)), name="inproj_small")(
            h, w_cq, w_ckv, w_sm, g_cq.reshape(1, nq), g_ckv.reshape(1, nkv))


def _mlstm_kernel(*refs, reverse, finish, L, goff):
    if finish:
        (q_ref, k_ref, v_ref, gc_ref, gr_ref, hb_ref, o_ref, gm_ref,
         out_ref, C_ref, n_ref, m_ref) = refs
    else:
        q_ref, k_ref, v_ref, gc_ref, gr_ref, out_ref, C_ref, n_ref, m_ref = refs
    H, dk, dv = M_HEADS, M_QK_HD, M_V_HD
    log_scale = -0.5 * math.log(dk)

    @pl.when(pl.program_id(1) == 0)
    def _():
        C_ref[...] = jnp.zeros_like(C_ref)
        n_ref[...] = jnp.zeros_like(n_ref)
        m_ref[...] = jnp.zeros_like(m_ref)

    t_idx = lax.broadcasted_iota(jnp.int32, (L, L), 0)
    s_idx = lax.broadcasted_iota(jnp.int32, (L, L), 1)
    if reverse:
        mask, mask_t = s_idx >= t_idx, t_idx >= s_idx
    else:
        mask, mask_t = s_idx <= t_idx, t_idx <= s_idx

    for h in range(H):
        i_col = gc_ref[:, goff + h:goff + h + 1]
        f_col = gc_ref[:, goff + H + h:goff + H + h + 1]
        i_row = gr_ref[goff + h:goff + h + 1, :]
        f_row = gr_ref[goff + H + h:goff + H + h + 1, :]
        lf_col = _log_sigmoid(f_col)
        lf_row = _log_sigmoid(f_row)
        b_col = jnp.sum(jnp.where(mask, lf_row, 0.0), axis=1, keepdims=True)
        b_row = jnp.sum(jnp.where(mask_t, lf_col, 0.0), axis=0, keepdims=True)
        g = jnp.sum(lf_row, axis=1, keepdims=True)
        m_prev = m_ref[h][:, 0:1]
        dmat = jnp.where(mask, b_col - b_row + i_row, NEG)
        a = b_col + m_prev
        m_t = jnp.maximum(a, jnp.max(dmat, axis=1, keepdims=True))

        q = q_ref[:, h * dk:(h + 1) * dk]
        k = k_ref[:, h * dk:(h + 1) * dk]
        v = v_ref[:, h * dv:(h + 1) * dv]
        qk = lax.dot_general(q, k, (((1,), (1,)), ((), ())),
                             preferred_element_type=F32)
        m_s = m_t - log_scale
        w_intra = jnp.exp(dmat - m_s) * qk
        w_inter = jnp.exp(a - m_s)
        c_state = C_ref[h]
        n_state = n_ref[h]
        q_c = jnp.dot(q, c_state.astype(BF16), preferred_element_type=F32)
        num = jnp.dot(w_intra.astype(BF16), v, preferred_element_type=F32) + w_inter * q_c
        q_n = jnp.sum(q.astype(F32) * n_state, axis=1, keepdims=True)
        den = jnp.sum(w_intra, axis=1, keepdims=True) + w_inter * q_n
        hh = num / jnp.maximum(jnp.abs(den), jnp.exp(-m_t))

        w_col = g - b_col + i_col
        m_new = jnp.maximum(g + m_prev, jnp.max(w_col, axis=0, keepdims=True))
        decay = jnp.exp(g + m_prev - m_new)
        kw = k.astype(F32) * jnp.exp(w_col - m_new)
        C_ref[h] = decay * c_state + lax.dot_general(
            kw.astype(BF16), v, (((0,), (0,)), ((), ())), preferred_element_type=F32)
        n_ref[h] = decay * n_state + jnp.sum(kw, axis=0, keepdims=True)
        m_ref[h] = jnp.broadcast_to(m_new, (1, LANES))

        if finish:
            tot = hh + hb_ref[:, h * dv:(h + 1) * dv]
            y = _rms(tot, gm_ref[h:h + 1, :])
            gate = _sigmoid(o_ref[:, h * dv:(h + 1) * dv].astype(F32))
            out_ref[:, h * dv:(h + 1) * dv] = (y * gate).astype(out_ref.dtype)
        else:
            out_ref[:, h * dv:(h + 1) * dv] = hh


def _mlstm(z_a, g_col, g_row, B, S, *, reverse, hb=None, g_mhead=None):
    H, dk, dv = M_HEADS, M_QK_HD, M_V_HD
    L = min(M_CHUNK, S)
    nc = S // L
    T = B * S
    finish = not reverse
    if reverse:
        cidx = lambda b, c: b * nc + (nc - 1 - c)
        ridx = lambda b, c: nc - 1 - c
    else:
        cidx = lambda b, c: b * nc + c
        ridx = lambda b, c: c
    mqk, mv = H * dk, H * dv
    in_specs = [
        pl.BlockSpec((L, mqk), lambda b, c: (cidx(b, c), 0)),
        pl.BlockSpec((L, mqk), lambda b, c: (cidx(b, c), 1)),
        pl.BlockSpec((L, mv), lambda b, c: (cidx(b, c), 1)),
        pl.BlockSpec((L, 4 * H), lambda b, c: (cidx(b, c), 0)),
        pl.BlockSpec((None, 4 * H, L), lambda b, c: (b, 0, ridx(b, c))),
    ]
    args = [z_a, z_a, z_a, g_col, g_row]
    if finish:
        in_specs += [
            pl.BlockSpec((L, mv), lambda b, c: (cidx(b, c), 0)),
            pl.BlockSpec((L, mv), lambda b, c: (cidx(b, c), 2)),
            pl.BlockSpec((H, dv), lambda b, c: (0, 0)),
        ]
        args += [hb, z_a, g_mhead.reshape(H, dv)]
    kern = functools.partial(_mlstm_kernel, reverse=reverse, finish=finish, L=L,
                             goff=2 * H if reverse else 0)
    return pl.pallas_call(
        kern, grid=(B, nc), in_specs=in_specs,
        out_specs=pl.BlockSpec((L, mv), lambda b, c: (cidx(b, c), 0)),
        out_shape=jax.ShapeDtypeStruct((T, mv), BF16 if finish else F32),
        scratch_shapes=[pltpu.VMEM((H, dk, dv), F32), pltpu.VMEM((H, 1, dk), F32),
                        pltpu.VMEM((H, 1, LANES), F32)],
        compiler_params=_cparams(("parallel", "arbitrary")),
        name="mlstm_bwd" if reverse else "mlstm_fwd")(*args)


def _rope_table_kernel(pos_ref, inv_ref, cos_ref, sin_ref):
    ang = pos_ref[...].astype(F32) * inv_ref[...]
    lane = lax.broadcasted_iota(jnp.int32, ang.shape, 1)
    cos_ref[...] = jnp.cos(ang)
    sin_ref[...] = jnp.where(lane < A_ROPE // 2, -jnp.sin(ang), jnp.sin(ang))


def _rope_tables(pos_col, inv_row, tm=1024):
    T = pos_col.shape[0]
    tm = min(tm, T)
    return pl.pallas_call(
        _rope_table_kernel, grid=(T // tm,),
        in_specs=[pl.BlockSpec((tm, 1), lambda i: (i, 0)),
                  pl.BlockSpec((1, LANES), lambda i: (0, 0))],
        out_specs=[pl.BlockSpec((tm, LANES), lambda i: (i, 0))] * 2,
        out_shape=[jax.ShapeDtypeStruct((T, LANES), F32)] * 2,
        compiler_params=_cparams(("parallel",)), name="rope_tables")(pos_col, inv_row)


def _rope(r, cos_ref, sin_ref):
    lane = lax.broadcasted_iota(jnp.int32, r.shape, 1)
    half = A_ROPE // 2
    swapped = jnp.where(lane < half, pltpu.roll(r, LANES - half, 1), pltpu.roll(r, half, 1))
    return r * cos_ref[...] + swapped * sin_ref[...]


PREP_HEADS = 4


def _q_prep_kernel(c_ref, w_ref, g_ref, cos_ref, sin_ref, o_ref):
    c = c_ref[...]
    for h in range(w_ref.shape[0]):
        acc = jnp.dot(c, w_ref[h], preferred_element_type=F32)
        ss = jnp.sum(acc * acc, axis=1, keepdims=True)
        y = acc * lax.rsqrt(ss / A_QK_HD + EPS) * g_ref[...] * (A_QK_HD ** -0.5 * LOG2E)
        o_ref[h, :, 0:A_NOPE] = y[:, 0:A_NOPE].astype(o_ref.dtype)
        o_ref[h, :, A_NOPE:] = _rope(y[:, A_NOPE:], cos_ref, sin_ref).astype(o_ref.dtype)


def _q_prep(cq_n, w_uq_p, g_qn_p, cos_t, sin_t, tm=1024):
    T, R = cq_n.shape
    tm = min(tm, T)
    H, hb = A_HEADS, PREP_HEADS
    return pl.pallas_call(
        _q_prep_kernel, grid=(T // tm, H // hb),
        in_specs=[pl.BlockSpec((tm, R), lambda i, h: (i, 0)),
                  pl.BlockSpec((hb, R, A_QK_PAD), lambda i, h: (h, 0, 0)),
                  pl.BlockSpec((1, A_QK_PAD), lambda i, h: (0, 0)),
                  pl.BlockSpec((tm, LANES), lambda i, h: (i, 0)),
                  pl.BlockSpec((tm, LANES), lambda i, h: (i, 0))],
        out_specs=pl.BlockSpec((hb, tm, A_QK_PAD), lambda i, h: (h, i, 0)),
        out_shape=jax.ShapeDtypeStruct((H, T, A_QK_PAD), BF16),
        compiler_params=_cparams(("parallel", "parallel")), name="mla_q_prep")(
            cq_n, w_uq_p, g_qn_p, cos_t, sin_t)


def _kv_prep_kernel(c_ref, w_ref, sm_ref, g_ref, cos_ref, sin_ref, kt_ref, v_ref):
    c = c_ref[...]
    lane = lax.broadcasted_iota(jnp.int32, sm_ref.shape, 1)
    k_pe = jnp.where(lane < A_ROPE, sm_ref[...], 0.0)
    ss_pe = jnp.sum(k_pe * k_pe, axis=1, keepdims=True)
    for h in range(w_ref.shape[0]):
        acc = jnp.dot(c, w_ref[h], preferred_element_type=F32)
        k_nope = acc[:, 0:A_NOPE]
        ss = jnp.sum(k_nope * k_nope, axis=1, keepdims=True) + ss_pe
        r = lax.rsqrt(ss / A_QK_HD + EPS)
        kn = k_nope * r * g_ref[:, 0:A_NOPE]
        kr = _rope(k_pe * r * g_ref[:, A_NOPE:], cos_ref, sin_ref)
        kt_ref[h, 0:A_NOPE, :] = kn.T.astype(kt_ref.dtype)
        kt_ref[h, A_NOPE:, :] = kr.T.astype(kt_ref.dtype)
        v_ref[h, :, 0:A_V_HD] = acc[:, A_NOPE:].astype(v_ref.dtype)
        v_ref[h, :, A_V_HD:] = jnp.ones((acc.shape[0], A_V_HD), v_ref.dtype)


def _kv_prep(ckv_n, w_ukv_h, small, g_kn_p, cos_t, sin_t, tm=1024):
    T, R = ckv_n.shape
    tm = min(tm, T)
    H, hb = A_HEADS, PREP_HEADS
    return pl.pallas_call(
        _kv_prep_kernel, grid=(T // tm, H // hb),
        in_specs=[pl.BlockSpec((tm, R), lambda i, h: (i, 0)),
                  pl.BlockSpec((hb, R, A_NOPE + A_V_HD), lambda i, h: (h, 0, 0)),
                  pl.BlockSpec((tm, LANES), lambda i, h: (i, 0)),
                  pl.BlockSpec((1, A_QK_PAD), lambda i, h: (0, 0)),
                  pl.BlockSpec((tm, LANES), lambda i, h: (i, 0)),
                  pl.BlockSpec((tm, LANES), lambda i, h: (i, 0))],
        out_specs=[pl.BlockSpec((hb, A_QK_PAD, tm), lambda i, h: (h, 0, i)),
                   pl.BlockSpec((hb, tm, 2 * A_V_HD), lambda i, h: (h, i, 0))],
        out_shape=[jax.ShapeDtypeStruct((H, A_QK_PAD, T), BF16),
                   jax.ShapeDtypeStruct((H, T, 2 * A_V_HD), BF16)],
        compiler_params=_cparams(("parallel", "parallel")), name="mla_kv_prep")(
            ckv_n, w_ukv_h, small, g_kn_p, cos_t, sin_t)


def _attn_kernel(q_ref, kt_ref, v_ref, o_ref, *, tk):
    q = q_ref[...]
    S = kt_ref.shape[1]
    m = acc = None
    for j in range(S // tk):
        s = jnp.dot(q, kt_ref[:, j * tk:(j + 1) * tk], preferred_element_type=F32)
        mj = jnp.max(s, axis=1, keepdims=True)
        vj = v_ref[j * tk:(j + 1) * tk, :]
        if j == 0:
            m = mj
            acc = jnp.dot(jnp.exp2(s - m).astype(BF16), vj, preferred_element_type=F32)
        else:
            m_new = jnp.maximum(m, mj)
            acc = (jnp.exp2(m - m_new) * acc
                   + jnp.dot(jnp.exp2(s - m_new).astype(BF16), vj, preferred_element_type=F32))
            m = m_new
    o_ref[...] = (acc[:, 0:A_V_HD] / acc[:, A_V_HD:A_V_HD + 1]).astype(o_ref.dtype)


def _attention(q, kt, v, B, S):
    H, T, dq = q.shape
    tq, tk = min(ATT_TQ, S), min(ATT_TK, S)
    nq = S // tq
    return pl.pallas_call(
        functools.partial(_attn_kernel, tk=tk), grid=(B, H, nq),
        in_specs=[pl.BlockSpec((None, tq, dq), lambda b, h, i: (h, b * nq + i, 0)),
                  pl.BlockSpec((None, dq, S), lambda b, h, i: (h, 0, b)),
                  pl.BlockSpec((None, S, 2 * A_V_HD), lambda b, h, i: (h, b, 0))],
        out_specs=pl.BlockSpec((tq, A_V_HD), lambda b, h, i: (b * nq + i, h)),
        out_shape=jax.ShapeDtypeStruct((T, H * A_V_HD), BF16),
        compiler_params=_cparams(("parallel", "parallel", "parallel")), name="mla_attention")(
            q, kt, v)


def _merge_kernel(hm_ref, ha_ref, wm_ref, wa_ref, bm_ref, ba_ref, o_ref):
    ym = jnp.dot(hm_ref[...], wm_ref[...], preferred_element_type=F32)
    ya = jnp.dot(ha_ref[...], wa_ref[...], preferred_element_type=F32)
    o_ref[...] = (_sigmoid(bm_ref[...].astype(F32)) * ym
                  + _sigmoid(ba_ref[...].astype(F32)) * ya).astype(o_ref.dtype)


def _merge(hm, ha, w_bm, w_ba, z_a, br_off, tm=512, tn=1024):
    T, K = hm.shape
    D = w_bm.shape[1]
    tm, tn = min(tm, T), min(tn, D)
    ob = br_off // tn
    return pl.pallas_call(
        _merge_kernel, grid=(D // tn, T // tm),
        in_specs=[pl.BlockSpec((tm, K), lambda j, i: (i, 0)),
                  pl.BlockSpec((tm, K), lambda j, i: (i, 0)),
                  pl.BlockSpec((K, tn), lambda j, i: (0, j)),
                  pl.BlockSpec((K, tn), lambda j, i: (0, j)),
                  pl.BlockSpec((tm, tn), lambda j, i: (i, ob + j)),
                  pl.BlockSpec((tm, tn), lambda j, i: (i, ob + D // tn + j))],
        out_specs=pl.BlockSpec((tm, tn), lambda j, i: (i, j)),
        out_shape=jax.ShapeDtypeStruct((T, D), BF16),
        compiler_params=_cparams(("parallel", "parallel")), name="merge")(
            hm, ha, w_bm, w_ba, z_a, z_a)


def _mm_res_kernel(a_ref, b_ref, r_ref, o_ref):
    o_ref[...] = r_ref[...] + jnp.dot(a_ref[...], b_ref[...], preferred_element_type=F32)


def _matmul_residual(a, b, res, tm=512, tn=1024):
    M, K = a.shape
    N = b.shape[1]
    tm, tn = min(tm, M), min(tn, N)
    return pl.pallas_call(
        _mm_res_kernel, grid=(N // tn, M // tm),
        in_specs=[pl.BlockSpec((tm, K), lambda j, i: (i, 0)),
                  pl.BlockSpec((K, tn), lambda j, i: (0, j)),
                  pl.BlockSpec((tm, tn), lambda j, i: (i, j))],
        out_specs=pl.BlockSpec((tm, tn), lambda j, i: (i, j)),
        out_shape=jax.ShapeDtypeStruct((M, N), F32),
        compiler_params=_cparams(("parallel", "parallel")), name="out_proj")(a, b, res)


def _router_kernel(x_ref, g_ref, w_ref, b_ref, o_ref, xn3_ref):
    xf = _rms(x_ref[...], g_ref[...])
    nch = xf.shape[1] // (2 * LANES)
    tm = xf.shape[0]
    pitch = xn3_ref.shape[0] // tm
    for c in range(pitch):
        xn3_ref[pl.ds(c, tm, stride=pitch), :] = (
            _pack_bf16_pair(xf[:, 2 * c * LANES:(2 * c + 1) * LANES],
                            xf[:, (2 * c + 1) * LANES:(2 * c + 2) * LANES])
            if c < nch else jnp.zeros((tm, LANES), jnp.int32))
    xn = xf.astype(BF16)
    logits = jnp.dot(xn, w_ref[...], preferred_element_type=F32) + b_ref[...]
    lane = lax.broadcasted_iota(jnp.int32, logits.shape, 1)
    big = jnp.int32(1 << 20)
    lg = jnp.where(lane < N_GROUPS, logits, NEG)
    gmax = jnp.max(lg, axis=1, keepdims=True)
    g_idx = jnp.min(jnp.where(lg == gmax, lane, big), axis=1, keepdims=True)
    g_w = 1.0 / jnp.sum(jnp.exp(lg - gmax), axis=1, keepdims=True)
    lo = N_GROUPS + g_idx * E_PER_GROUP
    le = jnp.where((lane >= lo) & (lane < lo + E_PER_GROUP), logits, NEG)
    v1 = jnp.max(le, axis=1, keepdims=True)
    i1 = jnp.min(jnp.where(le == v1, lane, big), axis=1, keepdims=True)
    le2 = jnp.where(lane == i1, NEG, le)
    v2 = jnp.max(le2, axis=1, keepdims=True)
    i2 = jnp.min(jnp.where(le2 == v2, lane, big), axis=1, keepdims=True)
    e21 = jnp.exp(v2 - v1)
    w1 = g_w / (1.0 + e21)
    w2 = g_w * e21 / (1.0 + e21)
    o_ref[...] = jnp.where(
        lane == 0, (i1 - N_GROUPS).astype(F32),
        jnp.where(lane == 1, (i2 - N_GROUPS).astype(F32),
                  jnp.where(lane == 2, w1, jnp.where(lane == 3, w2, 0.0))))


def _router(x, g, w_r, b_r, tm=256):
    T, D = x.shape
    tm = min(tm, T)
    pitch = _row_pitch(D // (2 * LANES))
    return pl.pallas_call(
        _router_kernel, grid=(T // tm,),
        in_specs=[pl.BlockSpec((tm, D), lambda i: (i, 0)),
                  pl.BlockSpec((1, D), lambda i: (0, 0)),
                  pl.BlockSpec((D, LANES), lambda i: (0, 0)),
                  pl.BlockSpec((1, LANES), lambda i: (0, 0))],
        out_specs=[pl.BlockSpec((tm, LANES), lambda i: (i, 0)),
                   pl.BlockSpec((tm * pitch, LANES), lambda i: (i, 0))],
        out_shape=[jax.ShapeDtypeStruct((T, LANES), F32),
                   jax.ShapeDtypeStruct((T * pitch, LANES), jnp.int32)],
        compiler_params=_cparams(("parallel",)), name="router")(x, g.reshape(1, D), w_r, b_r)


def _pack_bf16_pair(a, b):
    ua = lax.bitcast_convert_type(a.astype(BF16).astype(F32), jnp.int32)
    ub = lax.bitcast_convert_type(b.astype(BF16).astype(F32), jnp.int32)
    return lax.shift_right_logical(ua, jnp.int32(16)) | (ub & jnp.int32(-65536))


def _unpack_bf16_pair(u):
    return [lax.bitcast_convert_type(lax.shift_left(u, jnp.int32(16)), F32),
            lax.bitcast_convert_type(u & jnp.int32(-65536), F32)]


def _row_pitch(nch):
    return (nch + 7) // 8 * 8 + 8


def _gather_rows(start_one, wait_one, n, unroll=8):
    def start():
        def body(r, carry):
            start_one(r)
            return carry
        lax.fori_loop(0, n, body, 0, unroll=unroll)

    def wait():
        def body(r, carry):
            wait_one(r)
            return carry
        lax.fori_loop(0, n, body, 0, unroll=unroll)
    return start, wait


def _expert_kernel(blk_e_ref, nused_ref, src_ref, tok_ref, x_hbm, w1_ref, w3_ref, w2_ref,
                   y_ref, xbuf, sem, *, blk, n_assign, nch, pitch, ych):
    i = pl.program_id(0)
    nused = nused_ref[0]

    def row_copy(step, r, slot):
        idx = jnp.minimum(src_ref[step] + r, n_assign - 1)
        return pltpu.make_async_copy(
            x_hbm.at[pl.ds(pl.multiple_of(tok_ref[idx] * pitch, 8), nch), :],
            xbuf.at[slot, pl.ds(pl.multiple_of(r * pitch, 8), nch), :], sem.at[slot])

    def gather(step, slot):
        return _gather_rows(lambda r: row_copy(step, r, slot).start(),
                            lambda r: row_copy(step, r, slot).wait(), blk)

    @pl.when(i == 0)
    def _():
        gather(0, 0)[0]()

    @pl.when(i + 1 < nused)
    def _():
        gather(i + 1, (i + 1) % 2)[0]()

    @pl.when(i < nused)
    def _():
        slot = i % 2
        gather(i, slot)[1]()
        parts = []
        for c in range(nch):
            parts += _unpack_bf16_pair(xbuf[slot, pl.ds(c, blk, stride=pitch), :])
        xn = jnp.concatenate([t.astype(BF16) for t in parts], axis=1)
        h1 = jnp.dot(xn, w1_ref[...], preferred_element_type=F32)
        h3 = jnp.dot(xn, w3_ref[...], preferred_element_type=F32)
        hdn = (h1 * _sigmoid(h1) * h3).astype(BF16)
        for j in range(nch * 2 * LANES // ych):
            yj = jnp.dot(hdn, w2_ref[:, j * ych:(j + 1) * ych], preferred_element_type=F32)
            for c in range(ych // (2 * LANES)):
                y_ref[pl.ds(j * (ych // (2 * LANES)) + c, blk, stride=pitch), :] = (
                    _pack_bf16_pair(yj[:, 2 * c * LANES:(2 * c + 1) * LANES],
                                    yj[:, (2 * c + 1) * LANES:(2 * c + 2) * LANES]))
        for c in range(nch, pitch):
            y_ref[pl.ds(c, blk, stride=pitch), :] = jnp.zeros((blk, LANES), jnp.int32)

    @pl.when(i >= nused)
    def _():
        y_ref[...] = jnp.zeros_like(y_ref)


def _experts(xn3, w1, w3, w2, blk_e, nused, src_start, tok_sorted, blk):
    E, D, De = w1.shape
    nch = D // (2 * LANES)
    pitch = _row_pitch(nch)
    nb = blk_e.shape[0]
    n_assign = tok_sorted.shape[0]
    wmap = lambda i, be, nu, sr, tk: (be[i], 0, 0)
    grid_spec = pltpu.PrefetchScalarGridSpec(
        num_scalar_prefetch=4, grid=(nb,),
        in_specs=[pl.BlockSpec(memory_space=pl.ANY),
                  pl.BlockSpec((None, D, De), wmap),
                  pl.BlockSpec((None, D, De), wmap),
                  pl.BlockSpec((None, De, D), wmap)],
        out_specs=pl.BlockSpec((blk * pitch, LANES), lambda i, be, nu, sr, tk: (i, 0)),
        scratch_shapes=[pltpu.VMEM((2, blk * pitch, LANES), jnp.int32),
                        pltpu.SemaphoreType.DMA((2,))])
    return pl.pallas_call(
        functools.partial(_expert_kernel, blk=blk, n_assign=n_assign, nch=nch, pitch=pitch,
                          ych=min(1024, D)),
        grid_spec=grid_spec,
        out_shape=jax.ShapeDtypeStruct((nb * blk * pitch, LANES), jnp.int32),
        compiler_params=_cparams(("arbitrary",)), name="experts")(
            blk_e, nused, src_start, tok_sorted, xn3, w1, w3, w2)


def _combine_kernel(pos_ref, y_hbm, x_ref, rt_ref, g_ref, x2_ref, xn_ref, ybuf, sem, *, tc, nch,
                    pitch):
    i = pl.program_id(0)
    n = pl.num_programs(0)

    def row_copy(step, r, slot):
        return pltpu.make_async_copy(
            y_hbm.at[pl.ds(pl.multiple_of(pos_ref[step * 2 * tc + r] * pitch, 8), nch), :],
            ybuf.at[slot, pl.ds(pl.multiple_of(r * pitch, 8), nch), :], sem.at[slot])

    def gather(step, slot):
        return _gather_rows(lambda r: row_copy(step, r, slot).start(),
                            lambda r: row_copy(step, r, slot).wait(), 2 * tc)

    @pl.when(i == 0)
    def _():
        gather(0, 0)[0]()

    @pl.when(i + 1 < n)
    def _():
        gather(i + 1, (i + 1) % 2)[0]()

    slot = i % 2
    gather(i, slot)[1]()
    y0, y1 = [], []
    for c in range(nch):
        y0 += _unpack_bf16_pair(ybuf[slot, pl.ds(c, tc, stride=pitch), :])
        y1 += _unpack_bf16_pair(ybuf[slot, pl.ds(tc * pitch + c, tc, stride=pitch), :])
    y0 = jnp.concatenate(y0, axis=1)
    y1 = jnp.concatenate(y1, axis=1)
    x2 = x_ref[...] + rt_ref[:, 2:3] * y0 + rt_ref[:, 3:4] * y1
    x2_ref[...] = x2
    xn_ref[...] = _rms(x2, g_ref[...]).astype(xn_ref.dtype)


def _combine(y, x, route, g, pos, tc=128):
    T, D = x.shape
    tc = min(tc, T)
    nch = D // (2 * LANES)
    pitch = _row_pitch(nch)
    grid_spec = pltpu.PrefetchScalarGridSpec(
        num_scalar_prefetch=1, grid=(T // tc,),
        in_specs=[pl.BlockSpec(memory_space=pl.ANY),
                  pl.BlockSpec((tc, D), lambda i, ps: (i, 0)),
                  pl.BlockSpec((tc, LANES), lambda i, ps: (i, 0)),
                  pl.BlockSpec((1, D), lambda i, ps: (0, 0))],
        out_specs=[pl.BlockSpec((tc, D), lambda i, ps: (i, 0)),
                   pl.BlockSpec((tc, D), lambda i, ps: (i, 0))],
        scratch_shapes=[pltpu.VMEM((2, 2 * tc * pitch, LANES), jnp.int32),
                        pltpu.SemaphoreType.DMA((2,))])
    pos_tiled = pos.reshape(T // tc, tc, 2).transpose(0, 2, 1).reshape(-1)
    return pl.pallas_call(
        functools.partial(_combine_kernel, tc=tc, nch=nch, pitch=pitch), grid_spec=grid_spec,
        out_shape=[jax.ShapeDtypeStruct((T, D), F32), jax.ShapeDtypeStruct((T, D), BF16)],
        compiler_params=_cparams(("arbitrary",)), name="combine")(
            pos_tiled, y, x, route, g.reshape(1, D))


def _ple_kernel(xn_ref, wg_ref, p_ref, wp_ref, x_ref, o_ref):
    gate = _sigmoid(jnp.dot(xn_ref[...], wg_ref[...], preferred_element_type=F32))
    emb = jnp.dot(p_ref[...].astype(BF16), wp_ref[...], preferred_element_type=F32)
    o_ref[...] = x_ref[...] + gate * emb


def _ple(xn, w_pg, p, w_ple, x, tm=512, tn=1024):
    T, D = xn.shape
    Pd = p.shape[1]
    tm, tn = min(tm, T), min(tn, D)
    return pl.pallas_call(
        _ple_kernel, grid=(D // tn, T // tm),
        in_specs=[pl.BlockSpec((tm, D), lambda j, i: (i, 0)),
                  pl.BlockSpec((D, tn), lambda j, i: (0, j)),
                  pl.BlockSpec((tm, Pd), lambda j, i: (i, 0)),
                  pl.BlockSpec((Pd, tn), lambda j, i: (0, j)),
                  pl.BlockSpec((tm, tn), lambda j, i: (i, j))],
        out_specs=pl.BlockSpec((tm, tn), lambda j, i: (i, j)),
        out_shape=jax.ShapeDtypeStruct((T, D), F32),
        compiler_params=_cparams(("parallel", "parallel")), name="ple")(xn, w_pg, p, w_ple, x)


def _dispatch_tables(route, blk):
    T = route.shape[0]
    A = 2 * T
    eid = route[:, 0:2].astype(jnp.int32).reshape(-1)
    iota = jnp.arange(A, dtype=jnp.int32)
    experts = jnp.arange(N_EXPERTS, dtype=jnp.int32)
    _, order = lax.sort((eid, iota), num_keys=1)
    _, inv = lax.sort((order, iota), num_keys=1)
    onehot = eid[:, None] == experts[None, :]
    counts = jnp.sum(onehot, axis=0, dtype=jnp.int32)
    starts = jnp.cumsum(counts) - counts
    nblk = (counts + blk - 1) // blk
    bends = jnp.cumsum(nblk)
    bstarts = bends - nblk
    nb = (A + N_EXPERTS * blk) // blk
    b = jnp.arange(nb, dtype=jnp.int32)
    blk_e = jnp.minimum(jnp.sum(b[:, None] >= bends[None, :], axis=1),
                        N_EXPERTS - 1).astype(jnp.int32)
    shift = starts - bstarts * blk
    src_start = b * blk + jnp.sum(
        jnp.where(blk_e[:, None] == experts[None, :], shift[None, :], 0), axis=1)
    nused = bends[-1].astype(jnp.int32).reshape(1)
    pos = inv - jnp.sum(jnp.where(onehot, shift[None, :], 0), axis=1)
    return blk_e, nused, src_start.astype(jnp.int32), order // 2, pos.reshape(T, 2)


def _layer(x2d, p2d, pos_col, B, S, g_mix, w_in, b_mgate, g_mhead, g_cq, w_uq, g_ckv, w_ukv,
           g_qn, g_kn, w_bm, w_ba, w_out, g_ffn, w_rg, b_rg, w_re, b_re, w_e1, w_e3, w_e2,
           g_ple, w_pg, w_ple):
    T, D = x2d.shape
    H = M_HEADS
    mqk, mv = H * M_QK_HD, H * M_V_HD
    o_gate = 2 * mqk + 2 * mv
    o_cq = o_gate + 4 * H
    o_ckv = o_cq + Q_LORA
    o_kpe = o_ckv + KV_LORA
    o_br = o_kpe + A_ROPE
    w_qkvo = w_in[:, :o_gate].astype(BF16)
    w_br = w_in[:, o_br:].astype(BF16)
    w_cq = w_in[:, o_cq:o_ckv].astype(BF16)
    w_ckv = w_in[:, o_ckv:o_kpe].astype(BF16)
    w_sm = jnp.concatenate(
        [w_in[:, o_kpe:o_br], w_in[:, o_gate:o_cq],
         jnp.zeros((D, LANES - A_ROPE - 4 * H), F32)], axis=1).astype(BF16)

    h = _rmsnorm(x2d, g_mix)
    z_a = _matmul(h, w_qkvo, BF16, name="inproj_mlstm")
    z_br = _matmul(h, w_br, BF16, name="inproj_gates")
    cq_n, ckv_n, small = _inproj_small(h, w_cq, w_ckv, w_sm, g_cq, g_ckv)

    g_col = small[:, A_ROPE:A_ROPE + 4 * H] + b_mgate[None, :]
    g_row = g_col.reshape(B, S, 4 * H).transpose(0, 2, 1)
    hb = _mlstm(z_a, g_col, g_row, B, S, reverse=True)
    hm = _mlstm(z_a, g_col, g_row, B, S, reverse=False, hb=hb, g_mhead=g_mhead)

    pad_q = A_QK_PAD - A_QK_HD
    w_uq_p = jnp.pad(w_uq.reshape(Q_LORA, A_HEADS, A_QK_HD).transpose(1, 0, 2),
                     ((0, 0), (0, 0), (0, pad_q))).astype(BF16)
    w_ukv_h = w_ukv.reshape(KV_LORA, A_HEADS, A_NOPE + A_V_HD).transpose(1, 0, 2).astype(BF16)
    g_qn_p = jnp.pad(g_qn, (0, pad_q)).reshape(1, A_QK_PAD)
    g_kn_p = jnp.pad(g_kn, (0, pad_q)).reshape(1, A_QK_PAD)
    inv = ROPE_THETA ** (-jnp.arange(0, A_ROPE, 2, dtype=F32) / A_ROPE)
    inv_row = jnp.concatenate([inv, inv, jnp.zeros((LANES - A_ROPE,), F32)]).reshape(1, LANES)
    cos_t, sin_t = _rope_tables(pos_col, inv_row)
    q = _q_prep(cq_n, w_uq_p, g_qn_p, cos_t, sin_t)
    kt, v = _kv_prep(ckv_n, w_ukv_h, small, g_kn_p, cos_t, sin_t)
    ha = _attention(q, kt, v, B, S)

    mix = _merge(hm, ha, w_bm.astype(BF16), w_ba.astype(BF16), z_br, 0)
    x1 = _matmul_residual(mix, w_out.astype(BF16), x2d)

    w_r = jnp.concatenate([w_rg, w_re, jnp.zeros((D, LANES - N_GROUPS - N_EXPERTS), F32)],
                          axis=1).astype(BF16)
    b_r = jnp.concatenate([b_rg, b_re, jnp.zeros((LANES - N_GROUPS - N_EXPERTS,), F32)]
                          ).reshape(1, LANES)
    route, xn3 = _router(x1, g_ffn, w_r, b_r)
    blk = min(MOE_BLK, T)
    blk_e, nused, src_start, tok_sorted, pos = _dispatch_tables(route, blk)
    y = _experts(xn3, w_e1.astype(BF16), w_e3.astype(BF16), w_e2.astype(BF16),
                 blk_e, nused, src_start, tok_sorted, blk)
    x2, xn2 = _combine(y, x1, route, g_ple, pos)

    return _ple(xn2, w_pg.astype(BF16), p2d, w_ple.astype(BF16), x2)


def kernel(x, p, positions, g_mix, w_in, b_mgate, g_mhead, g_cq, w_uq, g_ckv, w_ukv, g_qn, g_kn,
           w_bm, w_ba, w_out, g_ffn, w_rg, b_rg, w_re, b_re, w_e1, w_e3, w_e2, g_ple, w_pg, w_ple):
    B, S, D = x.shape
    T = B * S
    x2d = x.reshape(T, D)
    pos_col = positions.reshape(T, 1).astype(jnp.int32)
    for l in range(p.shape[0]):
        x2d = _layer(x2d, p[l].reshape(T, -1), pos_col, B, S, g_mix[l], w_in[l], b_mgate[l],
                     g_mhead[l], g_cq[l], w_uq[l], g_ckv[l], w_ukv[l], g_qn[l], g_kn[l],
                     w_bm[l], w_ba[l], w_out[l], g_ffn[l], w_rg[l], b_rg[l], w_re[l], b_re[l],
                     w_e1[l], w_e3[l], w_e2[l], g_ple[l], w_pg[l], w_ple[l])
    return x2d.reshape(B, S, D)
```

```python
import functools
import math

import jax
import jax.numpy as jnp
from jax import lax
from jax.experimental import pallas as pl
from jax.experimental.pallas import tpu as pltpu

F32 = jnp.float32
BF16 = jnp.bfloat16

EPS = 1e-6
M_HEADS = 8
M_QK_HD = 128
M_V_HD = 256
A_HEADS = 16
A_NOPE = 128
A_ROPE = 64
A_QK_HD = A_NOPE + A_ROPE
A_V_HD = 128
A_QK_PAD = 256
Q_LORA = 1024
KV_LORA = 512
ROPE_THETA = 10000.0
N_GROUPS = 8
E_PER_GROUP = 8
N_EXPERTS = N_GROUPS * E_PER_GROUP
D_EXPERT = 512
LANES = 128

M_CHUNK = 256
MOE_BLK = 256
ATT_TQ = 512
ATT_TK = 512
VMEM_LIMIT = 56 * 1024 * 1024
NEG = -1e30
LOG2E = 1.4426950408889634


def _cparams(sem):
    return pltpu.CompilerParams(dimension_semantics=sem, vmem_limit_bytes=VMEM_LIMIT)


def _sigmoid(x):
    return 1.0 / (1.0 + jnp.exp(-x))


def _log_sigmoid(x):
    return jnp.minimum(x, 0.0) - jnp.log(1.0 + jnp.exp(-jnp.abs(x)))


def _rms(xf, g):
    return xf * lax.rsqrt(jnp.mean(xf * xf, axis=-1, keepdims=True) + EPS) * g


def _norm_kernel(x_ref, g_ref, o_ref):
    o_ref[...] = _rms(x_ref[...], g_ref[...]).astype(o_ref.dtype)


def _rmsnorm(x, g, tm=256):
    M, D = x.shape
    tm = min(tm, M)
    return pl.pallas_call(
        _norm_kernel, grid=(M // tm,),
        in_specs=[pl.BlockSpec((tm, D), lambda i: (i, 0)),
                  pl.BlockSpec((1, D), lambda i: (0, 0))],
        out_specs=pl.BlockSpec((tm, D), lambda i: (i, 0)),
        out_shape=jax.ShapeDtypeStruct((M, D), BF16),
        compiler_params=_cparams(("parallel",)), name="rmsnorm")(x, g.reshape(1, D))


def _cache_bf16(w_ref, wbf_ref):
    @pl.when(pl.program_id(1) == 0)
    def _():
        wbf_ref[...] = w_ref[...].astype(BF16)


def _mm_t_kernel(a_ref, wt_ref, o_ref, wbf_ref):
    @pl.when(pl.program_id(1) == 0)
    def _():
        wbf_ref[...] = wt_ref[...].T.astype(BF16)
    o_ref[...] = jnp.dot(a_ref[...], wbf_ref[...],
                         preferred_element_type=F32).astype(o_ref.dtype)


def _matmul_t(a, w_t, n_cols, out_dtype, tm=1024, tn=512, name="matmul"):
    M, K = a.shape
    tm, tn = min(tm, M), min(tn, n_cols)
    return pl.pallas_call(
        _mm_t_kernel, grid=(n_cols // tn, M // tm),
        in_specs=[pl.BlockSpec((tm, K), lambda j, i: (i, 0)),
                  pl.BlockSpec((tn, K), lambda j, i: (j, 0))],
        out_specs=pl.BlockSpec((tm, tn), lambda j, i: (i, j)),
        out_shape=jax.ShapeDtypeStruct((M, n_cols), out_dtype),
        scratch_shapes=[pltpu.VMEM((K, tn), BF16)],
        compiler_params=_cparams(("parallel", "arbitrary")), name=name)(a, w_t)


def _inproj_small_kernel(h_ref, wq_ref, wkv_ref, ws_ref, gq_ref, gkv_ref,
                         cq_ref, ckv_ref, sm_ref):
    h = h_ref[...]
    cq = jnp.dot(h, wq_ref[...], preferred_element_type=F32)
    cq_ref[...] = _rms(cq, gq_ref[...]).astype(cq_ref.dtype)
    ckv = jnp.dot(h, wkv_ref[...], preferred_element_type=F32)
    ckv_ref[...] = _rms(ckv, gkv_ref[...]).astype(ckv_ref.dtype)
    sm_ref[...] = jnp.dot(h, ws_ref[...], preferred_element_type=F32)


def _inproj_small(h, w_cq, w_ckv, w_sm, g_cq, g_ckv, tm=512):
    M, D = h.shape
    tm = min(tm, M)
    nq, nkv, ns = w_cq.shape[1], w_ckv.shape[1], w_sm.shape[1]
    full = lambda n: pl.BlockSpec((D, n), lambda i: (0, 0))
    row = lambda n: pl.BlockSpec((tm, n), lambda i: (i, 0))
    return pl.pallas_call(
        _inproj_small_kernel, grid=(M // tm,),
        in_specs=[row(D), full(nq), full(nkv), full(ns),
                  pl.BlockSpec((1, nq), lambda i: (0, 0)),
                  pl.BlockSpec((1, nkv), lambda i: (0, 0))],
        out_specs=[row(nq), row(nkv), row(ns)],
        out_shape=[jax.ShapeDtypeStruct((M, nq), BF16),
                   jax.ShapeDtypeStruct((M, nkv), BF16),
                   jax.ShapeDtypeStruct((M, ns), F32)],
        compiler_params=_cparams(("parallel",)), name="inproj_small")(
            h, w_cq, w_ckv, w_sm, g_cq.reshape(1, nq), g_ckv.reshape(1, nkv))


def _mlstm_kernel(*refs, reverse, finish, L, goff):
    if finish:
        (q_ref, k_ref, v_ref, gc_ref, gr_ref, hb_ref, o_ref, gm_ref,
         out_ref, C_ref, n_ref, m_ref) = refs
    else:
        q_ref, k_ref, v_ref, gc_ref, gr_ref, out_ref, C_ref, n_ref, m_ref = refs
    H, dk, dv = M_HEADS, M_QK_HD, M_V_HD
    log_scale = -0.5 * math.log(dk)

    @pl.when(pl.program_id(1) == 0)
    def _():
        C_ref[...] = jnp.zeros_like(C_ref)
        n_ref[...] = jnp.zeros_like(n_ref)
        m_ref[...] = jnp.zeros_like(m_ref)

    t_idx = lax.broadcasted_iota(jnp.int32, (L, L), 0)
    s_idx = lax.broadcasted_iota(jnp.int32, (L, L), 1)
    if reverse:
        mask, mask_t = s_idx >= t_idx, t_idx >= s_idx
    else:
        mask, mask_t = s_idx <= t_idx, t_idx <= s_idx

    for h in range(H):
        i_col = gc_ref[:, goff + h:goff + h + 1]
        f_col = gc_ref[:, goff + H + h:goff + H + h + 1]
        i_row = gr_ref[goff + h:goff + h + 1, :]
        f_row = gr_ref[goff + H + h:goff + H + h + 1, :]
        lf_col = _log_sigmoid(f_col)
        lf_row = _log_sigmoid(f_row)
        b_col = jnp.sum(jnp.where(mask, lf_row, 0.0), axis=1, keepdims=True)
        b_row = jnp.sum(jnp.where(mask_t, lf_col, 0.0), axis=0, keepdims=True)
        g = jnp.sum(lf_row, axis=1, keepdims=True)
        m_prev = m_ref[h][:, 0:1]
        dmat = jnp.where(mask, b_col - b_row + i_row, NEG)
        a = b_col + m_prev
        m_t = jnp.maximum(a, jnp.max(dmat, axis=1, keepdims=True))

        q = q_ref[:, h * dk:(h + 1) * dk]
        k = k_ref[:, h * dk:(h + 1) * dk]
        v = v_ref[:, h * dv:(h + 1) * dv]
        qk = lax.dot_general(q, k, (((1,), (1,)), ((), ())),
                             preferred_element_type=F32)
        m_s = m_t - log_scale
        w_intra = jnp.exp(dmat - m_s) * qk
        w_inter = jnp.exp(a - m_s)
        c_state = C_ref[h]
        n_state = n_ref[h]
        q_c = jnp.dot(q, c_state.astype(BF16), preferred_element_type=F32)
        num = jnp.dot(w_intra.astype(BF16), v, preferred_element_type=F32) + w_inter * q_c
        q_n = jnp.sum(q.astype(F32) * n_state, axis=1, keepdims=True)
        den = jnp.sum(w_intra, axis=1, keepdims=True) + w_inter * q_n
        hh = num / jnp.maximum(jnp.abs(den), jnp.exp(-m_t))

        w_col = g - b_col + i_col
        m_new = jnp.maximum(g + m_prev, jnp.max(w_col, axis=0, keepdims=True))
        decay = jnp.exp(g + m_prev - m_new)
        kw = k.astype(F32) * jnp.exp(w_col - m_new)
        C_ref[h] = decay * c_state + lax.dot_general(
            kw.astype(BF16), v, (((0,), (0,)), ((), ())), preferred_element_type=F32)
        n_ref[h] = decay * n_state + jnp.sum(kw, axis=0, keepdims=True)
        m_ref[h] = jnp.broadcast_to(m_new, (1, LANES))

        if finish:
            tot = hh + hb_ref[:, h * dv:(h + 1) * dv]
            y = _rms(tot, gm_ref[h:h + 1, :])
            gate = _sigmoid(o_ref[:, h * dv:(h + 1) * dv].astype(F32))
            out_ref[:, h * dv:(h + 1) * dv] = (y * gate).astype(out_ref.dtype)
        else:
            out_ref[:, h * dv:(h + 1) * dv] = hh


def _mlstm(z_a, g_col, g_row, B, S, *, reverse, hb=None, g_mhead=None):
    H, dk, dv = M_HEADS, M_QK_HD, M_V_HD
    L = min(M_CHUNK, S)
    nc = S // L
    T = B * S
    finish = not reverse
    if reverse:
        cidx = lambda b, c: b * nc + (nc - 1 - c)
        ridx = lambda b, c: nc - 1 - c
    else:
        cidx = lambda b, c: b * nc + c
        ridx = lambda b, c: c
    mqk, mv = H * dk, H * dv
    in_specs = [
        pl.BlockSpec((L, mqk), lambda b, c: (cidx(b, c), 0)),
        pl.BlockSpec((L, mqk), lambda b, c: (cidx(b, c), 1)),
        pl.BlockSpec((L, mv), lambda b, c: (cidx(b, c), 1)),
        pl.BlockSpec((L, 4 * H), lambda b, c: (cidx(b, c), 0)),
        pl.BlockSpec((None, 4 * H, L), lambda b, c: (b, 0, ridx(b, c))),
    ]
    args = [z_a, z_a, z_a, g_col, g_row]
    if finish:
        in_specs += [
            pl.BlockSpec((L, mv), lambda b, c: (cidx(b, c), 0)),
            pl.BlockSpec((L, mv), lambda b, c: (cidx(b, c), 2)),
            pl.BlockSpec((H, dv), lambda b, c: (0, 0)),
        ]
        args += [hb, z_a, g_mhead.reshape(H, dv)]
    kern = functools.partial(_mlstm_kernel, reverse=reverse, finish=finish, L=L,
                             goff=2 * H if reverse else 0)
    return pl.pallas_call(
        kern, grid=(B, nc), in_specs=in_specs,
        out_specs=pl.BlockSpec((L, mv), lambda b, c: (cidx(b, c), 0)),
        out_shape=jax.ShapeDtypeStruct((T, mv), BF16 if finish else F32),
        scratch_shapes=[pltpu.VMEM((H, dk, dv), F32), pltpu.VMEM((H, 1, dk), F32),
                        pltpu.VMEM((H, 1, LANES), F32)],
        compiler_params=_cparams(("parallel", "arbitrary")),
        name="mlstm_bwd" if reverse else "mlstm_fwd")(*args)


def _rope_table_kernel(pos_ref, inv_ref, cos_ref, sin_ref):
    ang = pos_ref[...].astype(F32) * inv_ref[...]
    lane = lax.broadcasted_iota(jnp.int32, ang.shape, 1)
    cos_ref[...] = jnp.cos(ang)
    sin_ref[...] = jnp.where(lane < A_ROPE // 2, -jnp.sin(ang), jnp.sin(ang))


def _rope_tables(pos_col, inv_row, tm=1024):
    T = pos_col.shape[0]
    tm = min(tm, T)
    return pl.pallas_call(
        _rope_table_kernel, grid=(T // tm,),
        in_specs=[pl.BlockSpec((tm, 1), lambda i: (i, 0)),
                  pl.BlockSpec((1, LANES), lambda i: (0, 0))],
        out_specs=[pl.BlockSpec((tm, LANES), lambda i: (i, 0))] * 2,
        out_shape=[jax.ShapeDtypeStruct((T, LANES), F32)] * 2,
        compiler_params=_cparams(("parallel",)), name="rope_tables")(pos_col, inv_row)


def _rope(r, cos_ref, sin_ref):
    lane = lax.broadcasted_iota(jnp.int32, r.shape, 1)
    half = A_ROPE // 2
    swapped = jnp.where(lane < half, pltpu.roll(r, LANES - half, 1), pltpu.roll(r, half, 1))
    return r * cos_ref[...] + swapped * sin_ref[...]


PREP_HEADS = 4


def _q_prep_kernel(c_ref, w_ref, g_ref, cos_ref, sin_ref, o_ref):
    c = c_ref[...]
    for h in range(w_ref.shape[0]):
        acc = jnp.dot(c, w_ref[h], preferred_element_type=F32)
        ss = jnp.sum(acc * acc, axis=1, keepdims=True)
        y = acc * lax.rsqrt(ss / A_QK_HD + EPS) * g_ref[...] * (A_QK_HD ** -0.5 * LOG2E)
        o_ref[h, :, 0:A_NOPE] = y[:, 0:A_NOPE].astype(o_ref.dtype)
        o_ref[h, :, A_NOPE:] = _rope(y[:, A_NOPE:], cos_ref, sin_ref).astype(o_ref.dtype)


def _q_prep(cq_n, w_uq_p, g_qn_p, cos_t, sin_t, tm=1024):
    T, R = cq_n.shape
    tm = min(tm, T)
    H, hb = A_HEADS, PREP_HEADS
    return pl.pallas_call(
        _q_prep_kernel, grid=(T // tm, H // hb),
        in_specs=[pl.BlockSpec((tm, R), lambda i, h: (i, 0)),
                  pl.BlockSpec((hb, R, A_QK_PAD), lambda i, h: (h, 0, 0)),
                  pl.BlockSpec((1, A_QK_PAD), lambda i, h: (0, 0)),
                  pl.BlockSpec((tm, LANES), lambda i, h: (i, 0)),
                  pl.BlockSpec((tm, LANES), lambda i, h: (i, 0))],
        out_specs=pl.BlockSpec((hb, tm, A_QK_PAD), lambda i, h: (h, i, 0)),
        out_shape=jax.ShapeDtypeStruct((H, T, A_QK_PAD), BF16),
        compiler_params=_cparams(("parallel", "parallel")), name="mla_q_prep")(
            cq_n, w_uq_p, g_qn_p, cos_t, sin_t)


def _kv_prep_kernel(c_ref, w_ref, sm_ref, g_ref, cos_ref, sin_ref, kt_ref, v_ref):
    c = c_ref[...]
    lane = lax.broadcasted_iota(jnp.int32, sm_ref.shape, 1)
    k_pe = jnp.where(lane < A_ROPE, sm_ref[...], 0.0)
    ss_pe = jnp.sum(k_pe * k_pe, axis=1, keepdims=True)
    for h in range(w_ref.shape[0]):
        acc = jnp.dot(c, w_ref[h], preferred_element_type=F32)
        k_nope = acc[:, 0:A_NOPE]
        ss = jnp.sum(k_nope * k_nope, axis=1, keepdims=True) + ss_pe
        r = lax.rsqrt(ss / A_QK_HD + EPS)
        kn = k_nope * r * g_ref[:, 0:A_NOPE]
        kr = _rope(k_pe * r * g_ref[:, A_NOPE:], cos_ref, sin_ref)
        kt_ref[h, 0:A_NOPE, :] = kn.T.astype(kt_ref.dtype)
        kt_ref[h, A_NOPE:, :] = kr.T.astype(kt_ref.dtype)
        v_ref[h, :, 0:A_V_HD] = acc[:, A_NOPE:].astype(v_ref.dtype)
        v_ref[h, :, A_V_HD:] = jnp.ones((acc.shape[0], A_V_HD), v_ref.dtype)


def _kv_prep(ckv_n, w_ukv_h, small, g_kn_p, cos_t, sin_t, tm=1024):
    T, R = ckv_n.shape
    tm = min(tm, T)
    H, hb = A_HEADS, PREP_HEADS
    return pl.pallas_call(
        _kv_prep_kernel, grid=(T // tm, H // hb),
        in_specs=[pl.BlockSpec((tm, R), lambda i, h: (i, 0)),
                  pl.BlockSpec((hb, R, A_NOPE + A_V_HD), lambda i, h: (h, 0, 0)),
                  pl.BlockSpec((tm, LANES), lambda i, h: (i, 0)),
                  pl.BlockSpec((1, A_QK_PAD), lambda i, h: (0, 0)),
                  pl.BlockSpec((tm, LANES), lambda i, h: (i, 0)),
                  pl.BlockSpec((tm, LANES), lambda i, h: (i, 0))],
        out_specs=[pl.BlockSpec((hb, A_QK_PAD, tm), lambda i, h: (h, 0, i)),
                   pl.BlockSpec((hb, tm, 2 * A_V_HD), lambda i, h: (h, i, 0))],
        out_shape=[jax.ShapeDtypeStruct((H, A_QK_PAD, T), BF16),
                   jax.ShapeDtypeStruct((H, T, 2 * A_V_HD), BF16)],
        compiler_params=_cparams(("parallel", "parallel")), name="mla_kv_prep")(
            ckv_n, w_ukv_h, small, g_kn_p, cos_t, sin_t)


def _attn_kernel(q_ref, kt_ref, v_ref, o_ref, *, tk):
    q = q_ref[...]
    S = kt_ref.shape[1]
    m = acc = None
    for j in range(S // tk):
        s = jnp.dot(q, kt_ref[:, j * tk:(j + 1) * tk], preferred_element_type=F32)
        mj = jnp.max(s, axis=1, keepdims=True)
        vj = v_ref[j * tk:(j + 1) * tk, :]
        if j == 0:
            m = mj
            acc = jnp.dot(jnp.exp2(s - m).astype(BF16), vj, preferred_element_type=F32)
        else:
            m_new = jnp.maximum(m, mj)
            acc = (jnp.exp2(m - m_new) * acc
                   + jnp.dot(jnp.exp2(s - m_new).astype(BF16), vj, preferred_element_type=F32))
            m = m_new
    o_ref[...] = (acc[:, 0:A_V_HD] / acc[:, A_V_HD:A_V_HD + 1]).astype(o_ref.dtype)


def _attention(q, kt, v, B, S):
    H, T, dq = q.shape
    tq, tk = min(ATT_TQ, S), min(ATT_TK, S)
    nq = S // tq
    return pl.pallas_call(
        functools.partial(_attn_kernel, tk=tk), grid=(B, H, nq),
        in_specs=[pl.BlockSpec((None, tq, dq), lambda b, h, i: (h, b * nq + i, 0)),
                  pl.BlockSpec((None, dq, S), lambda b, h, i: (h, 0, b)),
                  pl.BlockSpec((None, S, 2 * A_V_HD), lambda b, h, i: (h, b, 0))],
        out_specs=pl.BlockSpec((tq, A_V_HD), lambda b, h, i: (b * nq + i, h)),
        out_shape=jax.ShapeDtypeStruct((T, H * A_V_HD), BF16),
        compiler_params=_cparams(("parallel", "parallel", "parallel")), name="mla_attention")(
            q, kt, v)


def _merge_kernel(hm_ref, ha_ref, wm_ref, wa_ref, bm_ref, ba_ref, o_ref, wm_bf, wa_bf):
    _cache_bf16(wm_ref, wm_bf)
    _cache_bf16(wa_ref, wa_bf)
    ym = jnp.dot(hm_ref[...], wm_bf[...], preferred_element_type=F32)
    ya = jnp.dot(ha_ref[...], wa_bf[...], preferred_element_type=F32)
    o_ref[...] = (_sigmoid(bm_ref[...].astype(F32)) * ym
                  + _sigmoid(ba_ref[...].astype(F32)) * ya).astype(o_ref.dtype)


def _merge(hm, ha, w_bm, w_ba, z_a, br_off, tm=512, tn=512):
    T, K = hm.shape
    D = w_bm.shape[1]
    tm, tn = min(tm, T), min(tn, D)
    ob = br_off // tn
    return pl.pallas_call(
        _merge_kernel, grid=(D // tn, T // tm),
        in_specs=[pl.BlockSpec((tm, K), lambda j, i: (i, 0)),
                  pl.BlockSpec((tm, K), lambda j, i: (i, 0)),
                  pl.BlockSpec((K, tn), lambda j, i: (0, j)),
                  pl.BlockSpec((K, tn), lambda j, i: (0, j)),
                  pl.BlockSpec((tm, tn), lambda j, i: (i, ob + j)),
                  pl.BlockSpec((tm, tn), lambda j, i: (i, ob + D // tn + j))],
        out_specs=pl.BlockSpec((tm, tn), lambda j, i: (i, j)),
        out_shape=jax.ShapeDtypeStruct((T, D), BF16),
        scratch_shapes=[pltpu.VMEM((K, tn), BF16), pltpu.VMEM((K, tn), BF16)],
        compiler_params=_cparams(("parallel", "arbitrary")), name="merge")(
            hm, ha, w_bm, w_ba, z_a, z_a)


def _mm_res_kernel(a_ref, w_ref, r_ref, o_ref, wbf_ref):
    _cache_bf16(w_ref, wbf_ref)
    o_ref[...] = r_ref[...] + jnp.dot(a_ref[...], wbf_ref[...], preferred_element_type=F32)


def _matmul_residual(a, b, res, tm=512, tn=512):
    M, K = a.shape
    N = b.shape[1]
    tm, tn = min(tm, M), min(tn, N)
    return pl.pallas_call(
        _mm_res_kernel, grid=(N // tn, M // tm),
        in_specs=[pl.BlockSpec((tm, K), lambda j, i: (i, 0)),
                  pl.BlockSpec((K, tn), lambda j, i: (0, j)),
                  pl.BlockSpec((tm, tn), lambda j, i: (i, j))],
        out_specs=pl.BlockSpec((tm, tn), lambda j, i: (i, j)),
        out_shape=jax.ShapeDtypeStruct((M, N), F32),
        scratch_shapes=[pltpu.VMEM((K, tn), BF16)],
        compiler_params=_cparams(("parallel", "arbitrary")), name="out_proj")(a, b, res)


def _router_kernel(x_ref, g_ref, w_ref, b_ref, o_ref, xn3_ref):
    xf = _rms(x_ref[...], g_ref[...])
    nch = xf.shape[1] // (2 * LANES)
    tm = xf.shape[0]
    pitch = xn3_ref.shape[0] // tm
    for c in range(pitch):
        xn3_ref[pl.ds(c, tm, stride=pitch), :] = (
            _pack_bf16_pair(xf[:, 2 * c * LANES:(2 * c + 1) * LANES],
                            xf[:, (2 * c + 1) * LANES:(2 * c + 2) * LANES])
            if c < nch else jnp.zeros((tm, LANES), jnp.int32))
    xn = xf.astype(BF16)
    logits = jnp.dot(xn, w_ref[...], preferred_element_type=F32) + b_ref[...]
    lane = lax.broadcasted_iota(jnp.int32, logits.shape, 1)
    big = jnp.int32(1 << 20)
    lg = jnp.where(lane < N_GROUPS, logits, NEG)
    gmax = jnp.max(lg, axis=1, keepdims=True)
    g_idx = jnp.min(jnp.where(lg == gmax, lane, big), axis=1, keepdims=True)
    g_w = 1.0 / jnp.sum(jnp.exp(lg - gmax), axis=1, keepdims=True)
    lo = N_GROUPS + g_idx * E_PER_GROUP
    le = jnp.where((lane >= lo) & (lane < lo + E_PER_GROUP), logits, NEG)
    v1 = jnp.max(le, axis=1, keepdims=True)
    i1 = jnp.min(jnp.where(le == v1, lane, big), axis=1, keepdims=True)
    le2 = jnp.where(lane == i1, NEG, le)
    v2 = jnp.max(le2, axis=1, keepdims=True)
    i2 = jnp.min(jnp.where(le2 == v2, lane, big), axis=1, keepdims=True)
    e21 = jnp.exp(v2 - v1)
    w1 = g_w / (1.0 + e21)
    w2 = g_w * e21 / (1.0 + e21)
    o_ref[...] = jnp.where(
        lane == 0, (i1 - N_GROUPS).astype(F32),
        jnp.where(lane == 1, (i2 - N_GROUPS).astype(F32),
                  jnp.where(lane == 2, w1, jnp.where(lane == 3, w2, 0.0))))


def _router(x, g, w_r, b_r, tm=256):
    T, D = x.shape
    tm = min(tm, T)
    pitch = _row_pitch(D // (2 * LANES))
    return pl.pallas_call(
        _router_kernel, grid=(T // tm,),
        in_specs=[pl.BlockSpec((tm, D), lambda i: (i, 0)),
                  pl.BlockSpec((1, D), lambda i: (0, 0)),
                  pl.BlockSpec((D, LANES), lambda i: (0, 0)),
                  pl.BlockSpec((1, LANES), lambda i: (0, 0))],
        out_specs=[pl.BlockSpec((tm, LANES), lambda i: (i, 0)),
                   pl.BlockSpec((tm * pitch, LANES), lambda i: (i, 0))],
        out_shape=[jax.ShapeDtypeStruct((T, LANES), F32),
                   jax.ShapeDtypeStruct((T * pitch, LANES), jnp.int32)],
        compiler_params=_cparams(("parallel",)), name="router")(x, g.reshape(1, D), w_r, b_r)


def _pack_bf16_pair(a, b):
    ua = lax.bitcast_convert_type(a.astype(BF16).astype(F32), jnp.int32)
    ub = lax.bitcast_convert_type(b.astype(BF16).astype(F32), jnp.int32)
    return lax.shift_right_logical(ua, jnp.int32(16)) | (ub & jnp.int32(-65536))


def _unpack_bf16_pair(u):
    return [lax.bitcast_convert_type(lax.shift_left(u, jnp.int32(16)), F32),
            lax.bitcast_convert_type(u & jnp.int32(-65536), F32)]


def _row_pitch(nch):
    return (nch + 7) // 8 * 8 + 8


def _gather_rows(start_one, wait_one, n, unroll=8):
    def start():
        def body(r, carry):
            start_one(r)
            return carry
        lax.fori_loop(0, n, body, 0, unroll=unroll)

    def wait():
        def body(r, carry):
            wait_one(r)
            return carry
        lax.fori_loop(0, n, body, 0, unroll=unroll)
    return start, wait


def _expert_changed(blk_e_ref, i):
    return (i == 0) | (blk_e_ref[i] != blk_e_ref[jnp.maximum(i - 1, 0)])


def _expert_up_kernel(blk_e_ref, nused_ref, src_ref, tok_ref, x_hbm, w1_ref, w3_ref,
                      h_ref, xbuf, w1_bf, w3_bf, sem, *, blk, n_assign, nch, pitch):
    i = pl.program_id(0)
    nused = nused_ref[0]

    def row_copy(step, r, slot):
        idx = jnp.minimum(src_ref[step] + r, n_assign - 1)
        return pltpu.make_async_copy(
            x_hbm.at[pl.ds(pl.multiple_of(tok_ref[idx] * pitch, 8), nch), :],
            xbuf.at[slot, pl.ds(pl.multiple_of(r * pitch, 8), nch), :], sem.at[slot])

    def gather(step, slot):
        return _gather_rows(lambda r: row_copy(step, r, slot).start(),
                            lambda r: row_copy(step, r, slot).wait(), blk)

    @pl.when(i == 0)
    def _():
        gather(0, 0)[0]()

    @pl.when(i + 1 < nused)
    def _():
        gather(i + 1, (i + 1) % 2)[0]()

    @pl.when((i < nused) & _expert_changed(blk_e_ref, i))
    def _():
        w1_bf[...] = w1_ref[...].astype(BF16)
        w3_bf[...] = w3_ref[...].astype(BF16)

    @pl.when(i < nused)
    def _():
        slot = i % 2
        gather(i, slot)[1]()
        parts = []
        for c in range(nch):
            parts += _unpack_bf16_pair(xbuf[slot, pl.ds(c, blk, stride=pitch), :])
        xn = jnp.concatenate([t.astype(BF16) for t in parts], axis=1)
        h1 = jnp.dot(xn, w1_bf[...], preferred_element_type=F32)
        h3 = jnp.dot(xn, w3_bf[...], preferred_element_type=F32)
        h_ref[...] = (h1 * _sigmoid(h1) * h3).astype(h_ref.dtype)

    @pl.when(i >= nused)
    def _():
        h_ref[...] = jnp.zeros_like(h_ref)


def _expert_down_kernel(blk_e_ref, nused_ref, h_ref, w2_ref, y_ref, w2_bf, *, blk, nch, pitch,
                        ych):
    i = pl.program_id(0)
    nused = nused_ref[0]

    @pl.when((i < nused) & _expert_changed(blk_e_ref, i))
    def _():
        w2_bf[...] = w2_ref[...].astype(BF16)

    @pl.when(i < nused)
    def _():
        hdn = h_ref[...]
        for j in range(nch * 2 * LANES // ych):
            yj = jnp.dot(hdn, w2_bf[:, j * ych:(j + 1) * ych], preferred_element_type=F32)
            for c in range(ych // (2 * LANES)):
                y_ref[pl.ds(j * (ych // (2 * LANES)) + c, blk, stride=pitch), :] = (
                    _pack_bf16_pair(yj[:, 2 * c * LANES:(2 * c + 1) * LANES],
                                    yj[:, (2 * c + 1) * LANES:(2 * c + 2) * LANES]))
        for c in range(nch, pitch):
            y_ref[pl.ds(c, blk, stride=pitch), :] = jnp.zeros((blk, LANES), jnp.int32)

    @pl.when(i >= nused)
    def _():
        y_ref[...] = jnp.zeros_like(y_ref)


def _experts(xn3, w1, w3, w2, blk_e, nused, src_start, tok_sorted, blk):
    E, D, De = w1.shape
    nch = D // (2 * LANES)
    pitch = _row_pitch(nch)
    nb = blk_e.shape[0]
    n_assign = tok_sorted.shape[0]
    wmap4 = lambda i, be, nu, sr, tk: (be[i], 0, 0)
    up_spec = pltpu.PrefetchScalarGridSpec(
        num_scalar_prefetch=4, grid=(nb,),
        in_specs=[pl.BlockSpec(memory_space=pl.ANY),
                  pl.BlockSpec((None, D, De), wmap4),
                  pl.BlockSpec((None, D, De), wmap4)],
        out_specs=pl.BlockSpec((blk, De), lambda i, be, nu, sr, tk: (i, 0)),
        scratch_shapes=[pltpu.VMEM((2, blk * pitch, LANES), jnp.int32),
                        pltpu.VMEM((D, De), BF16), pltpu.VMEM((D, De), BF16),
                        pltpu.SemaphoreType.DMA((2,))])
    hdn = pl.pallas_call(
        functools.partial(_expert_up_kernel, blk=blk, n_assign=n_assign, nch=nch, pitch=pitch),
        grid_spec=up_spec,
        out_shape=jax.ShapeDtypeStruct((nb * blk, De), BF16),
        compiler_params=_cparams(("arbitrary",)), name="experts_up")(
            blk_e, nused, src_start, tok_sorted, xn3, w1, w3)
    down_spec = pltpu.PrefetchScalarGridSpec(
        num_scalar_prefetch=2, grid=(nb,),
        in_specs=[pl.BlockSpec((blk, De), lambda i, be, nu: (i, 0)),
                  pl.BlockSpec((None, De, D), lambda i, be, nu: (be[i], 0, 0))],
        out_specs=pl.BlockSpec((blk * pitch, LANES), lambda i, be, nu: (i, 0)),
        scratch_shapes=[pltpu.VMEM((De, D), BF16)])
    return pl.pallas_call(
        functools.partial(_expert_down_kernel, blk=blk, nch=nch, pitch=pitch, ych=min(1024, D)),
        grid_spec=down_spec,
        out_shape=jax.ShapeDtypeStruct((nb * blk * pitch, LANES), jnp.int32),
        compiler_params=_cparams(("arbitrary",)), name="experts_down")(blk_e, nused, hdn, w2)


def _combine_kernel(pos_ref, y_hbm, x_ref, rt_ref, g_ref, x2_ref, xn_ref, ybuf, sem, *, tc, nch,
                    pitch):
    i = pl.program_id(0)
    n = pl.num_programs(0)

    def row_copy(step, r, slot):
        return pltpu.make_async_copy(
            y_hbm.at[pl.ds(pl.multiple_of(pos_ref[step * 2 * tc + r] * pitch, 8), nch), :],
            ybuf.at[slot, pl.ds(pl.multiple_of(r * pitch, 8), nch), :], sem.at[slot])

    def gather(step, slot):
        return _gather_rows(lambda r: row_copy(step, r, slot).start(),
                            lambda r: row_copy(step, r, slot).wait(), 2 * tc)

    @pl.when(i == 0)
    def _():
        gather(0, 0)[0]()

    @pl.when(i + 1 < n)
    def _():
        gather(i + 1, (i + 1) % 2)[0]()

    slot = i % 2
    gather(i, slot)[1]()
    y0, y1 = [], []
    for c in range(nch):
        y0 += _unpack_bf16_pair(ybuf[slot, pl.ds(c, tc, stride=pitch), :])
        y1 += _unpack_bf16_pair(ybuf[slot, pl.ds(tc * pitch + c, tc, stride=pitch), :])
    y0 = jnp.concatenate(y0, axis=1)
    y1 = jnp.concatenate(y1, axis=1)
    x2 = x_ref[...] + rt_ref[:, 2:3] * y0 + rt_ref[:, 3:4] * y1
    x2_ref[...] = x2
    xn_ref[...] = _rms(x2, g_ref[...]).astype(xn_ref.dtype)


def _combine(y, x, route, g, pos, tc=128):
    T, D = x.shape
    tc = min(tc, T)
    nch = D // (2 * LANES)
    pitch = _row_pitch(nch)
    grid_spec = pltpu.PrefetchScalarGridSpec(
        num_scalar_prefetch=1, grid=(T // tc,),
        in_specs=[pl.BlockSpec(memory_space=pl.ANY),
                  pl.BlockSpec((tc, D), lambda i, ps: (i, 0)),
                  pl.BlockSpec((tc, LANES), lambda i, ps: (i, 0)),
                  pl.BlockSpec((1, D), lambda i, ps: (0, 0))],
        out_specs=[pl.BlockSpec((tc, D), lambda i, ps: (i, 0)),
                   pl.BlockSpec((tc, D), lambda i, ps: (i, 0))],
        scratch_shapes=[pltpu.VMEM((2, 2 * tc * pitch, LANES), jnp.int32),
                        pltpu.SemaphoreType.DMA((2,))])
    pos_tiled = pos.reshape(T // tc, tc, 2).transpose(0, 2, 1).reshape(-1)
    return pl.pallas_call(
        functools.partial(_combine_kernel, tc=tc, nch=nch, pitch=pitch), grid_spec=grid_spec,
        out_shape=[jax.ShapeDtypeStruct((T, D), F32), jax.ShapeDtypeStruct((T, D), BF16)],
        compiler_params=_cparams(("arbitrary",)), name="combine")(
            pos_tiled, y, x, route, g.reshape(1, D))


def _ple_kernel(xn_ref, wg_ref, p_ref, wp_ref, x_ref, o_ref, wg_bf, wp_bf):
    _cache_bf16(wg_ref, wg_bf)
    _cache_bf16(wp_ref, wp_bf)
    gate = _sigmoid(jnp.dot(xn_ref[...], wg_bf[...], preferred_element_type=F32))
    emb = jnp.dot(p_ref[...].astype(BF16), wp_bf[...], preferred_element_type=F32)
    o_ref[...] = x_ref[...] + gate * emb


def _ple(xn, w_pg, p, w_ple, x, tm=512, tn=512):
    T, D = xn.shape
    Pd = p.shape[1]
    tm, tn = min(tm, T), min(tn, D)
    return pl.pallas_call(
        _ple_kernel, grid=(D // tn, T // tm),
        in_specs=[pl.BlockSpec((tm, D), lambda j, i: (i, 0)),
                  pl.BlockSpec((D, tn), lambda j, i: (0, j)),
                  pl.BlockSpec((tm, Pd), lambda j, i: (i, 0)),
                  pl.BlockSpec((Pd, tn), lambda j, i: (0, j)),
                  pl.BlockSpec((tm, tn), lambda j, i: (i, j))],
        out_specs=pl.BlockSpec((tm, tn), lambda j, i: (i, j)),
        out_shape=jax.ShapeDtypeStruct((T, D), F32),
        scratch_shapes=[pltpu.VMEM((D, tn), BF16), pltpu.VMEM((Pd, tn), BF16)],
        compiler_params=_cparams(("parallel", "arbitrary")), name="ple")(xn, w_pg, p, w_ple, x)


def _dispatch_tables(route, blk):
    T = route.shape[0]
    A = 2 * T
    eid = route[:, 0:2].astype(jnp.int32).reshape(-1)
    iota = jnp.arange(A, dtype=jnp.int32)
    experts = jnp.arange(N_EXPERTS, dtype=jnp.int32)
    _, order = lax.sort((eid, iota), num_keys=1)
    _, inv = lax.sort((order, iota), num_keys=1)
    onehot = eid[:, None] == experts[None, :]
    counts = jnp.sum(onehot, axis=0, dtype=jnp.int32)
    starts = jnp.cumsum(counts) - counts
    nblk = (counts + blk - 1) // blk
    bends = jnp.cumsum(nblk)
    bstarts = bends - nblk
    nb = (A + N_EXPERTS * blk) // blk
    b = jnp.arange(nb, dtype=jnp.int32)
    blk_e = jnp.minimum(jnp.sum(b[:, None] >= bends[None, :], axis=1),
                        N_EXPERTS - 1).astype(jnp.int32)
    shift = starts - bstarts * blk
    src_start = b * blk + jnp.sum(
        jnp.where(blk_e[:, None] == experts[None, :], shift[None, :], 0), axis=1)
    nused = bends[-1].astype(jnp.int32).reshape(1)
    pos = inv - jnp.sum(jnp.where(onehot, shift[None, :], 0), axis=1)
    return blk_e, nused, src_start.astype(jnp.int32), order // 2, pos.reshape(T, 2)


def _layer(x2d, p2d, pos_col, B, S, g_mix, w_in, b_mgate, g_mhead, g_cq, w_uq, g_ckv, w_ukv,
           g_qn, g_kn, w_bm, w_ba, w_out, g_ffn, w_rg, b_rg, w_re, b_re, w_e1, w_e3, w_e2,
           g_ple, w_pg, w_ple):
    T, D = x2d.shape
    H = M_HEADS
    mqk, mv = H * M_QK_HD, H * M_V_HD
    o_gate = 2 * mqk + 2 * mv
    o_cq = o_gate + 4 * H
    o_ckv = o_cq + Q_LORA
    o_kpe = o_ckv + KV_LORA
    o_br = o_kpe + A_ROPE
    w_t = jnp.swapaxes(w_in, 0, 1)
    w_br_t, w_mid_t = lax.optimization_barrier((w_t[o_br:], w_t[o_gate:o_br]))
    w_cq = w_mid_t[o_cq - o_gate:o_ckv - o_gate].T.astype(BF16)
    w_ckv = w_mid_t[o_ckv - o_gate:o_kpe - o_gate].T.astype(BF16)
    w_sm = jnp.concatenate(
        [w_mid_t[o_kpe - o_gate:], w_mid_t[:4 * H],
         jnp.zeros((LANES - A_ROPE - 4 * H, D), F32)], axis=0).T.astype(BF16)

    h = _rmsnorm(x2d, g_mix)
    z_a = _matmul_t(h, w_t, o_gate, BF16, name="inproj_mlstm")
    z_br = _matmul_t(h, w_br_t, 2 * D, BF16, name="inproj_gates")
    cq_n, ckv_n, small = _inproj_small(h, w_cq, w_ckv, w_sm, g_cq, g_ckv)

    g_col = small[:, A_ROPE:A_ROPE + 4 * H] + b_mgate[None, :]
    g_row = g_col.reshape(B, S, 4 * H).transpose(0, 2, 1)
    hb = _mlstm(z_a, g_col, g_row, B, S, reverse=True)
    hm = _mlstm(z_a, g_col, g_row, B, S, reverse=False, hb=hb, g_mhead=g_mhead)

    pad_q = A_QK_PAD - A_QK_HD
    w_uq_p = jnp.pad(w_uq.reshape(Q_LORA, A_HEADS, A_QK_HD).transpose(1, 0, 2),
                     ((0, 0), (0, 0), (0, pad_q))).astype(BF16)
    w_ukv_h = w_ukv.reshape(KV_LORA, A_HEADS, A_NOPE + A_V_HD).transpose(1, 0, 2).astype(BF16)
    g_qn_p = jnp.pad(g_qn, (0, pad_q)).reshape(1, A_QK_PAD)
    g_kn_p = jnp.pad(g_kn, (0, pad_q)).reshape(1, A_QK_PAD)
    inv = ROPE_THETA ** (-jnp.arange(0, A_ROPE, 2, dtype=F32) / A_ROPE)
    inv_row = jnp.concatenate([inv, inv, jnp.zeros((LANES - A_ROPE,), F32)]).reshape(1, LANES)
    cos_t, sin_t = _rope_tables(pos_col, inv_row)
    q = _q_prep(cq_n, w_uq_p, g_qn_p, cos_t, sin_t)
    kt, v = _kv_prep(ckv_n, w_ukv_h, small, g_kn_p, cos_t, sin_t)
    ha = _attention(q, kt, v, B, S)

    mix = _merge(hm, ha, w_bm, w_ba, z_br, 0)
    x1 = _matmul_residual(mix, w_out, x2d)

    w_r = jnp.concatenate([w_rg, w_re, jnp.zeros((D, LANES - N_GROUPS - N_EXPERTS), F32)],
                          axis=1).astype(BF16)
    b_r = jnp.concatenate([b_rg, b_re, jnp.zeros((LANES - N_GROUPS - N_EXPERTS,), F32)]
                          ).reshape(1, LANES)
    route, xn3 = _router(x1, g_ffn, w_r, b_r)
    blk = min(MOE_BLK, T)
    blk_e, nused, src_start, tok_sorted, pos = _dispatch_tables(route, blk)
    y = _experts(xn3, w_e1, w_e3, w_e2, blk_e, nused, src_start, tok_sorted, blk)
    x2, xn2 = _combine(y, x1, route, g_ple, pos)

    return _ple(xn2, w_pg, p2d, w_ple, x2)


def kernel(x, p, positions, g_mix, w_in, b_mgate, g_mhead, g_cq, w_uq, g_ckv, w_ukv, g_qn, g_kn,
           w_bm, w_ba, w_out, g_ffn, w_rg, b_rg, w_re, b_re, w_e1, w_e3, w_e2, g_ple, w_pg, w_ple):
    B, S, D = x.shape
    T = B * S
    x2d = x.reshape(T, D)
    pos_col = positions.reshape(T, 1).astype(jnp.int32)
    for l in range(p.shape[0]):
        x2d = _layer(x2d, p[l].reshape(T, -1), pos_col, B, S, g_mix[l], w_in[l], b_mgate[l],
                     g_mhead[l], g_cq[l], w_uq[l], g_ckv[l], w_ukv[l], g_qn[l], g_kn[l],
                     w_bm[l], w_ba[l], w_out[l], g_ffn[l], w_rg[l], b_rg[l], w_re[l], b_re[l],
                     w_e1[l], w_e3[l], w_e2[l], g_ple[l], w_pg[l], w_ple[l])
    return x2d.reshape(B, S, D)
```

```python
import functools
import math

import jax
import jax.numpy as jnp
from jax import lax
from jax.experimental import pallas as pl
from jax.experimental.pallas import tpu as pltpu

F32 = jnp.float32
BF16 = jnp.bfloat16

EPS = 1e-6
M_HEADS = 8
M_QK_HD = 128
M_V_HD = 256
A_HEADS = 16
A_NOPE = 128
A_ROPE = 64
A_QK_HD = A_NOPE + A_ROPE
A_V_HD = 128
A_QK_PAD = 256
Q_LORA = 1024
KV_LORA = 512
ROPE_THETA = 10000.0
N_GROUPS = 8
E_PER_GROUP = 8
N_EXPERTS = N_GROUPS * E_PER_GROUP
D_EXPERT = 512
LANES = 128

M_CHUNK = 256
MOE_BLK = 256
ATT_TQ = 1024
ATT_TK = 256
VMEM_LIMIT = 56 * 1024 * 1024
NEG = -1e30
LOG2E = 1.4426950408889634


def _cparams(sem):
    return pltpu.CompilerParams(dimension_semantics=sem, vmem_limit_bytes=VMEM_LIMIT)


def _sigmoid(x):
    return 1.0 / (1.0 + jnp.exp(-x))


def _log_sigmoid(x):
    return jnp.minimum(x, 0.0) - jnp.log(1.0 + jnp.exp(-jnp.abs(x)))


def _rms(xf, g):
    return xf * lax.rsqrt(jnp.mean(xf * xf, axis=-1, keepdims=True) + EPS) * g


def _norm_kernel(x_ref, g_ref, o_ref):
    o_ref[...] = _rms(x_ref[...], g_ref[...]).astype(o_ref.dtype)


def _rmsnorm(x, g, tm=256):
    M, D = x.shape
    tm = min(tm, M)
    return pl.pallas_call(
        _norm_kernel, grid=(M // tm,),
        in_specs=[pl.BlockSpec((tm, D), lambda i: (i, 0)),
                  pl.BlockSpec((1, D), lambda i: (0, 0))],
        out_specs=pl.BlockSpec((tm, D), lambda i: (i, 0)),
        out_shape=jax.ShapeDtypeStruct((M, D), BF16),
        compiler_params=_cparams(("parallel",)), name="rmsnorm")(x, g.reshape(1, D))


def _cache_bf16(w_ref, wbf_ref):
    @pl.when(pl.program_id(1) == 0)
    def _():
        wbf_ref[...] = w_ref[...].astype(BF16)


def _mm_t_kernel(a_ref, wt_ref, o_ref, wbf_ref):
    @pl.when(pl.program_id(1) == 0)
    def _():
        wbf_ref[...] = wt_ref[...].T.astype(BF16)
    o_ref[...] = jnp.dot(a_ref[...], wbf_ref[...],
                         preferred_element_type=F32).astype(o_ref.dtype)


def _matmul_t(a, w_t, n_cols, out_dtype, tm=1024, tn=512, name="matmul"):
    M, K = a.shape
    tm, tn = min(tm, M), min(tn, n_cols)
    return pl.pallas_call(
        _mm_t_kernel, grid=(n_cols // tn, M // tm),
        in_specs=[pl.BlockSpec((tm, K), lambda j, i: (i, 0)),
                  pl.BlockSpec((tn, K), lambda j, i: (j, 0))],
        out_specs=pl.BlockSpec((tm, tn), lambda j, i: (i, j)),
        out_shape=jax.ShapeDtypeStruct((M, n_cols), out_dtype),
        scratch_shapes=[pltpu.VMEM((K, tn), BF16)],
        compiler_params=_cparams(("parallel", "arbitrary")), name=name)(a, w_t)


def _inproj_small_kernel(h_ref, wq_ref, wkv_ref, ws_ref, gq_ref, gkv_ref,
                         cq_ref, ckv_ref, sm_ref):
    h = h_ref[...]
    cq = jnp.dot(h, wq_ref[...], preferred_element_type=F32)
    cq_ref[...] = _rms(cq, gq_ref[...]).astype(cq_ref.dtype)
    ckv = jnp.dot(h, wkv_ref[...], preferred_element_type=F32)
    ckv_ref[...] = _rms(ckv, gkv_ref[...]).astype(ckv_ref.dtype)
    sm_ref[...] = jnp.dot(h, ws_ref[...], preferred_element_type=F32)


def _inproj_small(h, w_cq, w_ckv, w_sm, g_cq, g_ckv, tm=512):
    M, D = h.shape
    tm = min(tm, M)
    nq, nkv, ns = w_cq.shape[1], w_ckv.shape[1], w_sm.shape[1]
    full = lambda n: pl.BlockSpec((D, n), lambda i: (0, 0))
    row = lambda n: pl.BlockSpec((tm, n), lambda i: (i, 0))
    return pl.pallas_call(
        _inproj_small_kernel, grid=(M // tm,),
        in_specs=[row(D), full(nq), full(nkv), full(ns),
                  pl.BlockSpec((1, nq), lambda i: (0, 0)),
                  pl.BlockSpec((1, nkv), lambda i: (0, 0))],
        out_specs=[row(nq), row(nkv), row(ns)],
        out_shape=[jax.ShapeDtypeStruct((M, nq), BF16),
                   jax.ShapeDtypeStruct((M, nkv), BF16),
                   jax.ShapeDtypeStruct((M, ns), F32)],
        compiler_params=_cparams(("parallel",)), name="inproj_small")(
            h, w_cq, w_ckv, w_sm, g_cq.reshape(1, nq), g_ckv.reshape(1, nkv))


def _mlstm_kernel(*refs, reverse, finish, L, goff):
    if finish:
        (q_ref, k_ref, v_ref, gc_ref, gr_ref, hb_ref, o_ref, gm_ref,
         out_ref, C_ref, n_ref, m_ref) = refs
    else:
        q_ref, k_ref, v_ref, gc_ref, gr_ref, out_ref, C_ref, n_ref, m_ref = refs
    H, dk, dv = M_HEADS, M_QK_HD, M_V_HD
    log_scale = -0.5 * math.log(dk)

    @pl.when(pl.program_id(1) == 0)
    def _():
        C_ref[...] = jnp.zeros_like(C_ref)
        n_ref[...] = jnp.zeros_like(n_ref)
        m_ref[...] = jnp.zeros_like(m_ref)

    t_idx = lax.broadcasted_iota(jnp.int32, (L, L), 0)
    s_idx = lax.broadcasted_iota(jnp.int32, (L, L), 1)
    if reverse:
        mask, mask_t = s_idx >= t_idx, t_idx >= s_idx
    else:
        mask, mask_t = s_idx <= t_idx, t_idx <= s_idx

    for h in range(H):
        i_col = gc_ref[:, goff + h:goff + h + 1]
        f_col = gc_ref[:, goff + H + h:goff + H + h + 1]
        i_row = gr_ref[goff + h:goff + h + 1, :]
        f_row = gr_ref[goff + H + h:goff + H + h + 1, :]
        lf_col = _log_sigmoid(f_col)
        lf_row = _log_sigmoid(f_row)
        b_col = jnp.sum(jnp.where(mask, lf_row, 0.0), axis=1, keepdims=True)
        b_row = jnp.sum(jnp.where(mask_t, lf_col, 0.0), axis=0, keepdims=True)
        g = jnp.sum(lf_row, axis=1, keepdims=True)
        m_prev = m_ref[h][:, 0:1]
        dmat = jnp.where(mask, b_col - b_row + i_row, NEG)
        a = b_col + m_prev
        m_t = jnp.maximum(a, jnp.max(dmat, axis=1, keepdims=True))

        q = q_ref[:, h * dk:(h + 1) * dk]
        k = k_ref[:, h * dk:(h + 1) * dk]
        v = v_ref[:, h * dv:(h + 1) * dv]
        qk = lax.dot_general(q, k, (((1,), (1,)), ((), ())),
                             preferred_element_type=F32)
        m_s = m_t - log_scale
        w_intra = jnp.exp(dmat - m_s) * qk
        w_inter = jnp.exp(a - m_s)
        c_state = C_ref[h]
        n_state = n_ref[h]
        q_c = jnp.dot(q, c_state.astype(BF16), preferred_element_type=F32)
        num = jnp.dot(w_intra.astype(BF16), v, preferred_element_type=F32) + w_inter * q_c
        q_n = jnp.sum(q.astype(F32) * n_state, axis=1, keepdims=True)
        den = jnp.sum(w_intra, axis=1, keepdims=True) + w_inter * q_n
        hh = num / jnp.maximum(jnp.abs(den), jnp.exp(-m_t))

        w_col = g - b_col + i_col
        m_new = jnp.maximum(g + m_prev, jnp.max(w_col, axis=0, keepdims=True))
        decay = jnp.exp(g + m_prev - m_new)
        kw = k.astype(F32) * jnp.exp(w_col - m_new)
        C_ref[h] = decay * c_state + lax.dot_general(
            kw.astype(BF16), v, (((0,), (0,)), ((), ())), preferred_element_type=F32)
        n_ref[h] = decay * n_state + jnp.sum(kw, axis=0, keepdims=True)
        m_ref[h] = jnp.broadcast_to(m_new, (1, LANES))

        if finish:
            tot = hh + hb_ref[:, h * dv:(h + 1) * dv]
            y = _rms(tot, gm_ref[h:h + 1, :])
            gate = _sigmoid(o_ref[:, h * dv:(h + 1) * dv].astype(F32))
            out_ref[:, h * dv:(h + 1) * dv] = (y * gate).astype(out_ref.dtype)
        else:
            out_ref[:, h * dv:(h + 1) * dv] = hh


def _mlstm(z_a, g_col, g_row, B, S, *, reverse, hb=None, g_mhead=None):
    H, dk, dv = M_HEADS, M_QK_HD, M_V_HD
    L = min(M_CHUNK, S)
    nc = S // L
    T = B * S
    finish = not reverse
    if reverse:
        cidx = lambda b, c: b * nc + (nc - 1 - c)
        ridx = lambda b, c: nc - 1 - c
    else:
        cidx = lambda b, c: b * nc + c
        ridx = lambda b, c: c
    mqk, mv = H * dk, H * dv
    in_specs = [
        pl.BlockSpec((L, mqk), lambda b, c: (cidx(b, c), 0)),
        pl.BlockSpec((L, mqk), lambda b, c: (cidx(b, c), 1)),
        pl.BlockSpec((L, mv), lambda b, c: (cidx(b, c), 1)),
        pl.BlockSpec((L, 4 * H), lambda b, c: (cidx(b, c), 0)),
        pl.BlockSpec((None, 4 * H, L), lambda b, c: (b, 0, ridx(b, c))),
    ]
    args = [z_a, z_a, z_a, g_col, g_row]
    if finish:
        in_specs += [
            pl.BlockSpec((L, mv), lambda b, c: (cidx(b, c), 0)),
            pl.BlockSpec((L, mv), lambda b, c: (cidx(b, c), 2)),
            pl.BlockSpec((H, dv), lambda b, c: (0, 0)),
        ]
        args += [hb, z_a, g_mhead.reshape(H, dv)]
    kern = functools.partial(_mlstm_kernel, reverse=reverse, finish=finish, L=L,
                             goff=2 * H if reverse else 0)
    return pl.pallas_call(
        kern, grid=(B, nc), in_specs=in_specs,
        out_specs=pl.BlockSpec((L, mv), lambda b, c: (cidx(b, c), 0)),
        out_shape=jax.ShapeDtypeStruct((T, mv), BF16 if finish else F32),
        scratch_shapes=[pltpu.VMEM((H, dk, dv), F32), pltpu.VMEM((H, 1, dk), F32),
                        pltpu.VMEM((H, 1, LANES), F32)],
        compiler_params=_cparams(("parallel", "arbitrary")),
        name="mlstm_bwd" if reverse else "mlstm_fwd")(*args)


def _rope_table_kernel(pos_ref, inv_ref, cos_ref, sin_ref):
    ang = pos_ref[...].astype(F32) * inv_ref[...]
    lane = lax.broadcasted_iota(jnp.int32, ang.shape, 1)
    cos_ref[...] = jnp.cos(ang)
    sin_ref[...] = jnp.where(lane < A_ROPE // 2, -jnp.sin(ang), jnp.sin(ang))


def _rope_tables(pos_col, inv_row, tm=1024):
    T = pos_col.shape[0]
    tm = min(tm, T)
    return pl.pallas_call(
        _rope_table_kernel, grid=(T // tm,),
        in_specs=[pl.BlockSpec((tm, 1), lambda i: (i, 0)),
                  pl.BlockSpec((1, LANES), lambda i: (0, 0))],
        out_specs=[pl.BlockSpec((tm, LANES), lambda i: (i, 0))] * 2,
        out_shape=[jax.ShapeDtypeStruct((T, LANES), F32)] * 2,
        compiler_params=_cparams(("parallel",)), name="rope_tables")(pos_col, inv_row)


def _rope(r, cos_ref, sin_ref):
    lane = lax.broadcasted_iota(jnp.int32, r.shape, 1)
    half = A_ROPE // 2
    swapped = jnp.where(lane < half, pltpu.roll(r, LANES - half, 1), pltpu.roll(r, half, 1))
    return r * cos_ref[...] + swapped * sin_ref[...]


PREP_HEADS = 4


def _q_prep_kernel(c_ref, w_ref, g_ref, cos_ref, sin_ref, o_ref):
    c = c_ref[...]
    for h in range(w_ref.shape[0]):
        acc = jnp.dot(c, w_ref[h], preferred_element_type=F32)
        ss = jnp.sum(acc * acc, axis=1, keepdims=True)
        y = acc * lax.rsqrt(ss / A_QK_HD + EPS) * g_ref[...] * (A_QK_HD ** -0.5 * LOG2E)
        o_ref[h, :, 0:A_NOPE] = y[:, 0:A_NOPE].astype(o_ref.dtype)
        o_ref[h, :, A_NOPE:] = _rope(y[:, A_NOPE:], cos_ref, sin_ref).astype(o_ref.dtype)


def _q_prep(cq_n, w_uq_p, g_qn_p, cos_t, sin_t, tm=1024):
    T, R = cq_n.shape
    tm = min(tm, T)
    H, hb = A_HEADS, PREP_HEADS
    return pl.pallas_call(
        _q_prep_kernel, grid=(T // tm, H // hb),
        in_specs=[pl.BlockSpec((tm, R), lambda i, h: (i, 0)),
                  pl.BlockSpec((hb, R, A_QK_PAD), lambda i, h: (h, 0, 0)),
                  pl.BlockSpec((1, A_QK_PAD), lambda i, h: (0, 0)),
                  pl.BlockSpec((tm, LANES), lambda i, h: (i, 0)),
                  pl.BlockSpec((tm, LANES), lambda i, h: (i, 0))],
        out_specs=pl.BlockSpec((hb, tm, A_QK_PAD), lambda i, h: (h, i, 0)),
        out_shape=jax.ShapeDtypeStruct((H, T, A_QK_PAD), BF16),
        compiler_params=_cparams(("parallel", "parallel")), name="mla_q_prep")(
            cq_n, w_uq_p, g_qn_p, cos_t, sin_t)


def _kv_prep_kernel(c_ref, w_ref, sm_ref, g_ref, cos_ref, sin_ref, kt_ref, v_ref):
    c = c_ref[...]
    lane = lax.broadcasted_iota(jnp.int32, sm_ref.shape, 1)
    k_pe = jnp.where(lane < A_ROPE, sm_ref[...], 0.0)
    ss_pe = jnp.sum(k_pe * k_pe, axis=1, keepdims=True)
    for h in range(w_ref.shape[0]):
        acc = jnp.dot(c, w_ref[h], preferred_element_type=F32)
        k_nope = acc[:, 0:A_NOPE]
        ss = jnp.sum(k_nope * k_nope, axis=1, keepdims=True) + ss_pe
        r = lax.rsqrt(ss / A_QK_HD + EPS)
        kn = k_nope * r * g_ref[:, 0:A_NOPE]
        kr = _rope(k_pe * r * g_ref[:, A_NOPE:], cos_ref, sin_ref)
        kt_ref[h, 0:A_NOPE, :] = kn.T.astype(kt_ref.dtype)
        kt_ref[h, A_NOPE:, :] = kr.T.astype(kt_ref.dtype)
        v_ref[h, :, 0:A_V_HD] = acc[:, A_NOPE:].astype(v_ref.dtype)
        v_ref[h, :, A_V_HD:] = jnp.ones((acc.shape[0], A_V_HD), v_ref.dtype)


def _kv_prep(ckv_n, w_ukv_h, small, g_kn_p, cos_t, sin_t, tm=1024):
    T, R = ckv_n.shape
    tm = min(tm, T)
    H, hb = A_HEADS, PREP_HEADS
    return pl.pallas_call(
        _kv_prep_kernel, grid=(T // tm, H // hb),
        in_specs=[pl.BlockSpec((tm, R), lambda i, h: (i, 0)),
                  pl.BlockSpec((hb, R, A_NOPE + A_V_HD), lambda i, h: (h, 0, 0)),
                  pl.BlockSpec((tm, LANES), lambda i, h: (i, 0)),
                  pl.BlockSpec((1, A_QK_PAD), lambda i, h: (0, 0)),
                  pl.BlockSpec((tm, LANES), lambda i, h: (i, 0)),
                  pl.BlockSpec((tm, LANES), lambda i, h: (i, 0))],
        out_specs=[pl.BlockSpec((hb, A_QK_PAD, tm), lambda i, h: (h, 0, i)),
                   pl.BlockSpec((hb, tm, 2 * A_V_HD), lambda i, h: (h, i, 0))],
        out_shape=[jax.ShapeDtypeStruct((H, A_QK_PAD, T), BF16),
                   jax.ShapeDtypeStruct((H, T, 2 * A_V_HD), BF16)],
        compiler_params=_cparams(("parallel", "parallel")), name="mla_kv_prep")(
            ckv_n, w_ukv_h, small, g_kn_p, cos_t, sin_t)


def _attn_kernel(q_ref, kt_ref, v_ref, o_ref, *, tk):
    q = q_ref[...]
    S = kt_ref.shape[1]
    m = acc = None
    for j in range(S // tk):
        s = jnp.dot(q, kt_ref[:, j * tk:(j + 1) * tk], preferred_element_type=F32)
        mj = jnp.max(s, axis=1, keepdims=True)
        vj = v_ref[j * tk:(j + 1) * tk, :]
        if j == 0:
            m = mj
            acc = jnp.dot(jnp.exp2(s - m).astype(BF16), vj, preferred_element_type=F32)
        else:
            m_new = jnp.maximum(m, mj)
            acc = (jnp.exp2(m - m_new) * acc
                   + jnp.dot(jnp.exp2(s - m_new).astype(BF16), vj, preferred_element_type=F32))
            m = m_new
    o_ref[...] = (acc[:, 0:A_V_HD] / acc[:, A_V_HD:A_V_HD + 1]).astype(o_ref.dtype)


def _attention(q, kt, v, B, S):
    H, T, dq = q.shape
    tq, tk = min(ATT_TQ, S), min(ATT_TK, S)
    nq = S // tq
    return pl.pallas_call(
        functools.partial(_attn_kernel, tk=tk), grid=(B, H, nq),
        in_specs=[pl.BlockSpec((None, tq, dq), lambda b, h, i: (h, b * nq + i, 0)),
                  pl.BlockSpec((None, dq, S), lambda b, h, i: (h, 0, b)),
                  pl.BlockSpec((None, S, 2 * A_V_HD), lambda b, h, i: (h, b, 0))],
        out_specs=pl.BlockSpec((tq, A_V_HD), lambda b, h, i: (b * nq + i, h)),
        out_shape=jax.ShapeDtypeStruct((T, H * A_V_HD), BF16),
        compiler_params=_cparams(("parallel", "parallel", "parallel")), name="mla_attention")(
            q, kt, v)


def _merge_kernel(hm_ref, ha_ref, wm_ref, wa_ref, bm_ref, ba_ref, o_ref, wm_bf, wa_bf):
    _cache_bf16(wm_ref, wm_bf)
    _cache_bf16(wa_ref, wa_bf)
    ym = jnp.dot(hm_ref[...], wm_bf[...], preferred_element_type=F32)
    ya = jnp.dot(ha_ref[...], wa_bf[...], preferred_element_type=F32)
    o_ref[...] = (_sigmoid(bm_ref[...].astype(F32)) * ym
                  + _sigmoid(ba_ref[...].astype(F32)) * ya).astype(o_ref.dtype)


def _merge(hm, ha, w_bm, w_ba, z_a, br_off, tm=1024, tn=512):
    T, K = hm.shape
    D = w_bm.shape[1]
    tm, tn = min(tm, T), min(tn, D)
    ob = br_off // tn
    return pl.pallas_call(
        _merge_kernel, grid=(D // tn, T // tm),
        in_specs=[pl.BlockSpec((tm, K), lambda j, i: (i, 0)),
                  pl.BlockSpec((tm, K), lambda j, i: (i, 0)),
                  pl.BlockSpec((K, tn), lambda j, i: (0, j)),
                  pl.BlockSpec((K, tn), lambda j, i: (0, j)),
                  pl.BlockSpec((tm, tn), lambda j, i: (i, ob + j)),
                  pl.BlockSpec((tm, tn), lambda j, i: (i, ob + D // tn + j))],
        out_specs=pl.BlockSpec((tm, tn), lambda j, i: (i, j)),
        out_shape=jax.ShapeDtypeStruct((T, D), BF16),
        scratch_shapes=[pltpu.VMEM((K, tn), BF16), pltpu.VMEM((K, tn), BF16)],
        compiler_params=_cparams(("parallel", "arbitrary")), name="merge")(
            hm, ha, w_bm, w_ba, z_a, z_a)


def _mm_res_kernel(a_ref, w_ref, r_ref, o_ref, wbf_ref):
    _cache_bf16(w_ref, wbf_ref)
    o_ref[...] = r_ref[...] + jnp.dot(a_ref[...], wbf_ref[...], preferred_element_type=F32)


def _matmul_residual(a, b, res, tm=1024, tn=512):
    M, K = a.shape
    N = b.shape[1]
    tm, tn = min(tm, M), min(tn, N)
    return pl.pallas_call(
        _mm_res_kernel, grid=(N // tn, M // tm),
        in_specs=[pl.BlockSpec((tm, K), lambda j, i: (i, 0)),
                  pl.BlockSpec((K, tn), lambda j, i: (0, j)),
                  pl.BlockSpec((tm, tn), lambda j, i: (i, j))],
        out_specs=pl.BlockSpec((tm, tn), lambda j, i: (i, j)),
        out_shape=jax.ShapeDtypeStruct((M, N), F32),
        scratch_shapes=[pltpu.VMEM((K, tn), BF16)],
        compiler_params=_cparams(("parallel", "arbitrary")), name="out_proj")(a, b, res)


def _router_kernel(x_ref, g_ref, w_ref, b_ref, o_ref, xn3_ref):
    xf = _rms(x_ref[...], g_ref[...])
    nch = xf.shape[1] // (2 * LANES)
    tm = xf.shape[0]
    pitch = xn3_ref.shape[0] // tm
    for c in range(pitch):
        xn3_ref[pl.ds(c, tm, stride=pitch), :] = (
            _pack_bf16_pair(xf[:, 2 * c * LANES:(2 * c + 1) * LANES],
                            xf[:, (2 * c + 1) * LANES:(2 * c + 2) * LANES])
            if c < nch else jnp.zeros((tm, LANES), jnp.int32))
    xn = xf.astype(BF16)
    logits = jnp.dot(xn, w_ref[...], preferred_element_type=F32) + b_ref[...]
    lane = lax.broadcasted_iota(jnp.int32, logits.shape, 1)
    big = jnp.int32(1 << 20)
    lg = jnp.where(lane < N_GROUPS, logits, NEG)
    gmax = jnp.max(lg, axis=1, keepdims=True)
    g_idx = jnp.min(jnp.where(lg == gmax, lane, big), axis=1, keepdims=True)
    g_w = 1.0 / jnp.sum(jnp.exp(lg - gmax), axis=1, keepdims=True)
    lo = N_GROUPS + g_idx * E_PER_GROUP
    le = jnp.where((lane >= lo) & (lane < lo + E_PER_GROUP), logits, NEG)
    v1 = jnp.max(le, axis=1, keepdims=True)
    i1 = jnp.min(jnp.where(le == v1, lane, big), axis=1, keepdims=True)
    le2 = jnp.where(lane == i1, NEG, le)
    v2 = jnp.max(le2, axis=1, keepdims=True)
    i2 = jnp.min(jnp.where(le2 == v2, lane, big), axis=1, keepdims=True)
    e21 = jnp.exp(v2 - v1)
    w1 = g_w / (1.0 + e21)
    w2 = g_w * e21 / (1.0 + e21)
    o_ref[...] = jnp.where(
        lane == 0, (i1 - N_GROUPS).astype(F32),
        jnp.where(lane == 1, (i2 - N_GROUPS).astype(F32),
                  jnp.where(lane == 2, w1, jnp.where(lane == 3, w2, 0.0))))


def _router(x, g, w_r, b_r, tm=256):
    T, D = x.shape
    tm = min(tm, T)
    pitch = _row_pitch(D // (2 * LANES))
    return pl.pallas_call(
        _router_kernel, grid=(T // tm,),
        in_specs=[pl.BlockSpec((tm, D), lambda i: (i, 0)),
                  pl.BlockSpec((1, D), lambda i: (0, 0)),
                  pl.BlockSpec((D, LANES), lambda i: (0, 0)),
                  pl.BlockSpec((1, LANES), lambda i: (0, 0))],
        out_specs=[pl.BlockSpec((tm, LANES), lambda i: (i, 0)),
                   pl.BlockSpec((tm * pitch, LANES), lambda i: (i, 0))],
        out_shape=[jax.ShapeDtypeStruct((T, LANES), F32),
                   jax.ShapeDtypeStruct((T * pitch, LANES), jnp.int32)],
        compiler_params=_cparams(("parallel",)), name="router")(x, g.reshape(1, D), w_r, b_r)


def _pack_bf16_pair(a, b):
    ua = lax.bitcast_convert_type(a.astype(BF16).astype(F32), jnp.int32)
    ub = lax.bitcast_convert_type(b.astype(BF16).astype(F32), jnp.int32)
    return lax.shift_right_logical(ua, jnp.int32(16)) | (ub & jnp.int32(-65536))


def _unpack_bf16_pair(u):
    return [lax.bitcast_convert_type(lax.shift_left(u, jnp.int32(16)), F32),
            lax.bitcast_convert_type(u & jnp.int32(-65536), F32)]


def _row_pitch(nch):
    return (nch + 7) // 8 * 8 + 8


def _gather_rows(start_one, wait_one, n, unroll=8):
    def start():
        def body(r, carry):
            start_one(r)
            return carry
        lax.fori_loop(0, n, body, 0, unroll=unroll)

    def wait():
        def body(r, carry):
            wait_one(r)
            return carry
        lax.fori_loop(0, n, body, 0, unroll=unroll)
    return start, wait


def _expert_up_kernel(blk_e_ref, nused_ref, src_ref, tok_ref, x_hbm, w1_ref, w3_ref,
                      h_ref, xbuf, sem, *, blk, n_assign, nch, pitch, kch):
    i = pl.program_id(0)
    nused = nused_ref[0]

    def row_copy(step, r, slot):
        idx = jnp.minimum(src_ref[step] + r, n_assign - 1)
        return pltpu.make_async_copy(
            x_hbm.at[pl.ds(pl.multiple_of(tok_ref[idx] * pitch, 8), nch), :],
            xbuf.at[slot, pl.ds(pl.multiple_of(r * pitch, 8), nch), :], sem.at[slot])

    def gather(step, slot):
        return _gather_rows(lambda r: row_copy(step, r, slot).start(),
                            lambda r: row_copy(step, r, slot).wait(), blk)

    @pl.when(i == 0)
    def _():
        gather(0, 0)[0]()

    @pl.when(i + 1 < nused)
    def _():
        gather(i + 1, (i + 1) % 2)[0]()

    @pl.when(i < nused)
    def _():
        slot = i % 2
        gather(i, slot)[1]()
        parts = []
        for c in range(nch):
            parts += _unpack_bf16_pair(xbuf[slot, pl.ds(c, blk, stride=pitch), :])
        xn = jnp.concatenate([t.astype(BF16) for t in parts], axis=1)
        h1 = h3 = None
        for k in range(xn.shape[1] // kch):
            xk = xn[:, k * kch:(k + 1) * kch]
            d1 = jnp.dot(xk, w1_ref[k * kch:(k + 1) * kch, :].astype(BF16),
                         preferred_element_type=F32)
            d3 = jnp.dot(xk, w3_ref[k * kch:(k + 1) * kch, :].astype(BF16),
                         preferred_element_type=F32)
            h1 = d1 if h1 is None else h1 + d1
            h3 = d3 if h3 is None else h3 + d3
        h_ref[...] = (h1 * _sigmoid(h1) * h3).astype(h_ref.dtype)

    @pl.when(i >= nused)
    def _():
        h_ref[...] = jnp.zeros_like(h_ref)


def _expert_down_kernel(blk_e_ref, nused_ref, h_ref, w2_ref, y_ref, *, blk, nch, pitch, ych):
    i = pl.program_id(0)
    nused = nused_ref[0]

    @pl.when(i < nused)
    def _():
        hdn = h_ref[...]
        for j in range(nch * 2 * LANES // ych):
            yj = jnp.dot(hdn, w2_ref[:, j * ych:(j + 1) * ych].astype(BF16),
                         preferred_element_type=F32)
            for c in range(ych // (2 * LANES)):
                y_ref[pl.ds(j * (ych // (2 * LANES)) + c, blk, stride=pitch), :] = (
                    _pack_bf16_pair(yj[:, 2 * c * LANES:(2 * c + 1) * LANES],
                                    yj[:, (2 * c + 1) * LANES:(2 * c + 2) * LANES]))
        for c in range(nch, pitch):
            y_ref[pl.ds(c, blk, stride=pitch), :] = jnp.zeros((blk, LANES), jnp.int32)

    @pl.when(i >= nused)
    def _():
        y_ref[...] = jnp.zeros_like(y_ref)


def _experts(xn3, w1, w3, w2, blk_e, nused, src_start, tok_sorted, blk):
    E, D, De = w1.shape
    nch = D // (2 * LANES)
    pitch = _row_pitch(nch)
    nb = blk_e.shape[0]
    n_assign = tok_sorted.shape[0]
    wmap4 = lambda i, be, nu, sr, tk: (be[i], 0, 0)
    up_spec = pltpu.PrefetchScalarGridSpec(
        num_scalar_prefetch=4, grid=(nb,),
        in_specs=[pl.BlockSpec(memory_space=pl.ANY),
                  pl.BlockSpec((None, D, De), wmap4),
                  pl.BlockSpec((None, D, De), wmap4)],
        out_specs=pl.BlockSpec((blk, De), lambda i, be, nu, sr, tk: (i, 0)),
        scratch_shapes=[pltpu.VMEM((2, blk * pitch, LANES), jnp.int32),
                        pltpu.SemaphoreType.DMA((2,))])
    hdn = pl.pallas_call(
        functools.partial(_expert_up_kernel, blk=blk, n_assign=n_assign, nch=nch, pitch=pitch,
                          kch=min(1024, D)),
        grid_spec=up_spec,
        out_shape=jax.ShapeDtypeStruct((nb * blk, De), BF16),
        compiler_params=_cparams(("arbitrary",)), name="experts_up")(
            blk_e, nused, src_start, tok_sorted, xn3, w1, w3)
    down_spec = pltpu.PrefetchScalarGridSpec(
        num_scalar_prefetch=2, grid=(nb,),
        in_specs=[pl.BlockSpec((blk, De), lambda i, be, nu: (i, 0)),
                  pl.BlockSpec((None, De, D), lambda i, be, nu: (be[i], 0, 0))],
        out_specs=pl.BlockSpec((blk * pitch, LANES), lambda i, be, nu: (i, 0)))
    return pl.pallas_call(
        functools.partial(_expert_down_kernel, blk=blk, nch=nch, pitch=pitch, ych=min(1024, D)),
        grid_spec=down_spec,
        out_shape=jax.ShapeDtypeStruct((nb * blk * pitch, LANES), jnp.int32),
        compiler_params=_cparams(("arbitrary",)), name="experts_down")(blk_e, nused, hdn, w2)


def _combine_kernel(pos_ref, y_hbm, x_ref, rt_ref, g_ref, x2_ref, xn_ref, ybuf, sem, *, tc, nch,
                    pitch):
    i = pl.program_id(0)
    n = pl.num_programs(0)

    def row_copy(step, r, slot):
        return pltpu.make_async_copy(
            y_hbm.at[pl.ds(pl.multiple_of(pos_ref[step * 2 * tc + r] * pitch, 8), nch), :],
            ybuf.at[slot, pl.ds(pl.multiple_of(r * pitch, 8), nch), :], sem.at[slot])

    def gather(step, slot):
        return _gather_rows(lambda r: row_copy(step, r, slot).start(),
                            lambda r: row_copy(step, r, slot).wait(), 2 * tc)

    @pl.when(i == 0)
    def _():
        gather(0, 0)[0]()

    @pl.when(i + 1 < n)
    def _():
        gather(i + 1, (i + 1) % 2)[0]()

    slot = i % 2
    gather(i, slot)[1]()
    y0, y1 = [], []
    for c in range(nch):
        y0 += _unpack_bf16_pair(ybuf[slot, pl.ds(c, tc, stride=pitch), :])
        y1 += _unpack_bf16_pair(ybuf[slot, pl.ds(tc * pitch + c, tc, stride=pitch), :])
    y0 = jnp.concatenate(y0, axis=1)
    y1 = jnp.concatenate(y1, axis=1)
    x2 = x_ref[...] + rt_ref[:, 2:3] * y0 + rt_ref[:, 3:4] * y1
    x2_ref[...] = x2
    xn_ref[...] = _rms(x2, g_ref[...]).astype(xn_ref.dtype)


def _combine(y, x, route, g, pos, tc=128):
    T, D = x.shape
    tc = min(tc, T)
    nch = D // (2 * LANES)
    pitch = _row_pitch(nch)
    grid_spec = pltpu.PrefetchScalarGridSpec(
        num_scalar_prefetch=1, grid=(T // tc,),
        in_specs=[pl.BlockSpec(memory_space=pl.ANY),
                  pl.BlockSpec((tc, D), lambda i, ps: (i, 0)),
                  pl.BlockSpec((tc, LANES), lambda i, ps: (i, 0)),
                  pl.BlockSpec((1, D), lambda i, ps: (0, 0))],
        out_specs=[pl.BlockSpec((tc, D), lambda i, ps: (i, 0)),
                   pl.BlockSpec((tc, D), lambda i, ps: (i, 0))],
        scratch_shapes=[pltpu.VMEM((2, 2 * tc * pitch, LANES), jnp.int32),
                        pltpu.SemaphoreType.DMA((2,))])
    pos_tiled = pos.reshape(T // tc, tc, 2).transpose(0, 2, 1).reshape(-1)
    return pl.pallas_call(
        functools.partial(_combine_kernel, tc=tc, nch=nch, pitch=pitch), grid_spec=grid_spec,
        out_shape=[jax.ShapeDtypeStruct((T, D), F32), jax.ShapeDtypeStruct((T, D), BF16)],
        compiler_params=_cparams(("arbitrary",)), name="combine")(
            pos_tiled, y, x, route, g.reshape(1, D))


def _ple_kernel(xn_ref, wg_ref, p_ref, wp_ref, x_ref, o_ref, wg_bf, wp_bf):
    _cache_bf16(wg_ref, wg_bf)
    _cache_bf16(wp_ref, wp_bf)
    gate = _sigmoid(jnp.dot(xn_ref[...], wg_bf[...], preferred_element_type=F32))
    emb = jnp.dot(p_ref[...].astype(BF16), wp_bf[...], preferred_element_type=F32)
    o_ref[...] = x_ref[...] + gate * emb


def _ple(xn, w_pg, p, w_ple, x, tm=1024, tn=512):
    T, D = xn.shape
    Pd = p.shape[1]
    tm, tn = min(tm, T), min(tn, D)
    return pl.pallas_call(
        _ple_kernel, grid=(D // tn, T // tm),
        in_specs=[pl.BlockSpec((tm, D), lambda j, i: (i, 0)),
                  pl.BlockSpec((D, tn), lambda j, i: (0, j)),
                  pl.BlockSpec((tm, Pd), lambda j, i: (i, 0)),
                  pl.BlockSpec((Pd, tn), lambda j, i: (0, j)),
                  pl.BlockSpec((tm, tn), lambda j, i: (i, j))],
        out_specs=pl.BlockSpec((tm, tn), lambda j, i: (i, j)),
        out_shape=jax.ShapeDtypeStruct((T, D), F32),
        scratch_shapes=[pltpu.VMEM((D, tn), BF16), pltpu.VMEM((Pd, tn), BF16)],
        compiler_params=_cparams(("parallel", "arbitrary")), name="ple")(xn, w_pg, p, w_ple, x)


def _dispatch_tables(route, blk):
    T = route.shape[0]
    A = 2 * T
    eid = route[:, 0:2].astype(jnp.int32).reshape(-1)
    iota = jnp.arange(A, dtype=jnp.int32)
    experts = jnp.arange(N_EXPERTS, dtype=jnp.int32)
    _, order = lax.sort((eid, iota), num_keys=1)
    _, inv = lax.sort((order, iota), num_keys=1)
    onehot = eid[:, None] == experts[None, :]
    counts = jnp.sum(onehot, axis=0, dtype=jnp.int32)
    starts = jnp.cumsum(counts) - counts
    nblk = (counts + blk - 1) // blk
    bends = jnp.cumsum(nblk)
    bstarts = bends - nblk
    nb = (A + N_EXPERTS * blk) // blk
    b = jnp.arange(nb, dtype=jnp.int32)
    blk_e = jnp.minimum(jnp.sum(b[:, None] >= bends[None, :], axis=1),
                        N_EXPERTS - 1).astype(jnp.int32)
    shift = starts - bstarts * blk
    src_start = b * blk + jnp.sum(
        jnp.where(blk_e[:, None] == experts[None, :], shift[None, :], 0), axis=1)
    nused = bends[-1].astype(jnp.int32).reshape(1)
    pos = inv - jnp.sum(jnp.where(onehot, shift[None, :], 0), axis=1)
    return blk_e, nused, src_start.astype(jnp.int32), order // 2, pos.reshape(T, 2)


def _layer(x2d, p2d, pos_col, B, S, g_mix, w_in, b_mgate, g_mhead, g_cq, w_uq, g_ckv, w_ukv,
           g_qn, g_kn, w_bm, w_ba, w_out, g_ffn, w_rg, b_rg, w_re, b_re, w_e1, w_e3, w_e2,
           g_ple, w_pg, w_ple):
    T, D = x2d.shape
    H = M_HEADS
    mqk, mv = H * M_QK_HD, H * M_V_HD
    o_gate = 2 * mqk + 2 * mv
    o_cq = o_gate + 4 * H
    o_ckv = o_cq + Q_LORA
    o_kpe = o_ckv + KV_LORA
    o_br = o_kpe + A_ROPE
    w_t = jnp.swapaxes(w_in, 0, 1)
    w_br_t, w_mid_t = lax.optimization_barrier((w_t[o_br:], w_t[o_gate:o_br]))
    w_cq = w_mid_t[o_cq - o_gate:o_ckv - o_gate].T.astype(BF16)
    w_ckv = w_mid_t[o_ckv - o_gate:o_kpe - o_gate].T.astype(BF16)
    w_sm = jnp.concatenate(
        [w_mid_t[o_kpe - o_gate:], w_mid_t[:4 * H],
         jnp.zeros((LANES - A_ROPE - 4 * H, D), F32)], axis=0).T.astype(BF16)

    h = _rmsnorm(x2d, g_mix)
    z_a = _matmul_t(h, w_t, o_gate, BF16, name="inproj_mlstm")
    z_br = _matmul_t(h, w_br_t, 2 * D, BF16, name="inproj_gates")
    cq_n, ckv_n, small = _inproj_small(h, w_cq, w_ckv, w_sm, g_cq, g_ckv)

    g_col = small[:, A_ROPE:A_ROPE + 4 * H] + b_mgate[None, :]
    g_row = g_col.reshape(B, S, 4 * H).transpose(0, 2, 1)
    hb = _mlstm(z_a, g_col, g_row, B, S, reverse=True)
    hm = _mlstm(z_a, g_col, g_row, B, S, reverse=False, hb=hb, g_mhead=g_mhead)

    pad_q = A_QK_PAD - A_QK_HD
    w_uq_p = jnp.pad(w_uq.reshape(Q_LORA, A_HEADS, A_QK_HD).transpose(1, 0, 2),
                     ((0, 0), (0, 0), (0, pad_q))).astype(BF16)
    w_ukv_h = w_ukv.reshape(KV_LORA, A_HEADS, A_NOPE + A_V_HD).transpose(1, 0, 2).astype(BF16)
    g_qn_p = jnp.pad(g_qn, (0, pad_q)).reshape(1, A_QK_PAD)
    g_kn_p = jnp.pad(g_kn, (0, pad_q)).reshape(1, A_QK_PAD)
    inv = ROPE_THETA ** (-jnp.arange(0, A_ROPE, 2, dtype=F32) / A_ROPE)
    inv_row = jnp.concatenate([inv, inv, jnp.zeros((LANES - A_ROPE,), F32)]).reshape(1, LANES)
    cos_t, sin_t = _rope_tables(pos_col, inv_row)
    q = _q_prep(cq_n, w_uq_p, g_qn_p, cos_t, sin_t)
    kt, v = _kv_prep(ckv_n, w_ukv_h, small, g_kn_p, cos_t, sin_t)
    ha = _attention(q, kt, v, B, S)

    mix = _merge(hm, ha, w_bm, w_ba, z_br, 0)
    x1 = _matmul_residual(mix, w_out, x2d)

    w_r = jnp.concatenate([w_rg, w_re, jnp.zeros((D, LANES - N_GROUPS - N_EXPERTS), F32)],
                          axis=1).astype(BF16)
    b_r = jnp.concatenate([b_rg, b_re, jnp.zeros((LANES - N_GROUPS - N_EXPERTS,), F32)]
                          ).reshape(1, LANES)
    route, xn3 = _router(x1, g_ffn, w_r, b_r)
    blk = min(MOE_BLK, T)
    blk_e, nused, src_start, tok_sorted, pos = _dispatch_tables(route, blk)
    y = _experts(xn3, w_e1, w_e3, w_e2, blk_e, nused, src_start, tok_sorted, blk)
    x2, xn2 = _combine(y, x1, route, g_ple, pos)

    return _ple(xn2, w_pg, p2d, w_ple, x2)


def kernel(x, p, positions, g_mix, w_in, b_mgate, g_mhead, g_cq, w_uq, g_ckv, w_ukv, g_qn, g_kn,
           w_bm, w_ba, w_out, g_ffn, w_rg, b_rg, w_re, b_re, w_e1, w_e3, w_e2, g_ple, w_pg, w_ple):
    B, S, D = x.shape
    T = B * S
    x2d = x.reshape(T, D)
    pos_col = positions.reshape(T, 1).astype(jnp.int32)
    for l in range(p.shape[0]):
        x2d = _layer(x2d, p[l].reshape(T, -1), pos_col, B, S, g_mix[l], w_in[l], b_mgate[l],
                     g_mhead[l], g_cq[l], w_uq[l], g_ckv[l], w_ukv[l], g_qn[l], g_kn[l],
                     w_bm[l], w_ba[l], w_out[l], g_ffn[l], w_rg[l], b_rg[l], w_re[l], b_re[l],
                     w_e1[l], w_e3[l], w_e2[l], g_ple[l], w_pg[l], w_ple[l])
    return x2d.reshape(B, S, D)
```

```python
import functools
import math

import jax
import jax.numpy as jnp
from jax import lax
from jax.experimental import pallas as pl
from jax.experimental.pallas import tpu as pltpu

F32 = jnp.float32
BF16 = jnp.bfloat16

EPS = 1e-6
M_HEADS = 8
M_QK_HD = 128
M_V_HD = 256
A_HEADS = 16
A_NOPE = 128
A_ROPE = 64
A_QK_HD = A_NOPE + A_ROPE
A_V_HD = 128
A_QK_PAD = 256
Q_LORA = 1024
KV_LORA = 512
ROPE_THETA = 10000.0
N_GROUPS = 8
E_PER_GROUP = 8
N_EXPERTS = N_GROUPS * E_PER_GROUP
D_EXPERT = 512
LANES = 128

M_CHUNK = 256
MOE_BLK = 256
ATT_TQ = 1024
ATT_TK = 256
VMEM_LIMIT = 56 * 1024 * 1024
NEG = -1e30
LOG2E = 1.4426950408889634


def _cparams(sem):
    return pltpu.CompilerParams(dimension_semantics=sem, vmem_limit_bytes=VMEM_LIMIT)


def _sigmoid(x):
    return 1.0 / (1.0 + jnp.exp(-x))


def _log_sigmoid(x):
    return jnp.minimum(x, 0.0) - jnp.log(1.0 + jnp.exp(-jnp.abs(x)))


def _rms(xf, g):
    return xf * lax.rsqrt(jnp.mean(xf * xf, axis=-1, keepdims=True) + EPS) * g


def _norm_kernel(x_ref, g_ref, o_ref):
    o_ref[...] = _rms(x_ref[...], g_ref[...]).astype(o_ref.dtype)


def _rmsnorm(x, g, tm=512):
    M, D = x.shape
    tm = min(tm, M)
    return pl.pallas_call(
        _norm_kernel, grid=(M // tm,),
        in_specs=[pl.BlockSpec((tm, D), lambda i: (i, 0)),
                  pl.BlockSpec((1, D), lambda i: (0, 0))],
        out_specs=pl.BlockSpec((tm, D), lambda i: (i, 0)),
        out_shape=jax.ShapeDtypeStruct((M, D), BF16),
        compiler_params=_cparams(("parallel",)), name="rmsnorm")(x, g.reshape(1, D))


def _cache_bf16(w_ref, wbf_ref):
    @pl.when(pl.program_id(1) == 0)
    def _():
        wbf_ref[...] = w_ref[...].astype(BF16)


def _mm_t_kernel(a_ref, wt_ref, o_ref, wbf_ref):
    @pl.when(pl.program_id(1) == 0)
    def _():
        wbf_ref[...] = wt_ref[...].astype(BF16)
    o_ref[...] = lax.dot_general(a_ref[...], wbf_ref[...], (((1,), (1,)), ((), ())),
                                 preferred_element_type=F32).astype(o_ref.dtype)


def _matmul_t(a, w_t, n_cols, out_dtype, tm=1024, tn=512, name="matmul"):
    M, K = a.shape
    tm, tn = min(tm, M), min(tn, n_cols)
    return pl.pallas_call(
        _mm_t_kernel, grid=(n_cols // tn, M // tm),
        in_specs=[pl.BlockSpec((tm, K), lambda j, i: (i, 0)),
                  pl.BlockSpec((tn, K), lambda j, i: (j, 0))],
        out_specs=pl.BlockSpec((tm, tn), lambda j, i: (i, j)),
        out_shape=jax.ShapeDtypeStruct((M, n_cols), out_dtype),
        scratch_shapes=[pltpu.VMEM((tn, K), BF16)],
        compiler_params=_cparams(("parallel", "arbitrary")), name=name)(a, w_t)


def _inproj_small_kernel(h_ref, wq_ref, wkv_ref, ws_ref, gq_ref, gkv_ref,
                         cq_ref, ckv_ref, sm_ref):
    h = h_ref[...]
    cq = jnp.dot(h, wq_ref[...], preferred_element_type=F32)
    cq_ref[...] = _rms(cq, gq_ref[...]).astype(cq_ref.dtype)
    ckv = jnp.dot(h, wkv_ref[...], preferred_element_type=F32)
    ckv_ref[...] = _rms(ckv, gkv_ref[...]).astype(ckv_ref.dtype)
    sm_ref[...] = jnp.dot(h, ws_ref[...], preferred_element_type=F32)


def _inproj_small(h, w_cq, w_ckv, w_sm, g_cq, g_ckv, tm=512):
    M, D = h.shape
    tm = min(tm, M)
    nq, nkv, ns = w_cq.shape[1], w_ckv.shape[1], w_sm.shape[1]
    full = lambda n: pl.BlockSpec((D, n), lambda i: (0, 0))
    row = lambda n: pl.BlockSpec((tm, n), lambda i: (i, 0))
    return pl.pallas_call(
        _inproj_small_kernel, grid=(M // tm,),
        in_specs=[row(D), full(nq), full(nkv), full(ns),
                  pl.BlockSpec((1, nq), lambda i: (0, 0)),
                  pl.BlockSpec((1, nkv), lambda i: (0, 0))],
        out_specs=[row(nq), row(nkv), row(ns)],
        out_shape=[jax.ShapeDtypeStruct((M, nq), BF16),
                   jax.ShapeDtypeStruct((M, nkv), BF16),
                   jax.ShapeDtypeStruct((M, ns), F32)],
        compiler_params=_cparams(("parallel",)), name="inproj_small")(
            h, w_cq, w_ckv, w_sm, g_cq.reshape(1, nq), g_ckv.reshape(1, nkv))


def _mlstm_kernel(*refs, reverse, finish, L, goff):
    if finish:
        (q_ref, k_ref, v_ref, gc_ref, gr_ref, hb_ref, o_ref, gm_ref,
         out_ref, C_ref, n_ref, m_ref) = refs
    else:
        q_ref, k_ref, v_ref, gc_ref, gr_ref, out_ref, C_ref, n_ref, m_ref = refs
    H, dk, dv = M_HEADS, M_QK_HD, M_V_HD
    log_scale = -0.5 * math.log(dk)

    @pl.when(pl.program_id(1) == 0)
    def _():
        C_ref[...] = jnp.zeros_like(C_ref)
        n_ref[...] = jnp.zeros_like(n_ref)
        m_ref[...] = jnp.zeros_like(m_ref)

    t_idx = lax.broadcasted_iota(jnp.int32, (L, L), 0)
    s_idx = lax.broadcasted_iota(jnp.int32, (L, L), 1)
    if reverse:
        mask, mask_t = s_idx >= t_idx, t_idx >= s_idx
    else:
        mask, mask_t = s_idx <= t_idx, t_idx <= s_idx

    for h in range(H):
        i_col = gc_ref[:, goff + h:goff + h + 1]
        f_col = gc_ref[:, goff + H + h:goff + H + h + 1]
        i_row = gr_ref[goff + h:goff + h + 1, :]
        f_row = gr_ref[goff + H + h:goff + H + h + 1, :]
        lf_col = _log_sigmoid(f_col)
        lf_row = _log_sigmoid(f_row)
        b_col = jnp.sum(jnp.where(mask, lf_row, 0.0), axis=1, keepdims=True)
        b_row = jnp.sum(jnp.where(mask_t, lf_col, 0.0), axis=0, keepdims=True)
        g = jnp.sum(lf_row, axis=1, keepdims=True)
        m_prev = m_ref[h][:, 0:1]
        dmat = jnp.where(mask, b_col - b_row + i_row, NEG)
        a = b_col + m_prev
        m_t = jnp.maximum(a, jnp.max(dmat, axis=1, keepdims=True))

        q = q_ref[:, h * dk:(h + 1) * dk]
        k = k_ref[:, h * dk:(h + 1) * dk]
        v = v_ref[:, h * dv:(h + 1) * dv]
        qk = lax.dot_general(q, k, (((1,), (1,)), ((), ())),
                             preferred_element_type=F32)
        m_s = m_t - log_scale
        w_intra = jnp.exp(dmat - m_s) * qk
        w_inter = jnp.exp(a - m_s)
        c_state = C_ref[h]
        n_state = n_ref[h]
        q_c = jnp.dot(q, c_state.astype(BF16), preferred_element_type=F32)
        num = jnp.dot(w_intra.astype(BF16), v, preferred_element_type=F32) + w_inter * q_c
        q_n = jnp.sum(q.astype(F32) * n_state, axis=1, keepdims=True)
        den = jnp.sum(w_intra, axis=1, keepdims=True) + w_inter * q_n
        hh = num / jnp.maximum(jnp.abs(den), jnp.exp(-m_t))

        w_col = g - b_col + i_col
        m_new = jnp.maximum(g + m_prev, jnp.max(w_col, axis=0, keepdims=True))
        decay = jnp.exp(g + m_prev - m_new)
        kw = k.astype(F32) * jnp.exp(w_col - m_new)
        C_ref[h] = decay * c_state + lax.dot_general(
            kw.astype(BF16), v, (((0,), (0,)), ((), ())), preferred_element_type=F32)
        n_ref[h] = decay * n_state + jnp.sum(kw, axis=0, keepdims=True)
        m_ref[h] = jnp.broadcast_to(m_new, (1, LANES))

        if finish:
            tot = hh + hb_ref[:, h * dv:(h + 1) * dv]
            y = _rms(tot, gm_ref[h:h + 1, :])
            gate = _sigmoid(o_ref[:, h * dv:(h + 1) * dv].astype(F32))
            out_ref[:, h * dv:(h + 1) * dv] = (y * gate).astype(out_ref.dtype)
        else:
            out_ref[:, h * dv:(h + 1) * dv] = hh


def _mlstm(z_a, g_col, g_row, B, S, *, reverse, hb=None, g_mhead=None):
    H, dk, dv = M_HEADS, M_QK_HD, M_V_HD
    L = min(M_CHUNK, S)
    nc = S // L
    T = B * S
    finish = not reverse
    if reverse:
        cidx = lambda b, c: b * nc + (nc - 1 - c)
        ridx = lambda b, c: nc - 1 - c
    else:
        cidx = lambda b, c: b * nc + c
        ridx = lambda b, c: c
    mqk, mv = H * dk, H * dv
    in_specs = [
        pl.BlockSpec((L, mqk), lambda b, c: (cidx(b, c), 0)),
        pl.BlockSpec((L, mqk), lambda b, c: (cidx(b, c), 1)),
        pl.BlockSpec((L, mv), lambda b, c: (cidx(b, c), 1)),
        pl.BlockSpec((L, 4 * H), lambda b, c: (cidx(b, c), 0)),
        pl.BlockSpec((None, 4 * H, L), lambda b, c: (b, 0, ridx(b, c))),
    ]
    args = [z_a, z_a, z_a, g_col, g_row]
    if finish:
        in_specs += [
            pl.BlockSpec((L, mv), lambda b, c: (cidx(b, c), 0)),
            pl.BlockSpec((L, mv), lambda b, c: (cidx(b, c), 2)),
            pl.BlockSpec((H, dv), lambda b, c: (0, 0)),
        ]
        args += [hb, z_a, g_mhead.reshape(H, dv)]
    kern = functools.partial(_mlstm_kernel, reverse=reverse, finish=finish, L=L,
                             goff=2 * H if reverse else 0)
    return pl.pallas_call(
        kern, grid=(B, nc), in_specs=in_specs,
        out_specs=pl.BlockSpec((L, mv), lambda b, c: (cidx(b, c), 0)),
        out_shape=jax.ShapeDtypeStruct((T, mv), BF16 if finish else F32),
        scratch_shapes=[pltpu.VMEM((H, dk, dv), F32), pltpu.VMEM((H, 1, dk), F32),
                        pltpu.VMEM((H, 1, LANES), F32)],
        compiler_params=_cparams(("parallel", "arbitrary")),
        name="mlstm_bwd" if reverse else "mlstm_fwd")(*args)


def _rope_table_kernel(pos_ref, inv_ref, cos_ref, sin_ref):
    ang = pos_ref[...].astype(F32) * inv_ref[...]
    lane = lax.broadcasted_iota(jnp.int32, ang.shape, 1)
    cos_ref[...] = jnp.cos(ang)
    sin_ref[...] = jnp.where(lane < A_ROPE // 2, -jnp.sin(ang), jnp.sin(ang))


def _rope_tables(pos_col, inv_row, tm=1024):
    T = pos_col.shape[0]
    tm = min(tm, T)
    return pl.pallas_call(
        _rope_table_kernel, grid=(T // tm,),
        in_specs=[pl.BlockSpec((tm, 1), lambda i: (i, 0)),
                  pl.BlockSpec((1, LANES), lambda i: (0, 0))],
        out_specs=[pl.BlockSpec((tm, LANES), lambda i: (i, 0))] * 2,
        out_shape=[jax.ShapeDtypeStruct((T, LANES), F32)] * 2,
        compiler_params=_cparams(("parallel",)), name="rope_tables")(pos_col, inv_row)


def _rope(r, cos_ref, sin_ref):
    lane = lax.broadcasted_iota(jnp.int32, r.shape, 1)
    half = A_ROPE // 2
    swapped = jnp.where(lane < half, pltpu.roll(r, LANES - half, 1), pltpu.roll(r, half, 1))
    return r * cos_ref[...] + swapped * sin_ref[...]


PREP_HEADS = 8


def _q_prep_kernel(c_ref, w_ref, g_ref, cos_ref, sin_ref, o_ref):
    c = c_ref[...]
    for h in range(w_ref.shape[0]):
        acc = jnp.dot(c, w_ref[h], preferred_element_type=F32)
        ss = jnp.sum(acc * acc, axis=1, keepdims=True)
        y = acc * lax.rsqrt(ss / A_QK_HD + EPS) * g_ref[...] * (A_QK_HD ** -0.5 * LOG2E)
        o_ref[h, :, 0:A_NOPE] = y[:, 0:A_NOPE].astype(o_ref.dtype)
        o_ref[h, :, A_NOPE:] = _rope(y[:, A_NOPE:], cos_ref, sin_ref).astype(o_ref.dtype)


def _q_prep(cq_n, w_uq_p, g_qn_p, cos_t, sin_t, tm=1024):
    T, R = cq_n.shape
    tm = min(tm, T)
    H, hb = A_HEADS, PREP_HEADS
    return pl.pallas_call(
        _q_prep_kernel, grid=(T // tm, H // hb),
        in_specs=[pl.BlockSpec((tm, R), lambda i, h: (i, 0)),
                  pl.BlockSpec((hb, R, A_QK_PAD), lambda i, h: (h, 0, 0)),
                  pl.BlockSpec((1, A_QK_PAD), lambda i, h: (0, 0)),
                  pl.BlockSpec((tm, LANES), lambda i, h: (i, 0)),
                  pl.BlockSpec((tm, LANES), lambda i, h: (i, 0))],
        out_specs=pl.BlockSpec((hb, tm, A_QK_PAD), lambda i, h: (h, i, 0)),
        out_shape=jax.ShapeDtypeStruct((H, T, A_QK_PAD), BF16),
        compiler_params=_cparams(("parallel", "parallel")), name="mla_q_prep")(
            cq_n, w_uq_p, g_qn_p, cos_t, sin_t)


def _kv_prep_kernel(c_ref, w_ref, sm_ref, g_ref, cos_ref, sin_ref, kt_ref, v_ref):
    c = c_ref[...]
    lane = lax.broadcasted_iota(jnp.int32, sm_ref.shape, 1)
    k_pe = jnp.where(lane < A_ROPE, sm_ref[...], 0.0)
    ss_pe = jnp.sum(k_pe * k_pe, axis=1, keepdims=True)
    for h in range(w_ref.shape[0]):
        acc = jnp.dot(c, w_ref[h], preferred_element_type=F32)
        k_nope = acc[:, 0:A_NOPE]
        ss = jnp.sum(k_nope * k_nope, axis=1, keepdims=True) + ss_pe
        r = lax.rsqrt(ss / A_QK_HD + EPS)
        kn = k_nope * r * g_ref[:, 0:A_NOPE]
        kr = _rope(k_pe * r * g_ref[:, A_NOPE:], cos_ref, sin_ref)
        kt_ref[h, 0:A_NOPE, :] = kn.T.astype(kt_ref.dtype)
        kt_ref[h, A_NOPE:, :] = kr.T.astype(kt_ref.dtype)
        v_ref[h, :, 0:A_V_HD] = acc[:, A_NOPE:].astype(v_ref.dtype)
        v_ref[h, :, A_V_HD:] = jnp.ones((acc.shape[0], A_V_HD), v_ref.dtype)


def _kv_prep(ckv_n, w_ukv_h, small, g_kn_p, cos_t, sin_t, tm=1024):
    T, R = ckv_n.shape
    tm = min(tm, T)
    H, hb = A_HEADS, PREP_HEADS
    return pl.pallas_call(
        _kv_prep_kernel, grid=(T // tm, H // hb),
        in_specs=[pl.BlockSpec((tm, R), lambda i, h: (i, 0)),
                  pl.BlockSpec((hb, R, A_NOPE + A_V_HD), lambda i, h: (h, 0, 0)),
                  pl.BlockSpec((tm, LANES), lambda i, h: (i, 0)),
                  pl.BlockSpec((1, A_QK_PAD), lambda i, h: (0, 0)),
                  pl.BlockSpec((tm, LANES), lambda i, h: (i, 0)),
                  pl.BlockSpec((tm, LANES), lambda i, h: (i, 0))],
        out_specs=[pl.BlockSpec((hb, A_QK_PAD, tm), lambda i, h: (h, 0, i)),
                   pl.BlockSpec((hb, tm, 2 * A_V_HD), lambda i, h: (h, i, 0))],
        out_shape=[jax.ShapeDtypeStruct((H, A_QK_PAD, T), BF16),
                   jax.ShapeDtypeStruct((H, T, 2 * A_V_HD), BF16)],
        compiler_params=_cparams(("parallel", "parallel")), name="mla_kv_prep")(
            ckv_n, w_ukv_h, small, g_kn_p, cos_t, sin_t)


def _attn_kernel(q_ref, kt_ref, v_ref, o_ref, *, tk):
    q = q_ref[...]
    S = kt_ref.shape[1]
    m = acc = None
    for j in range(S // tk):
        s = jnp.dot(q, kt_ref[:, j * tk:(j + 1) * tk], preferred_element_type=F32)
        mj = jnp.max(s, axis=1, keepdims=True)
        vj = v_ref[j * tk:(j + 1) * tk, :]
        if j == 0:
            m = mj
            acc = jnp.dot(jnp.exp2(s - m).astype(BF16), vj, preferred_element_type=F32)
        else:
            m_new = jnp.maximum(m, mj)
            acc = (jnp.exp2(m - m_new) * acc
                   + jnp.dot(jnp.exp2(s - m_new).astype(BF16), vj, preferred_element_type=F32))
            m = m_new
    o_ref[...] = (acc[:, 0:A_V_HD] / acc[:, A_V_HD:A_V_HD + 1]).astype(o_ref.dtype)


def _attention(q, kt, v, B, S):
    H, T, dq = q.shape
    tq, tk = min(ATT_TQ, S), min(ATT_TK, S)
    nq = S // tq
    return pl.pallas_call(
        functools.partial(_attn_kernel, tk=tk), grid=(B, H, nq),
        in_specs=[pl.BlockSpec((None, tq, dq), lambda b, h, i: (h, b * nq + i, 0)),
                  pl.BlockSpec((None, dq, S), lambda b, h, i: (h, 0, b)),
                  pl.BlockSpec((None, S, 2 * A_V_HD), lambda b, h, i: (h, b, 0))],
        out_specs=pl.BlockSpec((tq, A_V_HD), lambda b, h, i: (b * nq + i, h)),
        out_shape=jax.ShapeDtypeStruct((T, H * A_V_HD), BF16),
        compiler_params=_cparams(("parallel", "parallel", "parallel")), name="mla_attention")(
            q, kt, v)


def _merge_kernel(hm_ref, ha_ref, wm_ref, wa_ref, bm_ref, ba_ref, o_ref, wm_bf, wa_bf):
    _cache_bf16(wm_ref, wm_bf)
    _cache_bf16(wa_ref, wa_bf)
    ym = jnp.dot(hm_ref[...], wm_bf[...], preferred_element_type=F32)
    ya = jnp.dot(ha_ref[...], wa_bf[...], preferred_element_type=F32)
    o_ref[...] = (_sigmoid(bm_ref[...].astype(F32)) * ym
                  + _sigmoid(ba_ref[...].astype(F32)) * ya).astype(o_ref.dtype)


def _merge(hm, ha, w_bm, w_ba, z_a, br_off, tm=1024, tn=512):
    T, K = hm.shape
    D = w_bm.shape[1]
    tm, tn = min(tm, T), min(tn, D)
    ob = br_off // tn
    return pl.pallas_call(
        _merge_kernel, grid=(D // tn, T // tm),
        in_specs=[pl.BlockSpec((tm, K), lambda j, i: (i, 0)),
                  pl.BlockSpec((tm, K), lambda j, i: (i, 0)),
                  pl.BlockSpec((K, tn), lambda j, i: (0, j)),
                  pl.BlockSpec((K, tn), lambda j, i: (0, j)),
                  pl.BlockSpec((tm, tn), lambda j, i: (i, ob + j)),
                  pl.BlockSpec((tm, tn), lambda j, i: (i, ob + D // tn + j))],
        out_specs=pl.BlockSpec((tm, tn), lambda j, i: (i, j)),
        out_shape=jax.ShapeDtypeStruct((T, D), BF16),
        scratch_shapes=[pltpu.VMEM((K, tn), BF16), pltpu.VMEM((K, tn), BF16)],
        compiler_params=_cparams(("parallel", "arbitrary")), name="merge")(
            hm, ha, w_bm, w_ba, z_a, z_a)


def _mm_res_kernel(a_ref, w_ref, r_ref, o_ref, wbf_ref):
    _cache_bf16(w_ref, wbf_ref)
    o_ref[...] = r_ref[...] + jnp.dot(a_ref[...], wbf_ref[...], preferred_element_type=F32)


def _matmul_residual(a, b, res, tm=1024, tn=512):
    M, K = a.shape
    N = b.shape[1]
    tm, tn = min(tm, M), min(tn, N)
    return pl.pallas_call(
        _mm_res_kernel, grid=(N // tn, M // tm),
        in_specs=[pl.BlockSpec((tm, K), lambda j, i: (i, 0)),
                  pl.BlockSpec((K, tn), lambda j, i: (0, j)),
                  pl.BlockSpec((tm, tn), lambda j, i: (i, j))],
        out_specs=pl.BlockSpec((tm, tn), lambda j, i: (i, j)),
        out_shape=jax.ShapeDtypeStruct((M, N), F32),
        scratch_shapes=[pltpu.VMEM((K, tn), BF16)],
        compiler_params=_cparams(("parallel", "arbitrary")), name="out_proj")(a, b, res)


def _router_kernel(x_ref, g_ref, w_ref, b_ref, o_ref, xn3_ref):
    xf = _rms(x_ref[...], g_ref[...])
    nch = xf.shape[1] // (2 * LANES)
    tm = xf.shape[0]
    pitch = xn3_ref.shape[0] // tm
    for c in range(pitch):
        xn3_ref[pl.ds(c, tm, stride=pitch), :] = (
            _pack_bf16_pair(xf[:, 2 * c * LANES:(2 * c + 1) * LANES],
                            xf[:, (2 * c + 1) * LANES:(2 * c + 2) * LANES])
            if c < nch else jnp.zeros((tm, LANES), jnp.int32))
    xn = xf.astype(BF16)
    logits = jnp.dot(xn, w_ref[...], preferred_element_type=F32) + b_ref[...]
    lane = lax.broadcasted_iota(jnp.int32, logits.shape, 1)
    big = jnp.int32(1 << 20)
    lg = jnp.where(lane < N_GROUPS, logits, NEG)
    gmax = jnp.max(lg, axis=1, keepdims=True)
    g_idx = jnp.min(jnp.where(lg == gmax, lane, big), axis=1, keepdims=True)
    g_w = 1.0 / jnp.sum(jnp.exp(lg - gmax), axis=1, keepdims=True)
    lo = N_GROUPS + g_idx * E_PER_GROUP
    le = jnp.where((lane >= lo) & (lane < lo + E_PER_GROUP), logits, NEG)
    v1 = jnp.max(le, axis=1, keepdims=True)
    i1 = jnp.min(jnp.where(le == v1, lane, big), axis=1, keepdims=True)
    le2 = jnp.where(lane == i1, NEG, le)
    v2 = jnp.max(le2, axis=1, keepdims=True)
    i2 = jnp.min(jnp.where(le2 == v2, lane, big), axis=1, keepdims=True)
    e21 = jnp.exp(v2 - v1)
    w1 = g_w / (1.0 + e21)
    w2 = g_w * e21 / (1.0 + e21)
    o_ref[...] = jnp.where(
        lane == 0, (i1 - N_GROUPS).astype(F32),
        jnp.where(lane == 1, (i2 - N_GROUPS).astype(F32),
                  jnp.where(lane == 2, w1, jnp.where(lane == 3, w2, 0.0))))


def _router(x, g, w_r, b_r, tm=512):
    T, D = x.shape
    tm = min(tm, T)
    pitch = _row_pitch(D // (2 * LANES))
    return pl.pallas_call(
        _router_kernel, grid=(T // tm,),
        in_specs=[pl.BlockSpec((tm, D), lambda i: (i, 0)),
                  pl.BlockSpec((1, D), lambda i: (0, 0)),
                  pl.BlockSpec((D, LANES), lambda i: (0, 0)),
                  pl.BlockSpec((1, LANES), lambda i: (0, 0))],
        out_specs=[pl.BlockSpec((tm, LANES), lambda i: (i, 0)),
                   pl.BlockSpec((tm * pitch, LANES), lambda i: (i, 0))],
        out_shape=[jax.ShapeDtypeStruct((T, LANES), F32),
                   jax.ShapeDtypeStruct((T * pitch, LANES), jnp.int32)],
        compiler_params=_cparams(("parallel",)), name="router")(x, g.reshape(1, D), w_r, b_r)


def _pack_bf16_pair(a, b):
    ua = lax.bitcast_convert_type(a.astype(BF16).astype(F32), jnp.int32)
    ub = lax.bitcast_convert_type(b.astype(BF16).astype(F32), jnp.int32)
    return lax.shift_right_logical(ua, jnp.int32(16)) | (ub & jnp.int32(-65536))


def _unpack_bf16_pair(u):
    return [lax.bitcast_convert_type(lax.shift_left(u, jnp.int32(16)), F32),
            lax.bitcast_convert_type(u & jnp.int32(-65536), F32)]


def _row_pitch(nch):
    return (nch + 7) // 8 * 8 + 8


def _gather_rows(start_one, wait_one, n, unroll=8):
    def start():
        def body(r, carry):
            start_one(r)
            return carry
        lax.fori_loop(0, n, body, 0, unroll=unroll)

    def wait():
        def body(r, carry):
            wait_one(r)
            return carry
        lax.fori_loop(0, n, body, 0, unroll=unroll)
    return start, wait


def _expert_up_kernel(blk_e_ref, nused_ref, src_ref, tok_ref, x_hbm, w1_ref, w3_ref,
                      h_ref, xbuf, sem, *, blk, n_assign, nch, pitch, kch):
    i = pl.program_id(0)
    nused = nused_ref[0]

    def row_copy(step, r, slot):
        idx = jnp.minimum(src_ref[step] + r, n_assign - 1)
        return pltpu.make_async_copy(
            x_hbm.at[pl.ds(pl.multiple_of(tok_ref[idx] * pitch, 8), nch), :],
            xbuf.at[slot, pl.ds(pl.multiple_of(r * pitch, 8), nch), :], sem.at[slot])

    def gather(step, slot):
        return _gather_rows(lambda r: row_copy(step, r, slot).start(),
                            lambda r: row_copy(step, r, slot).wait(), blk)

    @pl.when(i == 0)
    def _():
        gather(0, 0)[0]()

    @pl.when(i + 1 < nused)
    def _():
        gather(i + 1, (i + 1) % 2)[0]()

    @pl.when(i < nused)
    def _():
        slot = i % 2
        gather(i, slot)[1]()
        parts = []
        for c in range(nch):
            parts += _unpack_bf16_pair(xbuf[slot, pl.ds(c, blk, stride=pitch), :])
        xn = jnp.concatenate([t.astype(BF16) for t in parts], axis=1)
        h1 = h3 = None
        for k in range(xn.shape[1] // kch):
            xk = xn[:, k * kch:(k + 1) * kch]
            d1 = jnp.dot(xk, w1_ref[k * kch:(k + 1) * kch, :].astype(BF16),
                         preferred_element_type=F32)
            d3 = jnp.dot(xk, w3_ref[k * kch:(k + 1) * kch, :].astype(BF16),
                         preferred_element_type=F32)
            h1 = d1 if h1 is None else h1 + d1
            h3 = d3 if h3 is None else h3 + d3
        h_ref[...] = (h1 * _sigmoid(h1) * h3).astype(h_ref.dtype)

    @pl.when(i >= nused)
    def _():
        h_ref[...] = jnp.zeros_like(h_ref)


def _expert_down_kernel(blk_e_ref, nused_ref, h_ref, w2_ref, y_ref, *, blk, nch, pitch, ych):
    i = pl.program_id(0)
    nused = nused_ref[0]

    @pl.when(i < nused)
    def _():
        hdn = h_ref[...]
        for j in range(nch * 2 * LANES // ych):
            yj = jnp.dot(hdn, w2_ref[:, j * ych:(j + 1) * ych].astype(BF16),
                         preferred_element_type=F32)
            for c in range(ych // (2 * LANES)):
                y_ref[pl.ds(j * (ych // (2 * LANES)) + c, blk, stride=pitch), :] = (
                    _pack_bf16_pair(yj[:, 2 * c * LANES:(2 * c + 1) * LANES],
                                    yj[:, (2 * c + 1) * LANES:(2 * c + 2) * LANES]))
        for c in range(nch, pitch):
            y_ref[pl.ds(c, blk, stride=pitch), :] = jnp.zeros((blk, LANES), jnp.int32)

    @pl.when(i >= nused)
    def _():
        y_ref[...] = jnp.zeros_like(y_ref)


def _experts(xn3, w1, w3, w2, blk_e, nused, src_start, tok_sorted, blk):
    E, D, De = w1.shape
    nch = D // (2 * LANES)
    pitch = _row_pitch(nch)
    nb = blk_e.shape[0]
    n_assign = tok_sorted.shape[0]
    wmap4 = lambda i, be, nu, sr, tk: (be[i], 0, 0)
    up_spec = pltpu.PrefetchScalarGridSpec(
        num_scalar_prefetch=4, grid=(nb,),
        in_specs=[pl.BlockSpec(memory_space=pl.ANY),
                  pl.BlockSpec((None, D, De), wmap4),
                  pl.BlockSpec((None, D, De), wmap4)],
        out_specs=pl.BlockSpec((blk, De), lambda i, be, nu, sr, tk: (i, 0)),
        scratch_shapes=[pltpu.VMEM((2, blk * pitch, LANES), jnp.int32),
                        pltpu.SemaphoreType.DMA((2,))])
    hdn = pl.pallas_call(
        functools.partial(_expert_up_kernel, blk=blk, n_assign=n_assign, nch=nch, pitch=pitch,
                          kch=min(1024, D)),
        grid_spec=up_spec,
        out_shape=jax.ShapeDtypeStruct((nb * blk, De), BF16),
        compiler_params=_cparams(("arbitrary",)), name="experts_up")(
            blk_e, nused, src_start, tok_sorted, xn3, w1, w3)
    down_spec = pltpu.PrefetchScalarGridSpec(
        num_scalar_prefetch=2, grid=(nb,),
        in_specs=[pl.BlockSpec((blk, De), lambda i, be, nu: (i, 0)),
                  pl.BlockSpec((None, De, D), lambda i, be, nu: (be[i], 0, 0))],
        out_specs=pl.BlockSpec((blk * pitch, LANES), lambda i, be, nu: (i, 0)))
    return pl.pallas_call(
        functools.partial(_expert_down_kernel, blk=blk, nch=nch, pitch=pitch, ych=min(1024, D)),
        grid_spec=down_spec,
        out_shape=jax.ShapeDtypeStruct((nb * blk * pitch, LANES), jnp.int32),
        compiler_params=_cparams(("arbitrary",)), name="experts_down")(blk_e, nused, hdn, w2)


def _combine_kernel(pos_ref, y_hbm, x_ref, rt_ref, g_ref, x2_ref, xn_ref, ybuf, sem, *, tc, nch,
                    pitch):
    i = pl.program_id(0)
    n = pl.num_programs(0)

    def row_copy(step, r, slot):
        return pltpu.make_async_copy(
            y_hbm.at[pl.ds(pl.multiple_of(pos_ref[step * 2 * tc + r] * pitch, 8), nch), :],
            ybuf.at[slot, pl.ds(pl.multiple_of(r * pitch, 8), nch), :], sem.at[slot])

    def gather(step, slot):
        return _gather_rows(lambda r: row_copy(step, r, slot).start(),
                            lambda r: row_copy(step, r, slot).wait(), 2 * tc)

    @pl.when(i == 0)
    def _():
        gather(0, 0)[0]()

    @pl.when(i + 1 < n)
    def _():
        gather(i + 1, (i + 1) % 2)[0]()

    slot = i % 2
    gather(i, slot)[1]()
    y0, y1 = [], []
    for c in range(nch):
        y0 += _unpack_bf16_pair(ybuf[slot, pl.ds(c, tc, stride=pitch), :])
        y1 += _unpack_bf16_pair(ybuf[slot, pl.ds(tc * pitch + c, tc, stride=pitch), :])
    y0 = jnp.concatenate(y0, axis=1)
    y1 = jnp.concatenate(y1, axis=1)
    x2 = x_ref[...] + rt_ref[:, 2:3] * y0 + rt_ref[:, 3:4] * y1
    x2_ref[...] = x2
    xn_ref[...] = _rms(x2, g_ref[...]).astype(xn_ref.dtype)


def _combine(y, x, route, g, pos, tc=256):
    T, D = x.shape
    tc = min(tc, T)
    nch = D // (2 * LANES)
    pitch = _row_pitch(nch)
    grid_spec = pltpu.PrefetchScalarGridSpec(
        num_scalar_prefetch=1, grid=(T // tc,),
        in_specs=[pl.BlockSpec(memory_space=pl.ANY),
                  pl.BlockSpec((tc, D), lambda i, ps: (i, 0)),
                  pl.BlockSpec((tc, LANES), lambda i, ps: (i, 0)),
                  pl.BlockSpec((1, D), lambda i, ps: (0, 0))],
        out_specs=[pl.BlockSpec((tc, D), lambda i, ps: (i, 0)),
                   pl.BlockSpec((tc, D), lambda i, ps: (i, 0))],
        scratch_shapes=[pltpu.VMEM((2, 2 * tc * pitch, LANES), jnp.int32),
                        pltpu.SemaphoreType.DMA((2,))])
    pos_tiled = pos.reshape(T // tc, tc, 2).transpose(0, 2, 1).reshape(-1)
    return pl.pallas_call(
        functools.partial(_combine_kernel, tc=tc, nch=nch, pitch=pitch), grid_spec=grid_spec,
        out_shape=[jax.ShapeDtypeStruct((T, D), F32), jax.ShapeDtypeStruct((T, D), BF16)],
        compiler_params=_cparams(("arbitrary",)), name="combine")(
            pos_tiled, y, x, route, g.reshape(1, D))


def _ple_kernel(xn_ref, wg_ref, p_ref, wp_ref, x_ref, o_ref, wg_bf, wp_bf):
    _cache_bf16(wg_ref, wg_bf)
    _cache_bf16(wp_ref, wp_bf)
    gate = _sigmoid(jnp.dot(xn_ref[...], wg_bf[...], preferred_element_type=F32))
    emb = jnp.dot(p_ref[...].astype(BF16), wp_bf[...], preferred_element_type=F32)
    o_ref[...] = x_ref[...] + gate * emb


def _ple(xn, w_pg, p, w_ple, x, tm=1024, tn=512):
    T, D = xn.shape
    Pd = p.shape[1]
    tm, tn = min(tm, T), min(tn, D)
    return pl.pallas_call(
        _ple_kernel, grid=(D // tn, T // tm),
        in_specs=[pl.BlockSpec((tm, D), lambda j, i: (i, 0)),
                  pl.BlockSpec((D, tn), lambda j, i: (0, j)),
                  pl.BlockSpec((tm, Pd), lambda j, i: (i, 0)),
                  pl.BlockSpec((Pd, tn), lambda j, i: (0, j)),
                  pl.BlockSpec((tm, tn), lambda j, i: (i, j))],
        out_specs=pl.BlockSpec((tm, tn), lambda j, i: (i, j)),
        out_shape=jax.ShapeDtypeStruct((T, D), F32),
        scratch_shapes=[pltpu.VMEM((D, tn), BF16), pltpu.VMEM((Pd, tn), BF16)],
        compiler_params=_cparams(("parallel", "arbitrary")), name="ple")(xn, w_pg, p, w_ple, x)


def _dispatch_tables(route, blk):
    T = route.shape[0]
    A = 2 * T
    eid = route[:, 0:2].astype(jnp.int32).reshape(-1)
    iota = jnp.arange(A, dtype=jnp.int32)
    experts = jnp.arange(N_EXPERTS, dtype=jnp.int32)
    _, order = lax.sort((eid, iota), num_keys=1)
    _, inv = lax.sort((order, iota), num_keys=1)
    onehot = eid[:, None] == experts[None, :]
    counts = jnp.sum(onehot, axis=0, dtype=jnp.int32)
    starts = jnp.cumsum(counts) - counts
    nblk = (counts + blk - 1) // blk
    bends = jnp.cumsum(nblk)
    bstarts = bends - nblk
    nb = (A + N_EXPERTS * blk) // blk
    b = jnp.arange(nb, dtype=jnp.int32)
    blk_e = jnp.minimum(jnp.sum(b[:, None] >= bends[None, :], axis=1),
                        N_EXPERTS - 1).astype(jnp.int32)
    shift = starts - bstarts * blk
    src_start = b * blk + jnp.sum(
        jnp.where(blk_e[:, None] == experts[None, :], shift[None, :], 0), axis=1)
    nused = bends[-1].astype(jnp.int32).reshape(1)
    pos = inv - jnp.sum(jnp.where(onehot, shift[None, :], 0), axis=1)
    return blk_e, nused, src_start.astype(jnp.int32), order // 2, pos.reshape(T, 2)


def _layer(x2d, p2d, pos_col, B, S, g_mix, w_in, b_mgate, g_mhead, g_cq, w_uq, g_ckv, w_ukv,
           g_qn, g_kn, w_bm, w_ba, w_out, g_ffn, w_rg, b_rg, w_re, b_re, w_e1, w_e3, w_e2,
           g_ple, w_pg, w_ple):
    T, D = x2d.shape
    H = M_HEADS
    mqk, mv = H * M_QK_HD, H * M_V_HD
    o_gate = 2 * mqk + 2 * mv
    o_cq = o_gate + 4 * H
    o_ckv = o_cq + Q_LORA
    o_kpe = o_ckv + KV_LORA
    o_br = o_kpe + A_ROPE
    w_t = jnp.swapaxes(w_in, 0, 1)
    w_br_t, w_mid_t = lax.optimization_barrier((w_t[o_br:], w_t[o_gate:o_br]))
    w_cq = w_mid_t[o_cq - o_gate:o_ckv - o_gate].T.astype(BF16)
    w_ckv = w_mid_t[o_ckv - o_gate:o_kpe - o_gate].T.astype(BF16)
    w_sm = jnp.concatenate(
        [w_mid_t[o_kpe - o_gate:], w_mid_t[:4 * H],
         jnp.zeros((LANES - A_ROPE - 4 * H, D), F32)], axis=0).T.astype(BF16)

    h = _rmsnorm(x2d, g_mix)
    z_a = _matmul_t(h, w_t, o_gate, BF16, name="inproj_mlstm")
    z_br = _matmul_t(h, w_br_t, 2 * D, BF16, name="inproj_gates")
    cq_n, ckv_n, small = _inproj_small(h, w_cq, w_ckv, w_sm, g_cq, g_ckv)

    g_col = small[:, A_ROPE:A_ROPE + 4 * H] + b_mgate[None, :]
    g_row = g_col.reshape(B, S, 4 * H).transpose(0, 2, 1)
    hb = _mlstm(z_a, g_col, g_row, B, S, reverse=True)
    hm = _mlstm(z_a, g_col, g_row, B, S, reverse=False, hb=hb, g_mhead=g_mhead)

    pad_q = A_QK_PAD - A_QK_HD
    w_uq_p = jnp.pad(w_uq.reshape(Q_LORA, A_HEADS, A_QK_HD).transpose(1, 0, 2),
                     ((0, 0), (0, 0), (0, pad_q))).astype(BF16)
    w_ukv_h = w_ukv.reshape(KV_LORA, A_HEADS, A_NOPE + A_V_HD).transpose(1, 0, 2).astype(BF16)
    g_qn_p = jnp.pad(g_qn, (0, pad_q)).reshape(1, A_QK_PAD)
    g_kn_p = jnp.pad(g_kn, (0, pad_q)).reshape(1, A_QK_PAD)
    inv = ROPE_THETA ** (-jnp.arange(0, A_ROPE, 2, dtype=F32) / A_ROPE)
    inv_row = jnp.concatenate([inv, inv, jnp.zeros((LANES - A_ROPE,), F32)]).reshape(1, LANES)
    cos_t, sin_t = _rope_tables(pos_col, inv_row)
    q = _q_prep(cq_n, w_uq_p, g_qn_p, cos_t, sin_t)
    kt, v = _kv_prep(ckv_n, w_ukv_h, small, g_kn_p, cos_t, sin_t)
    ha = _attention(q, kt, v, B, S)

    mix = _merge(hm, ha, w_bm, w_ba, z_br, 0)
    x1 = _matmul_residual(mix, w_out, x2d)

    w_r = jnp.concatenate([w_rg, w_re, jnp.zeros((D, LANES - N_GROUPS - N_EXPERTS), F32)],
                          axis=1).astype(BF16)
    b_r = jnp.concatenate([b_rg, b_re, jnp.zeros((LANES - N_GROUPS - N_EXPERTS,), F32)]
                          ).reshape(1, LANES)
    route, xn3 = _router(x1, g_ffn, w_r, b_r)
    blk = min(MOE_BLK, T)
    blk_e, nused, src_start, tok_sorted, pos = _dispatch_tables(route, blk)
    y = _experts(xn3, w_e1, w_e3, w_e2, blk_e, nused, src_start, tok_sorted, blk)
    x2, xn2 = _combine(y, x1, route, g_ple, pos)

    return _ple(xn2, w_pg, p2d, w_ple, x2)


def kernel(x, p, positions, g_mix, w_in, b_mgate, g_mhead, g_cq, w_uq, g_ckv, w_ukv, g_qn, g_kn,
           w_bm, w_ba, w_out, g_ffn, w_rg, b_rg, w_re, b_re, w_e1, w_e3, w_e2, g_ple, w_pg, w_ple):
    B, S, D = x.shape
    T = B * S
    x2d = x.reshape(T, D)
    pos_col = positions.reshape(T, 1).astype(jnp.int32)
    for l in range(p.shape[0]):
        x2d = _layer(x2d, p[l].reshape(T, -1), pos_col, B, S, g_mix[l], w_in[l], b_mgate[l],
                     g_mhead[l], g_cq[l], w_uq[l], g_ckv[l], w_ukv[l], g_qn[l], g_kn[l],
                     w_bm[l], w_ba[l], w_out[l], g_ffn[l], w_rg[l], b_rg[l], w_re[l], b_re[l],
                     w_e1[l], w_e3[l], w_e2[l], g_ple[l], w_pg[l], w_ple[l])
    return x2d.reshape(B, S, D)
```

```python
import functools
import math

import jax
import jax.numpy as jnp
from jax import lax
from jax.experimental import pallas as pl
from jax.experimental.pallas import tpu as pltpu

F32 = jnp.float32
BF16 = jnp.bfloat16

EPS = 1e-6
M_HEADS = 8
M_QK_HD = 128
M_V_HD = 256
A_HEADS = 16
A_NOPE = 128
A_ROPE = 64
A_QK_HD = A_NOPE + A_ROPE
A_V_HD = 128
A_QK_PAD = 256
Q_LORA = 1024
KV_LORA = 512
ROPE_THETA = 10000.0
N_GROUPS = 8
E_PER_GROUP = 8
N_EXPERTS = N_GROUPS * E_PER_GROUP
D_EXPERT = 512
LANES = 128

M_CHUNK = 256
MOE_BLK = 256
ATT_TQ = 1024
ATT_TK = 256
VMEM_LIMIT = 56 * 1024 * 1024
NEG = -1e30
LOG2E = 1.4426950408889634


def _cparams(sem):
    return pltpu.CompilerParams(dimension_semantics=sem, vmem_limit_bytes=VMEM_LIMIT)


def _sigmoid(x):
    return 1.0 / (1.0 + jnp.exp(-x))


def _log_sigmoid(x):
    return jnp.minimum(x, 0.0) - jnp.log(1.0 + jnp.exp(-jnp.abs(x)))


def _rms(xf, g):
    return xf * lax.rsqrt(jnp.mean(xf * xf, axis=-1, keepdims=True) + EPS) * g


def _norm_kernel(x_ref, g_ref, o_ref):
    o_ref[...] = _rms(x_ref[...], g_ref[...]).astype(o_ref.dtype)


def _rmsnorm(x, g, tm=512):
    M, D = x.shape
    tm = min(tm, M)
    return pl.pallas_call(
        _norm_kernel, grid=(M // tm,),
        in_specs=[pl.BlockSpec((tm, D), lambda i: (i, 0)),
                  pl.BlockSpec((1, D), lambda i: (0, 0))],
        out_specs=pl.BlockSpec((tm, D), lambda i: (i, 0)),
        out_shape=jax.ShapeDtypeStruct((M, D), BF16),
        compiler_params=_cparams(("parallel",)), name="rmsnorm")(x, g.reshape(1, D))


def _cache_bf16(w_ref, wbf_ref):
    @pl.when(pl.program_id(1) == 0)
    def _():
        wbf_ref[...] = w_ref[...].astype(BF16)


def _mm_t_kernel(a_ref, wt_ref, o_ref, wbf_ref):
    @pl.when(pl.program_id(1) == 0)
    def _():
        wbf_ref[...] = wt_ref[...].astype(BF16)
    o_ref[...] = lax.dot_general(a_ref[...], wbf_ref[...], (((1,), (1,)), ((), ())),
                                 preferred_element_type=F32).astype(o_ref.dtype)


def _matmul_t(a, w_t, n_cols, out_dtype, tm=1024, tn=512, name="matmul"):
    M, K = a.shape
    tm, tn = min(tm, M), min(tn, n_cols)
    return pl.pallas_call(
        _mm_t_kernel, grid=(n_cols // tn, M // tm),
        in_specs=[pl.BlockSpec((tm, K), lambda j, i: (i, 0)),
                  pl.BlockSpec((tn, K), lambda j, i: (j, 0))],
        out_specs=pl.BlockSpec((tm, tn), lambda j, i: (i, j)),
        out_shape=jax.ShapeDtypeStruct((M, n_cols), out_dtype),
        scratch_shapes=[pltpu.VMEM((tn, K), BF16)],
        compiler_params=_cparams(("parallel", "arbitrary")), name=name)(a, w_t)


def _inproj_small_kernel(h_ref, wq_ref, wkv_ref, ws_ref, gq_ref, gkv_ref,
                         cq_ref, ckv_ref, sm_ref):
    h = h_ref[...]
    cq = jnp.dot(h, wq_ref[...], preferred_element_type=F32)
    cq_ref[...] = _rms(cq, gq_ref[...]).astype(cq_ref.dtype)
    ckv = jnp.dot(h, wkv_ref[...], preferred_element_type=F32)
    ckv_ref[...] = _rms(ckv, gkv_ref[...]).astype(ckv_ref.dtype)
    sm_ref[...] = jnp.dot(h, ws_ref[...], preferred_element_type=F32)


def _inproj_small(h, w_cq, w_ckv, w_sm, g_cq, g_ckv, tm=512):
    M, D = h.shape
    tm = min(tm, M)
    nq, nkv, ns = w_cq.shape[1], w_ckv.shape[1], w_sm.shape[1]
    full = lambda n: pl.BlockSpec((D, n), lambda i: (0, 0))
    row = lambda n: pl.BlockSpec((tm, n), lambda i: (i, 0))
    return pl.pallas_call(
        _inproj_small_kernel, grid=(M // tm,),
        in_specs=[row(D), full(nq), full(nkv), full(ns),
                  pl.BlockSpec((1, nq), lambda i: (0, 0)),
                  pl.BlockSpec((1, nkv), lambda i: (0, 0))],
        out_specs=[row(nq), row(nkv), row(ns)],
        out_shape=[jax.ShapeDtypeStruct((M, nq), BF16),
                   jax.ShapeDtypeStruct((M, nkv), BF16),
                   jax.ShapeDtypeStruct((M, ns), F32)],
        compiler_params=_cparams(("parallel",)), name="inproj_small")(
            h, w_cq, w_ckv, w_sm, g_cq.reshape(1, nq), g_ckv.reshape(1, nkv))


def _mlstm_kernel(*refs, reverse, finish, L, goff):
    if finish:
        (q_ref, k_ref, v_ref, gc_ref, gr_ref, hb_ref, o_ref, gm_ref,
         out_ref, C_ref, n_ref, m_ref) = refs
    else:
        q_ref, k_ref, v_ref, gc_ref, gr_ref, out_ref, C_ref, n_ref, m_ref = refs
    H, dk, dv = M_HEADS, M_QK_HD, M_V_HD
    log_scale = -0.5 * math.log(dk)

    @pl.when(pl.program_id(1) == 0)
    def _():
        C_ref[...] = jnp.zeros_like(C_ref)
        n_ref[...] = jnp.zeros_like(n_ref)
        m_ref[...] = jnp.zeros_like(m_ref)

    t_idx = lax.broadcasted_iota(jnp.int32, (L, L), 0)
    s_idx = lax.broadcasted_iota(jnp.int32, (L, L), 1)
    if reverse:
        mask, mask_t = s_idx >= t_idx, t_idx >= s_idx
    else:
        mask, mask_t = s_idx <= t_idx, t_idx <= s_idx

    for h in range(H):
        i_col = gc_ref[:, goff + h:goff + h + 1]
        f_col = gc_ref[:, goff + H + h:goff + H + h + 1]
        i_row = gr_ref[goff + h:goff + h + 1, :]
        f_row = gr_ref[goff + H + h:goff + H + h + 1, :]
        lf_col = _log_sigmoid(f_col)
        lf_row = _log_sigmoid(f_row)
        b_col = jnp.sum(jnp.where(mask, lf_row, 0.0), axis=1, keepdims=True)
        b_row = jnp.sum(jnp.where(mask_t, lf_col, 0.0), axis=0, keepdims=True)
        g = jnp.sum(lf_row, axis=1, keepdims=True)
        m_prev = m_ref[h][:, 0:1]
        dmat = jnp.where(mask, b_col - b_row + i_row, NEG)
        a = b_col + m_prev
        m_t = jnp.maximum(a, jnp.max(dmat, axis=1, keepdims=True))

        q = q_ref[:, h * dk:(h + 1) * dk]
        k = k_ref[:, h * dk:(h + 1) * dk]
        v = v_ref[:, h * dv:(h + 1) * dv]
        qk = lax.dot_general(q, k, (((1,), (1,)), ((), ())),
                             preferred_element_type=F32)
        m_s = m_t - log_scale
        w_intra = jnp.exp(dmat - m_s) * qk
        w_inter = jnp.exp(a - m_s)
        c_state = C_ref[h]
        n_state = n_ref[h]
        q_c = jnp.dot(q, c_state.astype(BF16), preferred_element_type=F32)
        num = jnp.dot(w_intra.astype(BF16), v, preferred_element_type=F32) + w_inter * q_c
        q_n = jnp.sum(q.astype(F32) * n_state, axis=1, keepdims=True)
        den = jnp.sum(w_intra, axis=1, keepdims=True) + w_inter * q_n
        hh = num / jnp.maximum(jnp.abs(den), jnp.exp(-m_t))

        w_col = g - b_col + i_col
        m_new = jnp.maximum(g + m_prev, jnp.max(w_col, axis=0, keepdims=True))
        decay = jnp.exp(g + m_prev - m_new)
        kw = k.astype(F32) * jnp.exp(w_col - m_new)
        C_ref[h] = decay * c_state + lax.dot_general(
            kw.astype(BF16), v, (((0,), (0,)), ((), ())), preferred_element_type=F32)
        n_ref[h] = decay * n_state + jnp.sum(kw, axis=0, keepdims=True)
        m_ref[h] = jnp.broadcast_to(m_new, (1, LANES))

        if finish:
            tot = hh + hb_ref[:, h * dv:(h + 1) * dv]
            y = _rms(tot, gm_ref[h:h + 1, :])
            gate = _sigmoid(o_ref[:, h * dv:(h + 1) * dv].astype(F32))
            out_ref[:, h * dv:(h + 1) * dv] = (y * gate).astype(out_ref.dtype)
        else:
            out_ref[:, h * dv:(h + 1) * dv] = hh


def _mlstm(z_a, g_col, g_row, B, S, *, reverse, hb=None, g_mhead=None):
    H, dk, dv = M_HEADS, M_QK_HD, M_V_HD
    L = min(M_CHUNK, S)
    nc = S // L
    T = B * S
    finish = not reverse
    if reverse:
        cidx = lambda b, c: b * nc + (nc - 1 - c)
        ridx = lambda b, c: nc - 1 - c
    else:
        cidx = lambda b, c: b * nc + c
        ridx = lambda b, c: c
    mqk, mv = H * dk, H * dv
    in_specs = [
        pl.BlockSpec((L, mqk), lambda b, c: (cidx(b, c), 0)),
        pl.BlockSpec((L, mqk), lambda b, c: (cidx(b, c), 1)),
        pl.BlockSpec((L, mv), lambda b, c: (cidx(b, c), 1)),
        pl.BlockSpec((L, 4 * H), lambda b, c: (cidx(b, c), 0)),
        pl.BlockSpec((None, 4 * H, L), lambda b, c: (b, 0, ridx(b, c))),
    ]
    args = [z_a, z_a, z_a, g_col, g_row]
    if finish:
        in_specs += [
            pl.BlockSpec((L, mv), lambda b, c: (cidx(b, c), 0)),
            pl.BlockSpec((L, mv), lambda b, c: (cidx(b, c), 2)),
            pl.BlockSpec((H, dv), lambda b, c: (0, 0)),
        ]
        args += [hb, z_a, g_mhead.reshape(H, dv)]
    kern = functools.partial(_mlstm_kernel, reverse=reverse, finish=finish, L=L,
                             goff=2 * H if reverse else 0)
    return pl.pallas_call(
        kern, grid=(B, nc), in_specs=in_specs,
        out_specs=pl.BlockSpec((L, mv), lambda b, c: (cidx(b, c), 0)),
        out_shape=jax.ShapeDtypeStruct((T, mv), BF16 if finish else F32),
        scratch_shapes=[pltpu.VMEM((H, dk, dv), F32), pltpu.VMEM((H, 1, dk), F32),
                        pltpu.VMEM((H, 1, LANES), F32)],
        compiler_params=_cparams(("parallel", "arbitrary")),
        name="mlstm_bwd" if reverse else "mlstm_fwd")(*args)


def _rope_table_kernel(pos_ref, inv_ref, cos_ref, sin_ref):
    ang = pos_ref[...].astype(F32) * inv_ref[...]
    lane = lax.broadcasted_iota(jnp.int32, ang.shape, 1)
    cos_ref[...] = jnp.cos(ang)
    sin_ref[...] = jnp.where(lane < A_ROPE // 2, -jnp.sin(ang), jnp.sin(ang))


def _rope_tables(pos_col, inv_row, tm=1024):
    T = pos_col.shape[0]
    tm = min(tm, T)
    return pl.pallas_call(
        _rope_table_kernel, grid=(T // tm,),
        in_specs=[pl.BlockSpec((tm, 1), lambda i: (i, 0)),
                  pl.BlockSpec((1, LANES), lambda i: (0, 0))],
        out_specs=[pl.BlockSpec((tm, LANES), lambda i: (i, 0))] * 2,
        out_shape=[jax.ShapeDtypeStruct((T, LANES), F32)] * 2,
        compiler_params=_cparams(("parallel",)), name="rope_tables")(pos_col, inv_row)


def _rope(r, cos_ref, sin_ref):
    lane = lax.broadcasted_iota(jnp.int32, r.shape, 1)
    half = A_ROPE // 2
    swapped = jnp.where(lane < half, pltpu.roll(r, LANES - half, 1), pltpu.roll(r, half, 1))
    return r * cos_ref[...] + swapped * sin_ref[...]


PREP_HEADS = 8


def _q_prep_kernel(c_ref, w_ref, g_ref, cos_ref, sin_ref, o_ref):
    c = c_ref[...]
    for h in range(w_ref.shape[0]):
        acc = jnp.dot(c, w_ref[h], preferred_element_type=F32)
        ss = jnp.sum(acc * acc, axis=1, keepdims=True)
        y = acc * lax.rsqrt(ss / A_QK_HD + EPS) * g_ref[...] * (A_QK_HD ** -0.5 * LOG2E)
        o_ref[h, :, 0:A_NOPE] = y[:, 0:A_NOPE].astype(o_ref.dtype)
        o_ref[h, :, A_NOPE:] = _rope(y[:, A_NOPE:], cos_ref, sin_ref).astype(o_ref.dtype)


def _q_prep(cq_n, w_uq_p, g_qn_p, cos_t, sin_t, tm=1024):
    T, R = cq_n.shape
    tm = min(tm, T)
    H, hb = A_HEADS, PREP_HEADS
    return pl.pallas_call(
        _q_prep_kernel, grid=(T // tm, H // hb),
        in_specs=[pl.BlockSpec((tm, R), lambda i, h: (i, 0)),
                  pl.BlockSpec((hb, R, A_QK_PAD), lambda i, h: (h, 0, 0)),
                  pl.BlockSpec((1, A_QK_PAD), lambda i, h: (0, 0)),
                  pl.BlockSpec((tm, LANES), lambda i, h: (i, 0)),
                  pl.BlockSpec((tm, LANES), lambda i, h: (i, 0))],
        out_specs=pl.BlockSpec((hb, tm, A_QK_PAD), lambda i, h: (h, i, 0)),
        out_shape=jax.ShapeDtypeStruct((H, T, A_QK_PAD), BF16),
        compiler_params=_cparams(("parallel", "parallel")), name="mla_q_prep")(
            cq_n, w_uq_p, g_qn_p, cos_t, sin_t)


def _kv_prep_kernel(c_ref, w_ref, sm_ref, g_ref, cos_ref, sin_ref, kt_ref, v_ref):
    c = c_ref[...]
    lane = lax.broadcasted_iota(jnp.int32, sm_ref.shape, 1)
    k_pe = jnp.where(lane < A_ROPE, sm_ref[...], 0.0)
    ss_pe = jnp.sum(k_pe * k_pe, axis=1, keepdims=True)
    for h in range(w_ref.shape[0]):
        acc = jnp.dot(c, w_ref[h], preferred_element_type=F32)
        k_nope = acc[:, 0:A_NOPE]
        ss = jnp.sum(k_nope * k_nope, axis=1, keepdims=True) + ss_pe
        r = lax.rsqrt(ss / A_QK_HD + EPS)
        kn = k_nope * r * g_ref[:, 0:A_NOPE]
        kr = _rope(k_pe * r * g_ref[:, A_NOPE:], cos_ref, sin_ref)
        kt_ref[h, 0:A_NOPE, :] = kn.T.astype(kt_ref.dtype)
        kt_ref[h, A_NOPE:, :] = kr.T.astype(kt_ref.dtype)
        v_ref[h, :, 0:A_V_HD] = acc[:, A_NOPE:].astype(v_ref.dtype)
        v_ref[h, :, A_V_HD:] = jnp.ones((acc.shape[0], A_V_HD), v_ref.dtype)


def _kv_prep(ckv_n, w_ukv_h, small, g_kn_p, cos_t, sin_t, tm=1024):
    T, R = ckv_n.shape
    tm = min(tm, T)
    H, hb = A_HEADS, PREP_HEADS
    return pl.pallas_call(
        _kv_prep_kernel, grid=(T // tm, H // hb),
        in_specs=[pl.BlockSpec((tm, R), lambda i, h: (i, 0)),
                  pl.BlockSpec((hb, R, A_NOPE + A_V_HD), lambda i, h: (h, 0, 0)),
                  pl.BlockSpec((tm, LANES), lambda i, h: (i, 0)),
                  pl.BlockSpec((1, A_QK_PAD), lambda i, h: (0, 0)),
                  pl.BlockSpec((tm, LANES), lambda i, h: (i, 0)),
                  pl.BlockSpec((tm, LANES), lambda i, h: (i, 0))],
        out_specs=[pl.BlockSpec((hb, A_QK_PAD, tm), lambda i, h: (h, 0, i)),
                   pl.BlockSpec((hb, tm, 2 * A_V_HD), lambda i, h: (h, i, 0))],
        out_shape=[jax.ShapeDtypeStruct((H, A_QK_PAD, T), BF16),
                   jax.ShapeDtypeStruct((H, T, 2 * A_V_HD), BF16)],
        compiler_params=_cparams(("parallel", "parallel")), name="mla_kv_prep")(
            ckv_n, w_ukv_h, small, g_kn_p, cos_t, sin_t)


def _attn_kernel(q_ref, kt_ref, v_ref, o_ref, *, tk):
    q = q_ref[...]
    S = kt_ref.shape[1]
    m = acc = None
    for j in range(S // tk):
        s = jnp.dot(q, kt_ref[:, j * tk:(j + 1) * tk], preferred_element_type=F32)
        mj = jnp.max(s, axis=1, keepdims=True)
        vj = v_ref[j * tk:(j + 1) * tk, :]
        if j == 0:
            m = mj
            acc = jnp.dot(jnp.exp2(s - m).astype(BF16), vj, preferred_element_type=F32)
        else:
            m_new = jnp.maximum(m, mj)
            acc = (jnp.exp2(m - m_new) * acc
                   + jnp.dot(jnp.exp2(s - m_new).astype(BF16), vj, preferred_element_type=F32))
            m = m_new
    o_ref[...] = (acc[:, 0:A_V_HD] / acc[:, A_V_HD:A_V_HD + 1]).astype(o_ref.dtype)


def _attention(q, kt, v, B, S):
    H, T, dq = q.shape
    tq, tk = min(ATT_TQ, S), min(ATT_TK, S)
    nq = S // tq
    return pl.pallas_call(
        functools.partial(_attn_kernel, tk=tk), grid=(B, H, nq),
        in_specs=[pl.BlockSpec((None, tq, dq), lambda b, h, i: (h, b * nq + i, 0)),
                  pl.BlockSpec((None, dq, S), lambda b, h, i: (h, 0, b)),
                  pl.BlockSpec((None, S, 2 * A_V_HD), lambda b, h, i: (h, b, 0))],
        out_specs=pl.BlockSpec((tq, A_V_HD), lambda b, h, i: (b * nq + i, h)),
        out_shape=jax.ShapeDtypeStruct((T, H * A_V_HD), BF16),
        compiler_params=_cparams(("parallel", "parallel", "parallel")), name="mla_attention")(
            q, kt, v)


def _merge_kernel(hm_ref, ha_ref, wm_ref, wa_ref, bm_ref, ba_ref, o_ref, wm_bf, wa_bf):
    _cache_bf16(wm_ref, wm_bf)
    _cache_bf16(wa_ref, wa_bf)
    ym = jnp.dot(hm_ref[...], wm_bf[...], preferred_element_type=F32)
    ya = jnp.dot(ha_ref[...], wa_bf[...], preferred_element_type=F32)
    o_ref[...] = (_sigmoid(bm_ref[...].astype(F32)) * ym
                  + _sigmoid(ba_ref[...].astype(F32)) * ya).astype(o_ref.dtype)


def _merge(hm, ha, w_bm, w_ba, z_a, br_off, tm=1024, tn=512):
    T, K = hm.shape
    D = w_bm.shape[1]
    tm, tn = min(tm, T), min(tn, D)
    ob = br_off // tn
    return pl.pallas_call(
        _merge_kernel, grid=(D // tn, T // tm),
        in_specs=[pl.BlockSpec((tm, K), lambda j, i: (i, 0)),
                  pl.BlockSpec((tm, K), lambda j, i: (i, 0)),
                  pl.BlockSpec((K, tn), lambda j, i: (0, j)),
                  pl.BlockSpec((K, tn), lambda j, i: (0, j)),
                  pl.BlockSpec((tm, tn), lambda j, i: (i, ob + j)),
                  pl.BlockSpec((tm, tn), lambda j, i: (i, ob + D // tn + j))],
        out_specs=pl.BlockSpec((tm, tn), lambda j, i: (i, j)),
        out_shape=jax.ShapeDtypeStruct((T, D), BF16),
        scratch_shapes=[pltpu.VMEM((K, tn), BF16), pltpu.VMEM((K, tn), BF16)],
        compiler_params=_cparams(("parallel", "arbitrary")), name="merge")(
            hm, ha, w_bm, w_ba, z_a, z_a)


def _mm_res_kernel(a_ref, w_ref, r_ref, o_ref, wbf_ref):
    _cache_bf16(w_ref, wbf_ref)
    o_ref[...] = r_ref[...] + jnp.dot(a_ref[...], wbf_ref[...], preferred_element_type=F32)


def _matmul_residual(a, b, res, tm=1024, tn=512):
    M, K = a.shape
    N = b.shape[1]
    tm, tn = min(tm, M), min(tn, N)
    return pl.pallas_call(
        _mm_res_kernel, grid=(N // tn, M // tm),
        in_specs=[pl.BlockSpec((tm, K), lambda j, i: (i, 0)),
                  pl.BlockSpec((K, tn), lambda j, i: (0, j)),
                  pl.BlockSpec((tm, tn), lambda j, i: (i, j))],
        out_specs=pl.BlockSpec((tm, tn), lambda j, i: (i, j)),
        out_shape=jax.ShapeDtypeStruct((M, N), F32),
        scratch_shapes=[pltpu.VMEM((K, tn), BF16)],
        compiler_params=_cparams(("parallel", "arbitrary")), name="out_proj")(a, b, res)


def _router_kernel(x_ref, g_ref, w_ref, b_ref, o_ref, xn3_ref):
    xf = _rms(x_ref[...], g_ref[...])
    nch = xf.shape[1] // (2 * LANES)
    tm = xf.shape[0]
    pitch = xn3_ref.shape[0] // tm
    for c in range(pitch):
        xn3_ref[pl.ds(c, tm, stride=pitch), :] = (
            _pack_bf16_pair(xf[:, 2 * c * LANES:(2 * c + 1) * LANES],
                            xf[:, (2 * c + 1) * LANES:(2 * c + 2) * LANES])
            if c < nch else jnp.zeros((tm, LANES), jnp.int32))
    xn = xf.astype(BF16)
    logits = jnp.dot(xn, w_ref[...], preferred_element_type=F32) + b_ref[...]
    lane = lax.broadcasted_iota(jnp.int32, logits.shape, 1)
    big = jnp.int32(1 << 20)
    lg = jnp.where(lane < N_GROUPS, logits, NEG)
    gmax = jnp.max(lg, axis=1, keepdims=True)
    g_idx = jnp.min(jnp.where(lg == gmax, lane, big), axis=1, keepdims=True)
    g_w = 1.0 / jnp.sum(jnp.exp(lg - gmax), axis=1, keepdims=True)
    lo = N_GROUPS + g_idx * E_PER_GROUP
    le = jnp.where((lane >= lo) & (lane < lo + E_PER_GROUP), logits, NEG)
    v1 = jnp.max(le, axis=1, keepdims=True)
    i1 = jnp.min(jnp.where(le == v1, lane, big), axis=1, keepdims=True)
    le2 = jnp.where(lane == i1, NEG, le)
    v2 = jnp.max(le2, axis=1, keepdims=True)
    i2 = jnp.min(jnp.where(le2 == v2, lane, big), axis=1, keepdims=True)
    e21 = jnp.exp(v2 - v1)
    w1 = g_w / (1.0 + e21)
    w2 = g_w * e21 / (1.0 + e21)
    o_ref[...] = jnp.where(
        lane == 0, (i1 - N_GROUPS).astype(F32),
        jnp.where(lane == 1, (i2 - N_GROUPS).astype(F32),
                  jnp.where(lane == 2, w1, jnp.where(lane == 3, w2, 0.0))))


def _router(x, g, w_r, b_r, tm=512):
    T, D = x.shape
    tm = min(tm, T)
    pitch = _row_pitch(D // (2 * LANES))
    return pl.pallas_call(
        _router_kernel, grid=(T // tm,),
        in_specs=[pl.BlockSpec((tm, D), lambda i: (i, 0)),
                  pl.BlockSpec((1, D), lambda i: (0, 0)),
                  pl.BlockSpec((D, LANES), lambda i: (0, 0)),
                  pl.BlockSpec((1, LANES), lambda i: (0, 0))],
        out_specs=[pl.BlockSpec((tm, LANES), lambda i: (i, 0)),
                   pl.BlockSpec((tm * pitch, LANES), lambda i: (i, 0))],
        out_shape=[jax.ShapeDtypeStruct((T, LANES), F32),
                   jax.ShapeDtypeStruct((T * pitch, LANES), jnp.int32)],
        compiler_params=_cparams(("parallel",)), name="router")(x, g.reshape(1, D), w_r, b_r)


def _pack_bf16_pair(a, b):
    ua = lax.bitcast_convert_type(a.astype(BF16).astype(F32), jnp.int32)
    ub = lax.bitcast_convert_type(b.astype(BF16).astype(F32), jnp.int32)
    return lax.shift_right_logical(ua, jnp.int32(16)) | (ub & jnp.int32(-65536))


def _unpack_bf16_pair(u):
    return [lax.bitcast_convert_type(lax.shift_left(u, jnp.int32(16)), F32),
            lax.bitcast_convert_type(u & jnp.int32(-65536), F32)]


def _row_pitch(nch):
    return (nch + 7) // 8 * 8 + 8


def _gather_rows(start_one, wait_one, n, unroll=8):
    def start():
        def body(r, carry):
            start_one(r)
            return carry
        lax.fori_loop(0, n, body, 0, unroll=unroll)

    def wait():
        def body(r, carry):
            wait_one(r)
            return carry
        lax.fori_loop(0, n, body, 0, unroll=unroll)
    return start, wait


def _weight_prefetch(i, nused, blk_e_ref, first_ref, par_ref, nxt_ref, copies):
    @pl.when(i == 0)
    def _():
        for c in copies(blk_e_ref[0], 0):
            c.start()

    @pl.when((i < nused) & (first_ref[i] == 1))
    def _():
        slot = par_ref[i]
        for c in copies(blk_e_ref[i], slot):
            c.wait()

        @pl.when(nxt_ref[i] >= 0)
        def _():
            for c in copies(nxt_ref[i], 1 - slot):
                c.start()


def _expert_up_kernel(blk_e_ref, nused_ref, first_ref, par_ref, nxt_ref, src_ref, tok_ref,
                      x_hbm, w1_hbm, w3_hbm, h_ref, xbuf, wb1, wb3, sem, wsem,
                      *, blk, n_assign, nch, pitch, kch):
    i = pl.program_id(0)
    nused = nused_ref[0]

    def row_copy(step, r, slot):
        idx = jnp.minimum(src_ref[step] + r, n_assign - 1)
        return pltpu.make_async_copy(
            x_hbm.at[pl.ds(pl.multiple_of(tok_ref[idx] * pitch, 8), nch), :],
            xbuf.at[slot, pl.ds(pl.multiple_of(r * pitch, 8), nch), :], sem.at[slot])

    def gather(step, slot):
        return _gather_rows(lambda r: row_copy(step, r, slot).start(),
                            lambda r: row_copy(step, r, slot).wait(), blk)

    def weight_copies(e, slot):
        return [pltpu.make_async_copy(w1_hbm.at[e], wb1.at[slot], wsem.at[0, slot]),
                pltpu.make_async_copy(w3_hbm.at[e], wb3.at[slot], wsem.at[1, slot])]

    @pl.when(i == 0)
    def _():
        gather(0, 0)[0]()

    @pl.when(i + 1 < nused)
    def _():
        gather(i + 1, (i + 1) % 2)[0]()

    _weight_prefetch(i, nused, blk_e_ref, first_ref, par_ref, nxt_ref, weight_copies)

    @pl.when(i < nused)
    def _():
        slot = i % 2
        wslot = par_ref[i]
        gather(i, slot)[1]()
        parts = []
        for c in range(nch):
            parts += _unpack_bf16_pair(xbuf[slot, pl.ds(c, blk, stride=pitch), :])
        xn = jnp.concatenate([t.astype(BF16) for t in parts], axis=1)
        h1 = h3 = None
        for k in range(xn.shape[1] // kch):
            xk = xn[:, k * kch:(k + 1) * kch]
            d1 = jnp.dot(xk, wb1[wslot, k * kch:(k + 1) * kch, :].astype(BF16),
                         preferred_element_type=F32)
            d3 = jnp.dot(xk, wb3[wslot, k * kch:(k + 1) * kch, :].astype(BF16),
                         preferred_element_type=F32)
            h1 = d1 if h1 is None else h1 + d1
            h3 = d3 if h3 is None else h3 + d3
        h_ref[...] = (h1 * _sigmoid(h1) * h3).astype(h_ref.dtype)

    @pl.when(i >= nused)
    def _():
        h_ref[...] = jnp.zeros_like(h_ref)


def _expert_down_kernel(blk_e_ref, nused_ref, first_ref, par_ref, nxt_ref, h_ref, w2_hbm, y_ref,
                        wb2, wsem, *, blk, nch, pitch, ych):
    i = pl.program_id(0)
    nused = nused_ref[0]

    def weight_copies(e, slot):
        return [pltpu.make_async_copy(w2_hbm.at[e], wb2.at[slot], wsem.at[slot])]

    _weight_prefetch(i, nused, blk_e_ref, first_ref, par_ref, nxt_ref, weight_copies)

    @pl.when(i < nused)
    def _():
        wslot = par_ref[i]
        hdn = h_ref[...]
        for j in range(nch * 2 * LANES // ych):
            yj = jnp.dot(hdn, wb2[wslot, :, j * ych:(j + 1) * ych].astype(BF16),
                         preferred_element_type=F32)
            for c in range(ych // (2 * LANES)):
                y_ref[pl.ds(j * (ych // (2 * LANES)) + c, blk, stride=pitch), :] = (
                    _pack_bf16_pair(yj[:, 2 * c * LANES:(2 * c + 1) * LANES],
                                    yj[:, (2 * c + 1) * LANES:(2 * c + 2) * LANES]))
        for c in range(nch, pitch):
            y_ref[pl.ds(c, blk, stride=pitch), :] = jnp.zeros((blk, LANES), jnp.int32)

    @pl.when(i >= nused)
    def _():
        y_ref[...] = jnp.zeros_like(y_ref)


def _experts(xn3, w1, w3, w2, blk_e, nused, src_start, tok_sorted, blk):
    E, D, De = w1.shape
    nch = D // (2 * LANES)
    pitch = _row_pitch(nch)
    nb = blk_e.shape[0]
    n_assign = tok_sorted.shape[0]
    b = jnp.arange(nb, dtype=jnp.int32)
    first = (b < nused[0]) & (blk_e != jnp.concatenate([jnp.full((1,), -1, jnp.int32), blk_e[:-1]]))
    par = ((jnp.cumsum(first.astype(jnp.int32)) - 1) % 2).astype(jnp.int32)
    nxt_first = lax.cummin(jnp.where(first, b, nb), axis=0, reverse=True)
    nxt_first = jnp.concatenate([nxt_first[1:], jnp.full((1,), nb, jnp.int32)])
    nxt = jnp.where(nxt_first < nb, blk_e[jnp.minimum(nxt_first, nb - 1)], -1).astype(jnp.int32)
    first = first.astype(jnp.int32)
    up_spec = pltpu.PrefetchScalarGridSpec(
        num_scalar_prefetch=7, grid=(nb,),
        in_specs=[pl.BlockSpec(memory_space=pl.ANY)] * 3,
        out_specs=pl.BlockSpec((blk, De), lambda i, *_: (i, 0)),
        scratch_shapes=[pltpu.VMEM((2, blk * pitch, LANES), jnp.int32),
                        pltpu.VMEM((2, D, De), F32), pltpu.VMEM((2, D, De), F32),
                        pltpu.SemaphoreType.DMA((2,)), pltpu.SemaphoreType.DMA((2, 2))])
    hdn = pl.pallas_call(
        functools.partial(_expert_up_kernel, blk=blk, n_assign=n_assign, nch=nch, pitch=pitch,
                          kch=min(1024, D)),
        grid_spec=up_spec,
        out_shape=jax.ShapeDtypeStruct((nb * blk, De), BF16),
        compiler_params=_cparams(("arbitrary",)), name="experts_up")(
            blk_e, nused, first, par, nxt, src_start, tok_sorted, xn3, w1, w3)
    down_spec = pltpu.PrefetchScalarGridSpec(
        num_scalar_prefetch=5, grid=(nb,),
        in_specs=[pl.BlockSpec((blk, De), lambda i, *_: (i, 0)),
                  pl.BlockSpec(memory_space=pl.ANY)],
        out_specs=pl.BlockSpec((blk * pitch, LANES), lambda i, *_: (i, 0)),
        scratch_shapes=[pltpu.VMEM((2, De, D), F32), pltpu.SemaphoreType.DMA((2,))])
    return pl.pallas_call(
        functools.partial(_expert_down_kernel, blk=blk, nch=nch, pitch=pitch, ych=min(1024, D)),
        grid_spec=down_spec,
        out_shape=jax.ShapeDtypeStruct((nb * blk * pitch, LANES), jnp.int32),
        compiler_params=_cparams(("arbitrary",)), name="experts_down")(
            blk_e, nused, first, par, nxt, hdn, w2)


def _combine_kernel(pos_ref, y_hbm, x_ref, rt_ref, g_ref, x2_ref, xn_ref, ybuf, sem, *, tc, nch,
                    pitch):
    i = pl.program_id(0)
    n = pl.num_programs(0)

    def row_copy(step, r, slot):
        return pltpu.make_async_copy(
            y_hbm.at[pl.ds(pl.multiple_of(pos_ref[step * 2 * tc + r] * pitch, 8), nch), :],
            ybuf.at[slot, pl.ds(pl.multiple_of(r * pitch, 8), nch), :], sem.at[slot])

    def gather(step, slot):
        return _gather_rows(lambda r: row_copy(step, r, slot).start(),
                            lambda r: row_copy(step, r, slot).wait(), 2 * tc)

    @pl.when(i == 0)
    def _():
        gather(0, 0)[0]()

    @pl.when(i + 1 < n)
    def _():
        gather(i + 1, (i + 1) % 2)[0]()

    slot = i % 2
    gather(i, slot)[1]()
    y0, y1 = [], []
    for c in range(nch):
        y0 += _unpack_bf16_pair(ybuf[slot, pl.ds(c, tc, stride=pitch), :])
        y1 += _unpack_bf16_pair(ybuf[slot, pl.ds(tc * pitch + c, tc, stride=pitch), :])
    y0 = jnp.concatenate(y0, axis=1)
    y1 = jnp.concatenate(y1, axis=1)
    x2 = x_ref[...] + rt_ref[:, 2:3] * y0 + rt_ref[:, 3:4] * y1
    x2_ref[...] = x2
    xn_ref[...] = _rms(x2, g_ref[...]).astype(xn_ref.dtype)


def _combine(y, x, route, g, pos, tc=256):
    T, D = x.shape
    tc = min(tc, T)
    nch = D // (2 * LANES)
    pitch = _row_pitch(nch)
    grid_spec = pltpu.PrefetchScalarGridSpec(
        num_scalar_prefetch=1, grid=(T // tc,),
        in_specs=[pl.BlockSpec(memory_space=pl.ANY),
                  pl.BlockSpec((tc, D), lambda i, ps: (i, 0)),
                  pl.BlockSpec((tc, LANES), lambda i, ps: (i, 0)),
                  pl.BlockSpec((1, D), lambda i, ps: (0, 0))],
        out_specs=[pl.BlockSpec((tc, D), lambda i, ps: (i, 0)),
                   pl.BlockSpec((tc, D), lambda i, ps: (i, 0))],
        scratch_shapes=[pltpu.VMEM((2, 2 * tc * pitch, LANES), jnp.int32),
                        pltpu.SemaphoreType.DMA((2,))])
    pos_tiled = pos.reshape(T // tc, tc, 2).transpose(0, 2, 1).reshape(-1)
    return pl.pallas_call(
        functools.partial(_combine_kernel, tc=tc, nch=nch, pitch=pitch), grid_spec=grid_spec,
        out_shape=[jax.ShapeDtypeStruct((T, D), F32), jax.ShapeDtypeStruct((T, D), BF16)],
        compiler_params=_cparams(("arbitrary",)), name="combine")(
            pos_tiled, y, x, route, g.reshape(1, D))


def _ple_kernel(xn_ref, wg_ref, p_ref, wp_ref, x_ref, o_ref, wg_bf, wp_bf):
    _cache_bf16(wg_ref, wg_bf)
    _cache_bf16(wp_ref, wp_bf)
    gate = _sigmoid(jnp.dot(xn_ref[...], wg_bf[...], preferred_element_type=F32))
    emb = jnp.dot(p_ref[...].astype(BF16), wp_bf[...], preferred_element_type=F32)
    o_ref[...] = x_ref[...] + gate * emb


def _ple(xn, w_pg, p, w_ple, x, tm=1024, tn=512):
    T, D = xn.shape
    Pd = p.shape[1]
    tm, tn = min(tm, T), min(tn, D)
    return pl.pallas_call(
        _ple_kernel, grid=(D // tn, T // tm),
        in_specs=[pl.BlockSpec((tm, D), lambda j, i: (i, 0)),
                  pl.BlockSpec((D, tn), lambda j, i: (0, j)),
                  pl.BlockSpec((tm, Pd), lambda j, i: (i, 0)),
                  pl.BlockSpec((Pd, tn), lambda j, i: (0, j)),
                  pl.BlockSpec((tm, tn), lambda j, i: (i, j))],
        out_specs=pl.BlockSpec((tm, tn), lambda j, i: (i, j)),
        out_shape=jax.ShapeDtypeStruct((T, D), F32),
        scratch_shapes=[pltpu.VMEM((D, tn), BF16), pltpu.VMEM((Pd, tn), BF16)],
        compiler_params=_cparams(("parallel", "arbitrary")), name="ple")(xn, w_pg, p, w_ple, x)


def _dispatch_tables(route, blk):
    T = route.shape[0]
    A = 2 * T
    eid = route[:, 0:2].astype(jnp.int32).reshape(-1)
    iota = jnp.arange(A, dtype=jnp.int32)
    experts = jnp.arange(N_EXPERTS, dtype=jnp.int32)
    _, order = lax.sort((eid, iota), num_keys=1)
    _, inv = lax.sort((order, iota), num_keys=1)
    onehot = eid[:, None] == experts[None, :]
    counts = jnp.sum(onehot, axis=0, dtype=jnp.int32)
    starts = jnp.cumsum(counts) - counts
    nblk = (counts + blk - 1) // blk
    bends = jnp.cumsum(nblk)
    bstarts = bends - nblk
    nb = (A + N_EXPERTS * blk) // blk
    b = jnp.arange(nb, dtype=jnp.int32)
    blk_e = jnp.minimum(jnp.sum(b[:, None] >= bends[None, :], axis=1),
                        N_EXPERTS - 1).astype(jnp.int32)
    shift = starts - bstarts * blk
    src_start = b * blk + jnp.sum(
        jnp.where(blk_e[:, None] == experts[None, :], shift[None, :], 0), axis=1)
    nused = bends[-1].astype(jnp.int32).reshape(1)
    pos = inv - jnp.sum(jnp.where(onehot, shift[None, :], 0), axis=1)
    return blk_e, nused, src_start.astype(jnp.int32), order // 2, pos.reshape(T, 2)


def _layer(x2d, p2d, pos_col, B, S, g_mix, w_in, b_mgate, g_mhead, g_cq, w_uq, g_ckv, w_ukv,
           g_qn, g_kn, w_bm, w_ba, w_out, g_ffn, w_rg, b_rg, w_re, b_re, w_e1, w_e3, w_e2,
           g_ple, w_pg, w_ple):
    T, D = x2d.shape
    H = M_HEADS
    mqk, mv = H * M_QK_HD, H * M_V_HD
    o_gate = 2 * mqk + 2 * mv
    o_cq = o_gate + 4 * H
    o_ckv = o_cq + Q_LORA
    o_kpe = o_ckv + KV_LORA
    o_br = o_kpe + A_ROPE
    w_t = jnp.swapaxes(w_in, 0, 1)
    w_br_t, w_mid_t = lax.optimization_barrier((w_t[o_br:], w_t[o_gate:o_br]))
    w_cq = w_mid_t[o_cq - o_gate:o_ckv - o_gate].T.astype(BF16)
    w_ckv = w_mid_t[o_ckv - o_gate:o_kpe - o_gate].T.astype(BF16)
    w_sm = jnp.concatenate(
        [w_mid_t[o_kpe - o_gate:], w_mid_t[:4 * H],
         jnp.zeros((LANES - A_ROPE - 4 * H, D), F32)], axis=0).T.astype(BF16)

    h = _rmsnorm(x2d, g_mix)
    z_a = _matmul_t(h, w_t, o_gate, BF16, name="inproj_mlstm")
    z_br = _matmul_t(h, w_br_t, 2 * D, BF16, name="inproj_gates")
    cq_n, ckv_n, small = _inproj_small(h, w_cq, w_ckv, w_sm, g_cq, g_ckv)

    g_col = small[:, A_ROPE:A_ROPE + 4 * H] + b_mgate[None, :]
    g_row = g_col.reshape(B, S, 4 * H).transpose(0, 2, 1)
    hb = _mlstm(z_a, g_col, g_row, B, S, reverse=True)
    hm = _mlstm(z_a, g_col, g_row, B, S, reverse=False, hb=hb, g_mhead=g_mhead)

    pad_q = A_QK_PAD - A_QK_HD
    w_uq_p = jnp.pad(w_uq.reshape(Q_LORA, A_HEADS, A_QK_HD).transpose(1, 0, 2),
                     ((0, 0), (0, 0), (0, pad_q))).astype(BF16)
    w_ukv_h = w_ukv.reshape(KV_LORA, A_HEADS, A_NOPE + A_V_HD).transpose(1, 0, 2).astype(BF16)
    g_qn_p = jnp.pad(g_qn, (0, pad_q)).reshape(1, A_QK_PAD)
    g_kn_p = jnp.pad(g_kn, (0, pad_q)).reshape(1, A_QK_PAD)
    inv = ROPE_THETA ** (-jnp.arange(0, A_ROPE, 2, dtype=F32) / A_ROPE)
    inv_row = jnp.concatenate([inv, inv, jnp.zeros((LANES - A_ROPE,), F32)]).reshape(1, LANES)
    cos_t, sin_t = _rope_tables(pos_col, inv_row)
    q = _q_prep(cq_n, w_uq_p, g_qn_p, cos_t, sin_t)
    kt, v = _kv_prep(ckv_n, w_ukv_h, small, g_kn_p, cos_t, sin_t)
    ha = _attention(q, kt, v, B, S)

    mix = _merge(hm, ha, w_bm, w_ba, z_br, 0)
    x1 = _matmul_residual(mix, w_out, x2d)

    w_r = jnp.concatenate([w_rg, w_re, jnp.zeros((D, LANES - N_GROUPS - N_EXPERTS), F32)],
                          axis=1).astype(BF16)
    b_r = jnp.concatenate([b_rg, b_re, jnp.zeros((LANES - N_GROUPS - N_EXPERTS,), F32)]
                          ).reshape(1, LANES)
    route, xn3 = _router(x1, g_ffn, w_r, b_r)
    blk = min(MOE_BLK, T)
    blk_e, nused, src_start, tok_sorted, pos = _dispatch_tables(route, blk)
    y = _experts(xn3, w_e1, w_e3, w_e2, blk_e, nused, src_start, tok_sorted, blk)
    x2, xn2 = _combine(y, x1, route, g_ple, pos)

    return _ple(xn2, w_pg, p2d, w_ple, x2)


def kernel(x, p, positions, g_mix, w_in, b_mgate, g_mhead, g_cq, w_uq, g_ckv, w_ukv, g_qn, g_kn,
           w_bm, w_ba, w_out, g_ffn, w_rg, b_rg, w_re, b_re, w_e1, w_e3, w_e2, g_ple, w_pg, w_ple):
    B, S, D = x.shape
    T = B * S
    x2d = x.reshape(T, D)
    pos_col = positions.reshape(T, 1).astype(jnp.int32)
    for l in range(p.shape[0]):
        x2d = _layer(x2d, p[l].reshape(T, -1), pos_col, B, S, g_mix[l], w_in[l], b_mgate[l],
                     g_mhead[l], g_cq[l], w_uq[l], g_ckv[l], w_ukv[l], g_qn[l], g_kn[l],
                     w_bm[l], w_ba[l], w_out[l], g_ffn[l], w_rg[l], b_rg[l], w_re[l], b_re[l],
                     w_e1[l], w_e3[l], w_e2[l], g_ple[l], w_pg[l], w_ple[l])
    return x2d.reshape(B, S, D)
```

```python
import functools
import math

import jax
import jax.numpy as jnp
from jax import lax
from jax.experimental import pallas as pl
from jax.experimental.pallas import tpu as pltpu

F32 = jnp.float32
BF16 = jnp.bfloat16

EPS = 1e-6
M_HEADS = 8
M_QK_HD = 128
M_V_HD = 256
A_HEADS = 16
A_NOPE = 128
A_ROPE = 64
A_QK_HD = A_NOPE + A_ROPE
A_V_HD = 128
A_QK_PAD = 256
Q_LORA = 1024
KV_LORA = 512
ROPE_THETA = 10000.0
N_GROUPS = 8
E_PER_GROUP = 8
N_EXPERTS = N_GROUPS * E_PER_GROUP
D_EXPERT = 512
LANES = 128

M_CHUNK = 256
MOE_BLK = 256
ATT_TQ = 1024
ATT_TK = 256
VMEM_LIMIT = 56 * 1024 * 1024
NEG = -1e30
LOG2E = 1.4426950408889634


def _cparams(sem):
    return pltpu.CompilerParams(dimension_semantics=sem, vmem_limit_bytes=VMEM_LIMIT)


def _sigmoid(x):
    return 1.0 / (1.0 + jnp.exp(-x))


def _log_sigmoid(x):
    return jnp.minimum(x, 0.0) - jnp.log(1.0 + jnp.exp(-jnp.abs(x)))


def _rms(xf, g):
    return xf * lax.rsqrt(jnp.mean(xf * xf, axis=-1, keepdims=True) + EPS) * g


def _norm_kernel(x_ref, g_ref, o_ref):
    o_ref[...] = _rms(x_ref[...], g_ref[...]).astype(o_ref.dtype)


def _rmsnorm(x, g, tm=512):
    M, D = x.shape
    tm = min(tm, M)
    return pl.pallas_call(
        _norm_kernel, grid=(M // tm,),
        in_specs=[pl.BlockSpec((tm, D), lambda i: (i, 0)),
                  pl.BlockSpec((1, D), lambda i: (0, 0))],
        out_specs=pl.BlockSpec((tm, D), lambda i: (i, 0)),
        out_shape=jax.ShapeDtypeStruct((M, D), BF16),
        compiler_params=_cparams(("parallel",)), name="rmsnorm")(x, g.reshape(1, D))


def _cache_bf16(w_ref, wbf_ref):
    @pl.when(pl.program_id(1) == 0)
    def _():
        wbf_ref[...] = w_ref[...].astype(BF16)


def _mm_t_kernel(a_ref, wt_ref, o_ref, wbf_ref):
    @pl.when(pl.program_id(1) == 0)
    def _():
        wbf_ref[...] = wt_ref[...].astype(BF16)
    o_ref[...] = lax.dot_general(a_ref[...], wbf_ref[...], (((1,), (1,)), ((), ())),
                                 preferred_element_type=F32).astype(o_ref.dtype)


def _matmul_t(a, w_t, n_cols, out_dtype, tm=1024, tn=512, name="matmul"):
    M, K = a.shape
    tm, tn = min(tm, M), min(tn, n_cols)
    return pl.pallas_call(
        _mm_t_kernel, grid=(n_cols // tn, M // tm),
        in_specs=[pl.BlockSpec((tm, K), lambda j, i: (i, 0)),
                  pl.BlockSpec((tn, K), lambda j, i: (j, 0))],
        out_specs=pl.BlockSpec((tm, tn), lambda j, i: (i, j)),
        out_shape=jax.ShapeDtypeStruct((M, n_cols), out_dtype),
        scratch_shapes=[pltpu.VMEM((tn, K), BF16)],
        compiler_params=_cparams(("parallel", "arbitrary")), name=name)(a, w_t)


def _inproj_small_kernel(h_ref, wq_ref, wkv_ref, ws_ref, gq_ref, gkv_ref,
                         cq_ref, ckv_ref, sm_ref):
    h = h_ref[...]
    cq = jnp.dot(h, wq_ref[...], preferred_element_type=F32)
    cq_ref[...] = _rms(cq, gq_ref[...]).astype(cq_ref.dtype)
    ckv = jnp.dot(h, wkv_ref[...], preferred_element_type=F32)
    ckv_ref[...] = _rms(ckv, gkv_ref[...]).astype(ckv_ref.dtype)
    sm_ref[...] = jnp.dot(h, ws_ref[...], preferred_element_type=F32)


def _inproj_small(h, w_cq, w_ckv, w_sm, g_cq, g_ckv, tm=512):
    M, D = h.shape
    tm = min(tm, M)
    nq, nkv, ns = w_cq.shape[1], w_ckv.shape[1], w_sm.shape[1]
    full = lambda n: pl.BlockSpec((D, n), lambda i: (0, 0))
    row = lambda n: pl.BlockSpec((tm, n), lambda i: (i, 0))
    return pl.pallas_call(
        _inproj_small_kernel, grid=(M // tm,),
        in_specs=[row(D), full(nq), full(nkv), full(ns),
                  pl.BlockSpec((1, nq), lambda i: (0, 0)),
                  pl.BlockSpec((1, nkv), lambda i: (0, 0))],
        out_specs=[row(nq), row(nkv), row(ns)],
        out_shape=[jax.ShapeDtypeStruct((M, nq), BF16),
                   jax.ShapeDtypeStruct((M, nkv), BF16),
                   jax.ShapeDtypeStruct((M, ns), F32)],
        compiler_params=_cparams(("parallel",)), name="inproj_small")(
            h, w_cq, w_ckv, w_sm, g_cq.reshape(1, nq), g_ckv.reshape(1, nkv))


def _mlstm_kernel(*refs, reverse, finish, L, goff):
    if finish:
        (q_ref, k_ref, v_ref, gc_ref, gr_ref, hb_ref, o_ref, gm_ref,
         out_ref, C_ref, n_ref, m_ref) = refs
    else:
        q_ref, k_ref, v_ref, gc_ref, gr_ref, out_ref, C_ref, n_ref, m_ref = refs
    H, dk, dv = M_HEADS, M_QK_HD, M_V_HD
    log_scale = -0.5 * math.log(dk)

    @pl.when(pl.program_id(1) == 0)
    def _():
        C_ref[...] = jnp.zeros_like(C_ref)
        n_ref[...] = jnp.zeros_like(n_ref)
        m_ref[...] = jnp.zeros_like(m_ref)

    t_idx = lax.broadcasted_iota(jnp.int32, (L, L), 0)
    s_idx = lax.broadcasted_iota(jnp.int32, (L, L), 1)
    if reverse:
        mask, mask_t = s_idx >= t_idx, t_idx >= s_idx
    else:
        mask, mask_t = s_idx <= t_idx, t_idx <= s_idx

    hs = range(H)
    gates = []
    for h in hs:
        i_col = gc_ref[:, goff + h:goff + h + 1]
        f_col = gc_ref[:, goff + H + h:goff + H + h + 1]
        i_row = gr_ref[goff + h:goff + h + 1, :]
        f_row = gr_ref[goff + H + h:goff + H + h + 1, :]
        lf_col = _log_sigmoid(f_col)
        lf_row = _log_sigmoid(f_row)
        b_col = jnp.sum(jnp.where(mask, lf_row, 0.0), axis=1, keepdims=True)
        b_row = jnp.sum(jnp.where(mask_t, lf_col, 0.0), axis=0, keepdims=True)
        g = jnp.sum(lf_row, axis=1, keepdims=True)
        gates.append((i_col, i_row, b_col, b_row, g, m_ref[h][:, 0:1]))

    stab = []
    for h in hs:
        i_col, i_row, b_col, b_row, g, m_prev = gates[h]
        dmat = jnp.where(mask, b_col - b_row + i_row, NEG)
        a = b_col + m_prev
        m_t = jnp.maximum(a, jnp.max(dmat, axis=1, keepdims=True))
        stab.append((dmat, a, m_t))

    qs = [q_ref[:, h * dk:(h + 1) * dk] for h in hs]
    ks = [k_ref[:, h * dk:(h + 1) * dk] for h in hs]
    vs = [v_ref[:, h * dv:(h + 1) * dv] for h in hs]
    qks = [lax.dot_general(qs[h], ks[h], (((1,), (1,)), ((), ())),
                           preferred_element_type=F32) for h in hs]

    intra = []
    for h in hs:
        dmat, a, m_t = stab[h]
        m_s = m_t - log_scale
        intra.append((jnp.exp(dmat - m_s) * qks[h], jnp.exp(a - m_s)))

    c_states = [C_ref[h] for h in hs]
    n_states = [n_ref[h] for h in hs]
    outs = []
    for h in hs:
        w_intra, w_inter = intra[h]
        q_c = jnp.dot(qs[h], c_states[h].astype(BF16), preferred_element_type=F32)
        num = jnp.dot(w_intra.astype(BF16), vs[h], preferred_element_type=F32) + w_inter * q_c
        q_n = jnp.sum(qs[h].astype(F32) * n_states[h], axis=1, keepdims=True)
        den = jnp.sum(w_intra, axis=1, keepdims=True) + w_inter * q_n
        outs.append(num / jnp.maximum(jnp.abs(den), jnp.exp(-stab[h][2])))

    for h in hs:
        i_col, i_row, b_col, b_row, g, m_prev = gates[h]
        w_col = g - b_col + i_col
        m_new = jnp.maximum(g + m_prev, jnp.max(w_col, axis=0, keepdims=True))
        decay = jnp.exp(g + m_prev - m_new)
        kw = ks[h].astype(F32) * jnp.exp(w_col - m_new)
        C_ref[h] = decay * c_states[h] + lax.dot_general(
            kw.astype(BF16), vs[h], (((0,), (0,)), ((), ())), preferred_element_type=F32)
        n_ref[h] = decay * n_states[h] + jnp.sum(kw, axis=0, keepdims=True)
        m_ref[h] = jnp.broadcast_to(m_new, (1, LANES))

    for h in hs:
        if finish:
            tot = outs[h] + hb_ref[:, h * dv:(h + 1) * dv]
            y = _rms(tot, gm_ref[h:h + 1, :])
            gate = _sigmoid(o_ref[:, h * dv:(h + 1) * dv].astype(F32))
            out_ref[:, h * dv:(h + 1) * dv] = (y * gate).astype(out_ref.dtype)
        else:
            out_ref[:, h * dv:(h + 1) * dv] = outs[h]


def _mlstm(z_a, g_col, g_row, B, S, *, reverse, hb=None, g_mhead=None):
    H, dk, dv = M_HEADS, M_QK_HD, M_V_HD
    L = min(M_CHUNK, S)
    nc = S // L
    T = B * S
    finish = not reverse
    if reverse:
        cidx = lambda b, c: b * nc + (nc - 1 - c)
        ridx = lambda b, c: nc - 1 - c
    else:
        cidx = lambda b, c: b * nc + c
        ridx = lambda b, c: c
    mqk, mv = H * dk, H * dv
    in_specs = [
        pl.BlockSpec((L, mqk), lambda b, c: (cidx(b, c), 0)),
        pl.BlockSpec((L, mqk), lambda b, c: (cidx(b, c), 1)),
        pl.BlockSpec((L, mv), lambda b, c: (cidx(b, c), 1)),
        pl.BlockSpec((L, 4 * H), lambda b, c: (cidx(b, c), 0)),
        pl.BlockSpec((None, 4 * H, L), lambda b, c: (b, 0, ridx(b, c))),
    ]
    args = [z_a, z_a, z_a, g_col, g_row]
    if finish:
        in_specs += [
            pl.BlockSpec((L, mv), lambda b, c: (cidx(b, c), 0)),
            pl.BlockSpec((L, mv), lambda b, c: (cidx(b, c), 2)),
            pl.BlockSpec((H, dv), lambda b, c: (0, 0)),
        ]
        args += [hb, z_a, g_mhead.reshape(H, dv)]
    kern = functools.partial(_mlstm_kernel, reverse=reverse, finish=finish, L=L,
                             goff=2 * H if reverse else 0)
    return pl.pallas_call(
        kern, grid=(B, nc), in_specs=in_specs,
        out_specs=pl.BlockSpec((L, mv), lambda b, c: (cidx(b, c), 0)),
        out_shape=jax.ShapeDtypeStruct((T, mv), BF16 if finish else F32),
        scratch_shapes=[pltpu.VMEM((H, dk, dv), F32), pltpu.VMEM((H, 1, dk), F32),
                        pltpu.VMEM((H, 1, LANES), F32)],
        compiler_params=_cparams(("parallel", "arbitrary")),
        name="mlstm_bwd" if reverse else "mlstm_fwd")(*args)


def _rope_table_kernel(pos_ref, inv_ref, cos_ref, sin_ref):
    ang = pos_ref[...].astype(F32) * inv_ref[...]
    lane = lax.broadcasted_iota(jnp.int32, ang.shape, 1)
    cos_ref[...] = jnp.cos(ang)
    sin_ref[...] = jnp.where(lane < A_ROPE // 2, -jnp.sin(ang), jnp.sin(ang))


def _rope_tables(pos_col, inv_row, tm=1024):
    T = pos_col.shape[0]
    tm = min(tm, T)
    return pl.pallas_call(
        _rope_table_kernel, grid=(T // tm,),
        in_specs=[pl.BlockSpec((tm, 1), lambda i: (i, 0)),
                  pl.BlockSpec((1, LANES), lambda i: (0, 0))],
        out_specs=[pl.BlockSpec((tm, LANES), lambda i: (i, 0))] * 2,
        out_shape=[jax.ShapeDtypeStruct((T, LANES), F32)] * 2,
        compiler_params=_cparams(("parallel",)), name="rope_tables")(pos_col, inv_row)


def _rope(r, cos_ref, sin_ref):
    lane = lax.broadcasted_iota(jnp.int32, r.shape, 1)
    half = A_ROPE // 2
    swapped = jnp.where(lane < half, pltpu.roll(r, LANES - half, 1), pltpu.roll(r, half, 1))
    return r * cos_ref[...] + swapped * sin_ref[...]


PREP_HEADS = 8


def _q_prep_kernel(c_ref, w_ref, g_ref, cos_ref, sin_ref, o_ref):
    c = c_ref[...]
    hs = range(w_ref.shape[0])
    accs = [jnp.dot(c, w_ref[h], preferred_element_type=F32) for h in hs]
    rs = [lax.rsqrt(jnp.sum(a * a, axis=1, keepdims=True) / A_QK_HD + EPS) for a in accs]
    gain = g_ref[...] * (A_QK_HD ** -0.5 * LOG2E)
    ys = [accs[h] * rs[h] * gain for h in hs]
    for h in hs:
        o_ref[h, :, 0:A_NOPE] = ys[h][:, 0:A_NOPE].astype(o_ref.dtype)
    for h in hs:
        o_ref[h, :, A_NOPE:] = _rope(ys[h][:, A_NOPE:], cos_ref, sin_ref).astype(o_ref.dtype)


def _q_prep(cq_n, w_uq_p, g_qn_p, cos_t, sin_t, tm=1024):
    T, R = cq_n.shape
    tm = min(tm, T)
    H, hb = A_HEADS, PREP_HEADS
    return pl.pallas_call(
        _q_prep_kernel, grid=(T // tm, H // hb),
        in_specs=[pl.BlockSpec((tm, R), lambda i, h: (i, 0)),
                  pl.BlockSpec((hb, R, A_QK_PAD), lambda i, h: (h, 0, 0)),
                  pl.BlockSpec((1, A_QK_PAD), lambda i, h: (0, 0)),
                  pl.BlockSpec((tm, LANES), lambda i, h: (i, 0)),
                  pl.BlockSpec((tm, LANES), lambda i, h: (i, 0))],
        out_specs=pl.BlockSpec((hb, tm, A_QK_PAD), lambda i, h: (h, i, 0)),
        out_shape=jax.ShapeDtypeStruct((H, T, A_QK_PAD), BF16),
        compiler_params=_cparams(("parallel", "parallel")), name="mla_q_prep")(
            cq_n, w_uq_p, g_qn_p, cos_t, sin_t)


def _kv_prep_kernel(c_ref, w_ref, sm_ref, g_ref, cos_ref, sin_ref, kt_ref, v_ref):
    c = c_ref[...]
    lane = lax.broadcasted_iota(jnp.int32, sm_ref.shape, 1)
    k_pe = jnp.where(lane < A_ROPE, sm_ref[...], 0.0)
    ss_pe = jnp.sum(k_pe * k_pe, axis=1, keepdims=True)
    hs = range(w_ref.shape[0])
    accs = [jnp.dot(c, w_ref[h], preferred_element_type=F32) for h in hs]
    for h in hs:
        v_ref[h, :, 0:A_V_HD] = accs[h][:, A_NOPE:].astype(v_ref.dtype)
        v_ref[h, :, A_V_HD:] = jnp.ones((c.shape[0], A_V_HD), v_ref.dtype)
    rs = [lax.rsqrt((jnp.sum(a[:, 0:A_NOPE] * a[:, 0:A_NOPE], axis=1, keepdims=True) + ss_pe)
                    / A_QK_HD + EPS) for a in accs]
    kns = [accs[h][:, 0:A_NOPE] * rs[h] * g_ref[:, 0:A_NOPE] for h in hs]
    krs = [_rope(k_pe * rs[h] * g_ref[:, A_NOPE:], cos_ref, sin_ref) for h in hs]
    for h in hs:
        kt_ref[h, 0:A_NOPE, :] = kns[h].T.astype(kt_ref.dtype)
    for h in hs:
        kt_ref[h, A_NOPE:, :] = krs[h].T.astype(kt_ref.dtype)


def _kv_prep(ckv_n, w_ukv_h, small, g_kn_p, cos_t, sin_t, tm=1024):
    T, R = ckv_n.shape
    tm = min(tm, T)
    H, hb = A_HEADS, PREP_HEADS
    return pl.pallas_call(
        _kv_prep_kernel, grid=(T // tm, H // hb),
        in_specs=[pl.BlockSpec((tm, R), lambda i, h: (i, 0)),
                  pl.BlockSpec((hb, R, A_NOPE + A_V_HD), lambda i, h: (h, 0, 0)),
                  pl.BlockSpec((tm, LANES), lambda i, h: (i, 0)),
                  pl.BlockSpec((1, A_QK_PAD), lambda i, h: (0, 0)),
                  pl.BlockSpec((tm, LANES), lambda i, h: (i, 0)),
                  pl.BlockSpec((tm, LANES), lambda i, h: (i, 0))],
        out_specs=[pl.BlockSpec((hb, A_QK_PAD, tm), lambda i, h: (h, 0, i)),
                   pl.BlockSpec((hb, tm, 2 * A_V_HD), lambda i, h: (h, i, 0))],
        out_shape=[jax.ShapeDtypeStruct((H, A_QK_PAD, T), BF16),
                   jax.ShapeDtypeStruct((H, T, 2 * A_V_HD), BF16)],
        compiler_params=_cparams(("parallel", "parallel")), name="mla_kv_prep")(
            ckv_n, w_ukv_h, small, g_kn_p, cos_t, sin_t)


def _attn_kernel(q_ref, kt_ref, v_ref, o_ref, *, tk):
    q = q_ref[...]
    S = kt_ref.shape[1]
    m = acc = None
    for j in range(S // tk):
        s = jnp.dot(q, kt_ref[:, j * tk:(j + 1) * tk], preferred_element_type=F32)
        mj = jnp.max(s, axis=1, keepdims=True)
        vj = v_ref[j * tk:(j + 1) * tk, :]
        if j == 0:
            m = mj
            acc = jnp.dot(jnp.exp2(s - m).astype(BF16), vj, preferred_element_type=F32)
        else:
            m_new = jnp.maximum(m, mj)
            acc = (jnp.exp2(m - m_new) * acc
                   + jnp.dot(jnp.exp2(s - m_new).astype(BF16), vj, preferred_element_type=F32))
            m = m_new
    o_ref[...] = (acc[:, 0:A_V_HD] / acc[:, A_V_HD:A_V_HD + 1]).astype(o_ref.dtype)


def _attention(q, kt, v, B, S):
    H, T, dq = q.shape
    tq, tk = min(ATT_TQ, S), min(ATT_TK, S)
    nq = S // tq
    return pl.pallas_call(
        functools.partial(_attn_kernel, tk=tk), grid=(B, H, nq),
        in_specs=[pl.BlockSpec((None, tq, dq), lambda b, h, i: (h, b * nq + i, 0)),
                  pl.BlockSpec((None, dq, S), lambda b, h, i: (h, 0, b)),
                  pl.BlockSpec((None, S, 2 * A_V_HD), lambda b, h, i: (h, b, 0))],
        out_specs=pl.BlockSpec((tq, A_V_HD), lambda b, h, i: (b * nq + i, h)),
        out_shape=jax.ShapeDtypeStruct((T, H * A_V_HD), BF16),
        compiler_params=_cparams(("parallel", "parallel", "parallel")), name="mla_attention")(
            q, kt, v)


def _merge_kernel(hm_ref, ha_ref, wm_ref, wa_ref, bm_ref, ba_ref, o_ref, wm_bf, wa_bf):
    _cache_bf16(wm_ref, wm_bf)
    _cache_bf16(wa_ref, wa_bf)
    ym = jnp.dot(hm_ref[...], wm_bf[...], preferred_element_type=F32)
    ya = jnp.dot(ha_ref[...], wa_bf[...], preferred_element_type=F32)
    o_ref[...] = (_sigmoid(bm_ref[...].astype(F32)) * ym
                  + _sigmoid(ba_ref[...].astype(F32)) * ya).astype(o_ref.dtype)


def _merge(hm, ha, w_bm, w_ba, z_a, br_off, tm=1024, tn=512):
    T, K = hm.shape
    D = w_bm.shape[1]
    tm, tn = min(tm, T), min(tn, D)
    ob = br_off // tn
    return pl.pallas_call(
        _merge_kernel, grid=(D // tn, T // tm),
        in_specs=[pl.BlockSpec((tm, K), lambda j, i: (i, 0)),
                  pl.BlockSpec((tm, K), lambda j, i: (i, 0)),
                  pl.BlockSpec((K, tn), lambda j, i: (0, j)),
                  pl.BlockSpec((K, tn), lambda j, i: (0, j)),
                  pl.BlockSpec((tm, tn), lambda j, i: (i, ob + j)),
                  pl.BlockSpec((tm, tn), lambda j, i: (i, ob + D // tn + j))],
        out_specs=pl.BlockSpec((tm, tn), lambda j, i: (i, j)),
        out_shape=jax.ShapeDtypeStruct((T, D), BF16),
        scratch_shapes=[pltpu.VMEM((K, tn), BF16), pltpu.VMEM((K, tn), BF16)],
        compiler_params=_cparams(("parallel", "arbitrary")), name="merge")(
            hm, ha, w_bm, w_ba, z_a, z_a)


def _mm_res_kernel(a_ref, w_ref, r_ref, o_ref, wbf_ref):
    _cache_bf16(w_ref, wbf_ref)
    o_ref[...] = r_ref[...] + jnp.dot(a_ref[...], wbf_ref[...], preferred_element_type=F32)


def _matmul_residual(a, b, res, tm=1024, tn=512):
    M, K = a.shape
    N = b.shape[1]
    tm, tn = min(tm, M), min(tn, N)
    return pl.pallas_call(
        _mm_res_kernel, grid=(N // tn, M // tm),
        in_specs=[pl.BlockSpec((tm, K), lambda j, i: (i, 0)),
                  pl.BlockSpec((K, tn), lambda j, i: (0, j)),
                  pl.BlockSpec((tm, tn), lambda j, i: (i, j))],
        out_specs=pl.BlockSpec((tm, tn), lambda j, i: (i, j)),
        out_shape=jax.ShapeDtypeStruct((M, N), F32),
        scratch_shapes=[pltpu.VMEM((K, tn), BF16)],
        compiler_params=_cparams(("parallel", "arbitrary")), name="out_proj")(a, b, res)


def _router_kernel(x_ref, g_ref, w_ref, b_ref, o_ref, xn3_ref):
    xf = _rms(x_ref[...], g_ref[...])
    nch = xf.shape[1] // (2 * LANES)
    tm = xf.shape[0]
    pitch = xn3_ref.shape[0] // tm
    for c in range(pitch):
        xn3_ref[pl.ds(c, tm, stride=pitch), :] = (
            _pack_bf16_pair(xf[:, 2 * c * LANES:(2 * c + 1) * LANES],
                            xf[:, (2 * c + 1) * LANES:(2 * c + 2) * LANES])
            if c < nch else jnp.zeros((tm, LANES), jnp.int32))
    xn = xf.astype(BF16)
    logits = jnp.dot(xn, w_ref[...], preferred_element_type=F32) + b_ref[...]
    lane = lax.broadcasted_iota(jnp.int32, logits.shape, 1)
    big = jnp.int32(1 << 20)
    lg = jnp.where(lane < N_GROUPS, logits, NEG)
    gmax = jnp.max(lg, axis=1, keepdims=True)
    g_idx = jnp.min(jnp.where(lg == gmax, lane, big), axis=1, keepdims=True)
    g_w = 1.0 / jnp.sum(jnp.exp(lg - gmax), axis=1, keepdims=True)
    lo = N_GROUPS + g_idx * E_PER_GROUP
    le = jnp.where((lane >= lo) & (lane < lo + E_PER_GROUP), logits, NEG)
    v1 = jnp.max(le, axis=1, keepdims=True)
    i1 = jnp.min(jnp.where(le == v1, lane, big), axis=1, keepdims=True)
    le2 = jnp.where(lane == i1, NEG, le)
    v2 = jnp.max(le2, axis=1, keepdims=True)
    i2 = jnp.min(jnp.where(le2 == v2, lane, big), axis=1, keepdims=True)
    e21 = jnp.exp(v2 - v1)
    w1 = g_w / (1.0 + e21)
    w2 = g_w * e21 / (1.0 + e21)
    o_ref[...] = jnp.where(
        lane == 0, (i1 - N_GROUPS).astype(F32),
        jnp.where(lane == 1, (i2 - N_GROUPS).astype(F32),
                  jnp.where(lane == 2, w1, jnp.where(lane == 3, w2, 0.0))))


def _router(x, g, w_r, b_r, tm=512):
    T, D = x.shape
    tm = min(tm, T)
    pitch = _row_pitch(D // (2 * LANES))
    return pl.pallas_call(
        _router_kernel, grid=(T // tm,),
        in_specs=[pl.BlockSpec((tm, D), lambda i: (i, 0)),
                  pl.BlockSpec((1, D), lambda i: (0, 0)),
                  pl.BlockSpec((D, LANES), lambda i: (0, 0)),
                  pl.BlockSpec((1, LANES), lambda i: (0, 0))],
        out_specs=[pl.BlockSpec((tm, LANES), lambda i: (i, 0)),
                   pl.BlockSpec((tm * pitch, LANES), lambda i: (i, 0))],
        out_shape=[jax.ShapeDtypeStruct((T, LANES), F32),
                   jax.ShapeDtypeStruct((T * pitch, LANES), jnp.int32)],
        compiler_params=_cparams(("parallel",)), name="router")(x, g.reshape(1, D), w_r, b_r)


def _pack_bf16_pair(a, b):
    ua = lax.bitcast_convert_type(a.astype(BF16).astype(F32), jnp.int32)
    ub = lax.bitcast_convert_type(b.astype(BF16).astype(F32), jnp.int32)
    return lax.shift_right_logical(ua, jnp.int32(16)) | (ub & jnp.int32(-65536))


def _unpack_bf16_pair(u):
    return [lax.bitcast_convert_type(lax.shift_left(u, jnp.int32(16)), F32),
            lax.bitcast_convert_type(u & jnp.int32(-65536), F32)]


def _row_pitch(nch):
    return (nch + 7) // 8 * 8 + 8


def _gather_rows(start_one, wait_one, n, unroll=8):
    def start():
        def body(r, carry):
            start_one(r)
            return carry
        lax.fori_loop(0, n, body, 0, unroll=unroll)

    def wait():
        def body(r, carry):
            wait_one(r)
            return carry
        lax.fori_loop(0, n, body, 0, unroll=unroll)
    return start, wait


def _weight_prefetch(i, nused, blk_e_ref, first_ref, par_ref, nxt_ref, copies):
    @pl.when(i == 0)
    def _():
        for c in copies(blk_e_ref[0], 0):
            c.start()

    @pl.when((i < nused) & (first_ref[i] == 1))
    def _():
        slot = par_ref[i]
        for c in copies(blk_e_ref[i], slot):
            c.wait()

        @pl.when(nxt_ref[i] >= 0)
        def _():
            for c in copies(nxt_ref[i], 1 - slot):
                c.start()


def _expert_up_kernel(blk_e_ref, nused_ref, first_ref, par_ref, nxt_ref, src_ref, tok_ref,
                      x_hbm, w1_hbm, w3_hbm, h_ref, xbuf, wb1, wb3, sem, wsem,
                      *, blk, n_assign, nch, pitch, kch):
    i = pl.program_id(0)
    nused = nused_ref[0]

    def row_copy(step, r, slot):
        idx = jnp.minimum(src_ref[step] + r, n_assign - 1)
        return pltpu.make_async_copy(
            x_hbm.at[pl.ds(pl.multiple_of(tok_ref[idx] * pitch, 8), nch), :],
            xbuf.at[slot, pl.ds(pl.multiple_of(r * pitch, 8), nch), :], sem.at[slot])

    def gather(step, slot):
        return _gather_rows(lambda r: row_copy(step, r, slot).start(),
                            lambda r: row_copy(step, r, slot).wait(), blk)

    def weight_copies(e, slot):
        return [pltpu.make_async_copy(w1_hbm.at[e], wb1.at[slot], wsem.at[0, slot]),
                pltpu.make_async_copy(w3_hbm.at[e], wb3.at[slot], wsem.at[1, slot])]

    @pl.when(i == 0)
    def _():
        gather(0, 0)[0]()

    @pl.when(i + 1 < nused)
    def _():
        gather(i + 1, (i + 1) % 2)[0]()

    _weight_prefetch(i, nused, blk_e_ref, first_ref, par_ref, nxt_ref, weight_copies)

    @pl.when(i < nused)
    def _():
        slot = i % 2
        wslot = par_ref[i]
        gather(i, slot)[1]()
        parts = []
        for c in range(nch):
            parts += _unpack_bf16_pair(xbuf[slot, pl.ds(c, blk, stride=pitch), :])
        xn = jnp.concatenate([t.astype(BF16) for t in parts], axis=1)
        h1 = h3 = None
        for k in range(xn.shape[1] // kch):
            xk = xn[:, k * kch:(k + 1) * kch]
            d1 = jnp.dot(xk, wb1[wslot, k * kch:(k + 1) * kch, :].astype(BF16),
                         preferred_element_type=F32)
            d3 = jnp.dot(xk, wb3[wslot, k * kch:(k + 1) * kch, :].astype(BF16),
                         preferred_element_type=F32)
            h1 = d1 if h1 is None else h1 + d1
            h3 = d3 if h3 is None else h3 + d3
        h_ref[...] = (h1 * _sigmoid(h1) * h3).astype(h_ref.dtype)

    @pl.when(i >= nused)
    def _():
        h_ref[...] = jnp.zeros_like(h_ref)


def _expert_down_kernel(blk_e_ref, nused_ref, first_ref, par_ref, nxt_ref, h_ref, w2_hbm, y_ref,
                        wb2, wsem, *, blk, nch, pitch, ych):
    i = pl.program_id(0)
    nused = nused_ref[0]

    def weight_copies(e, slot):
        return [pltpu.make_async_copy(w2_hbm.at[e], wb2.at[slot], wsem.at[slot])]

    _weight_prefetch(i, nused, blk_e_ref, first_ref, par_ref, nxt_ref, weight_copies)

    @pl.when(i < nused)
    def _():
        wslot = par_ref[i]
        hdn = h_ref[...]
        for j in range(nch * 2 * LANES // ych):
            yj = jnp.dot(hdn, wb2[wslot, :, j * ych:(j + 1) * ych].astype(BF16),
                         preferred_element_type=F32)
            for c in range(ych // (2 * LANES)):
                y_ref[pl.ds(j * (ych // (2 * LANES)) + c, blk, stride=pitch), :] = (
                    _pack_bf16_pair(yj[:, 2 * c * LANES:(2 * c + 1) * LANES],
                                    yj[:, (2 * c + 1) * LANES:(2 * c + 2) * LANES]))
        for c in range(nch, pitch):
            y_ref[pl.ds(c, blk, stride=pitch), :] = jnp.zeros((blk, LANES), jnp.int32)

    @pl.when(i >= nused)
    def _():
        y_ref[...] = jnp.zeros_like(y_ref)


def _experts(xn3, w1, w3, w2, blk_e, nused, src_start, tok_sorted, blk):
    E, D, De = w1.shape
    nch = D // (2 * LANES)
    pitch = _row_pitch(nch)
    nb = blk_e.shape[0]
    n_assign = tok_sorted.shape[0]
    b = jnp.arange(nb, dtype=jnp.int32)
    first = (b < nused[0]) & (blk_e != jnp.concatenate([jnp.full((1,), -1, jnp.int32), blk_e[:-1]]))
    par = ((jnp.cumsum(first.astype(jnp.int32)) - 1) % 2).astype(jnp.int32)
    nxt_first = lax.cummin(jnp.where(first, b, nb), axis=0, reverse=True)
    nxt_first = jnp.concatenate([nxt_first[1:], jnp.full((1,), nb, jnp.int32)])
    nxt = jnp.where(nxt_first < nb, blk_e[jnp.minimum(nxt_first, nb - 1)], -1).astype(jnp.int32)
    first = first.astype(jnp.int32)
    up_spec = pltpu.PrefetchScalarGridSpec(
        num_scalar_prefetch=7, grid=(nb,),
        in_specs=[pl.BlockSpec(memory_space=pl.ANY)] * 3,
        out_specs=pl.BlockSpec((blk, De), lambda i, *_: (i, 0)),
        scratch_shapes=[pltpu.VMEM((2, blk * pitch, LANES), jnp.int32),
                        pltpu.VMEM((2, D, De), F32), pltpu.VMEM((2, D, De), F32),
                        pltpu.SemaphoreType.DMA((2,)), pltpu.SemaphoreType.DMA((2, 2))])
    hdn = pl.pallas_call(
        functools.partial(_expert_up_kernel, blk=blk, n_assign=n_assign, nch=nch, pitch=pitch,
                          kch=min(1024, D)),
        grid_spec=up_spec,
        out_shape=jax.ShapeDtypeStruct((nb * blk, De), BF16),
        compiler_params=_cparams(("arbitrary",)), name="experts_up")(
            blk_e, nused, first, par, nxt, src_start, tok_sorted, xn3, w1, w3)
    down_spec = pltpu.PrefetchScalarGridSpec(
        num_scalar_prefetch=5, grid=(nb,),
        in_specs=[pl.BlockSpec((blk, De), lambda i, *_: (i, 0)),
                  pl.BlockSpec(memory_space=pl.ANY)],
        out_specs=pl.BlockSpec((blk * pitch, LANES), lambda i, *_: (i, 0)),
        scratch_shapes=[pltpu.VMEM((2, De, D), F32), pltpu.SemaphoreType.DMA((2,))])
    return pl.pallas_call(
        functools.partial(_expert_down_kernel, blk=blk, nch=nch, pitch=pitch, ych=min(1024, D)),
        grid_spec=down_spec,
        out_shape=jax.ShapeDtypeStruct((nb * blk * pitch, LANES), jnp.int32),
        compiler_params=_cparams(("arbitrary",)), name="experts_down")(
            blk_e, nused, first, par, nxt, hdn, w2)


def _combine_kernel(pos_ref, y_hbm, x_ref, rt_ref, g_ref, x2_ref, xn_ref, ybuf, sem, *, tc, nch,
                    pitch):
    i = pl.program_id(0)
    n = pl.num_programs(0)

    def row_copy(step, r, slot):
        return pltpu.make_async_copy(
            y_hbm.at[pl.ds(pl.multiple_of(pos_ref[step * 2 * tc + r] * pitch, 8), nch), :],
            ybuf.at[slot, pl.ds(pl.multiple_of(r * pitch, 8), nch), :], sem.at[slot])

    def gather(step, slot):
        return _gather_rows(lambda r: row_copy(step, r, slot).start(),
                            lambda r: row_copy(step, r, slot).wait(), 2 * tc)

    @pl.when(i == 0)
    def _():
        gather(0, 0)[0]()

    @pl.when(i + 1 < n)
    def _():
        gather(i + 1, (i + 1) % 2)[0]()

    slot = i % 2
    gather(i, slot)[1]()
    y0, y1 = [], []
    for c in range(nch):
        y0 += _unpack_bf16_pair(ybuf[slot, pl.ds(c, tc, stride=pitch), :])
        y1 += _unpack_bf16_pair(ybuf[slot, pl.ds(tc * pitch + c, tc, stride=pitch), :])
    y0 = jnp.concatenate(y0, axis=1)
    y1 = jnp.concatenate(y1, axis=1)
    x2 = x_ref[...] + rt_ref[:, 2:3] * y0 + rt_ref[:, 3:4] * y1
    x2_ref[...] = x2
    xn_ref[...] = _rms(x2, g_ref[...]).astype(xn_ref.dtype)


def _combine(y, x, route, g, pos, tc=256):
    T, D = x.shape
    tc = min(tc, T)
    nch = D // (2 * LANES)
    pitch = _row_pitch(nch)
    grid_spec = pltpu.PrefetchScalarGridSpec(
        num_scalar_prefetch=1, grid=(T // tc,),
        in_specs=[pl.BlockSpec(memory_space=pl.ANY),
                  pl.BlockSpec((tc, D), lambda i, ps: (i, 0)),
                  pl.BlockSpec((tc, LANES), lambda i, ps: (i, 0)),
                  pl.BlockSpec((1, D), lambda i, ps: (0, 0))],
        out_specs=[pl.BlockSpec((tc, D), lambda i, ps: (i, 0)),
                   pl.BlockSpec((tc, D), lambda i, ps: (i, 0))],
        scratch_shapes=[pltpu.VMEM((2, 2 * tc * pitch, LANES), jnp.int32),
                        pltpu.SemaphoreType.DMA((2,))])
    pos_tiled = pos.reshape(T // tc, tc, 2).transpose(0, 2, 1).reshape(-1)
    return pl.pallas_call(
        functools.partial(_combine_kernel, tc=tc, nch=nch, pitch=pitch), grid_spec=grid_spec,
        out_shape=[jax.ShapeDtypeStruct((T, D), F32), jax.ShapeDtypeStruct((T, D), BF16)],
        compiler_params=_cparams(("arbitrary",)), name="combine")(
            pos_tiled, y, x, route, g.reshape(1, D))


def _ple_kernel(xn_ref, wg_ref, p_ref, wp_ref, x_ref, o_ref, wg_bf, wp_bf):
    _cache_bf16(wg_ref, wg_bf)
    _cache_bf16(wp_ref, wp_bf)
    gate = _sigmoid(jnp.dot(xn_ref[...], wg_bf[...], preferred_element_type=F32))
    emb = jnp.dot(p_ref[...].astype(BF16), wp_bf[...], preferred_element_type=F32)
    o_ref[...] = x_ref[...] + gate * emb


def _ple(xn, w_pg, p, w_ple, x, tm=1024, tn=512):
    T, D = xn.shape
    Pd = p.shape[1]
    tm, tn = min(tm, T), min(tn, D)
    return pl.pallas_call(
        _ple_kernel, grid=(D // tn, T // tm),
        in_specs=[pl.BlockSpec((tm, D), lambda j, i: (i, 0)),
                  pl.BlockSpec((D, tn), lambda j, i: (0, j)),
                  pl.BlockSpec((tm, Pd), lambda j, i: (i, 0)),
                  pl.BlockSpec((Pd, tn), lambda j, i: (0, j)),
                  pl.BlockSpec((tm, tn), lambda j, i: (i, j))],
        out_specs=pl.BlockSpec((tm, tn), lambda j, i: (i, j)),
        out_shape=jax.ShapeDtypeStruct((T, D), F32),
        scratch_shapes=[pltpu.VMEM((D, tn), BF16), pltpu.VMEM((Pd, tn), BF16)],
        compiler_params=_cparams(("parallel", "arbitrary")), name="ple")(xn, w_pg, p, w_ple, x)


def _dispatch_tables(route, blk):
    T = route.shape[0]
    A = 2 * T
    eid = route[:, 0:2].astype(jnp.int32).reshape(-1)
    iota = jnp.arange(A, dtype=jnp.int32)
    experts = jnp.arange(N_EXPERTS, dtype=jnp.int32)
    _, order = lax.sort((eid, iota), num_keys=1)
    _, inv = lax.sort((order, iota), num_keys=1)
    onehot = eid[:, None] == experts[None, :]
    counts = jnp.sum(onehot, axis=0, dtype=jnp.int32)
    starts = jnp.cumsum(counts) - counts
    nblk = (counts + blk - 1) // blk
    bends = jnp.cumsum(nblk)
    bstarts = bends - nblk
    nb = (A + N_EXPERTS * blk) // blk
    b = jnp.arange(nb, dtype=jnp.int32)
    blk_e = jnp.minimum(jnp.sum(b[:, None] >= bends[None, :], axis=1),
                        N_EXPERTS - 1).astype(jnp.int32)
    shift = starts - bstarts * blk
    src_start = b * blk + jnp.sum(
        jnp.where(blk_e[:, None] == experts[None, :], shift[None, :], 0), axis=1)
    nused = bends[-1].astype(jnp.int32).reshape(1)
    pos = inv - jnp.sum(jnp.where(onehot, shift[None, :], 0), axis=1)
    return blk_e, nused, src_start.astype(jnp.int32), order // 2, pos.reshape(T, 2)


def _layer(x2d, p2d, pos_col, B, S, g_mix, w_in, b_mgate, g_mhead, g_cq, w_uq, g_ckv, w_ukv,
           g_qn, g_kn, w_bm, w_ba, w_out, g_ffn, w_rg, b_rg, w_re, b_re, w_e1, w_e3, w_e2,
           g_ple, w_pg, w_ple):
    T, D = x2d.shape
    H = M_HEADS
    mqk, mv = H * M_QK_HD, H * M_V_HD
    o_gate = 2 * mqk + 2 * mv
    o_cq = o_gate + 4 * H
    o_ckv = o_cq + Q_LORA
    o_kpe = o_ckv + KV_LORA
    o_br = o_kpe + A_ROPE
    w_t = jnp.swapaxes(w_in, 0, 1)
    w_br_t, w_mid_t = lax.optimization_barrier((w_t[o_br:], w_t[o_gate:o_br]))
    w_cq = w_mid_t[o_cq - o_gate:o_ckv - o_gate].T.astype(BF16)
    w_ckv = w_mid_t[o_ckv - o_gate:o_kpe - o_gate].T.astype(BF16)
    w_sm = jnp.concatenate(
        [w_mid_t[o_kpe - o_gate:], w_mid_t[:4 * H],
         jnp.zeros((LANES - A_ROPE - 4 * H, D), F32)], axis=0).T.astype(BF16)

    h = _rmsnorm(x2d, g_mix)
    z_a = _matmul_t(h, w_t, o_gate, BF16, name="inproj_mlstm")
    z_br = _matmul_t(h, w_br_t, 2 * D, BF16, name="inproj_gates")
    cq_n, ckv_n, small = _inproj_small(h, w_cq, w_ckv, w_sm, g_cq, g_ckv)

    g_col = small[:, A_ROPE:A_ROPE + 4 * H] + b_mgate[None, :]
    g_row = g_col.reshape(B, S, 4 * H).transpose(0, 2, 1)
    hb = _mlstm(z_a, g_col, g_row, B, S, reverse=True)
    hm = _mlstm(z_a, g_col, g_row, B, S, reverse=False, hb=hb, g_mhead=g_mhead)

    pad_q = A_QK_PAD - A_QK_HD
    w_uq_p = jnp.pad(w_uq.reshape(Q_LORA, A_HEADS, A_QK_HD).transpose(1, 0, 2),
                     ((0, 0), (0, 0), (0, pad_q))).astype(BF16)
    w_ukv_h = w_ukv.reshape(KV_LORA, A_HEADS, A_NOPE + A_V_HD).transpose(1, 0, 2).astype(BF16)
    g_qn_p = jnp.pad(g_qn, (0, pad_q)).reshape(1, A_QK_PAD)
    g_kn_p = jnp.pad(g_kn, (0, pad_q)).reshape(1, A_QK_PAD)
    inv = ROPE_THETA ** (-jnp.arange(0, A_ROPE, 2, dtype=F32) / A_ROPE)
    inv_row = jnp.concatenate([inv, inv, jnp.zeros((LANES - A_ROPE,), F32)]).reshape(1, LANES)
    cos_t, sin_t = _rope_tables(pos_col, inv_row)
    q = _q_prep(cq_n, w_uq_p, g_qn_p, cos_t, sin_t)
    kt, v = _kv_prep(ckv_n, w_ukv_h, small, g_kn_p, cos_t, sin_t)
    ha = _attention(q, kt, v, B, S)

    mix = _merge(hm, ha, w_bm, w_ba, z_br, 0)
    x1 = _matmul_residual(mix, w_out, x2d)

    w_r = jnp.concatenate([w_rg, w_re, jnp.zeros((D, LANES - N_GROUPS - N_EXPERTS), F32)],
                          axis=1).astype(BF16)
    b_r = jnp.concatenate([b_rg, b_re, jnp.zeros((LANES - N_GROUPS - N_EXPERTS,), F32)]
                          ).reshape(1, LANES)
    route, xn3 = _router(x1, g_ffn, w_r, b_r)
    blk = min(MOE_BLK, T)
    blk_e, nused, src_start, tok_sorted, pos = _dispatch_tables(route, blk)
    y = _experts(xn3, w_e1, w_e3, w_e2, blk_e, nused, src_start, tok_sorted, blk)
    x2, xn2 = _combine(y, x1, route, g_ple, pos)

    return _ple(xn2, w_pg, p2d, w_ple, x2)


def kernel(x, p, positions, g_mix, w_in, b_mgate, g_mhead, g_cq, w_uq, g_ckv, w_ukv, g_qn, g_kn,
           w_bm, w_ba, w_out, g_ffn, w_rg, b_rg, w_re, b_re, w_e1, w_e3, w_e2, g_ple, w_pg, w_ple):
    B, S, D = x.shape
    T = B * S
    x2d = x.reshape(T, D)
    pos_col = positions.reshape(T, 1).astype(jnp.int32)
    for l in range(p.shape[0]):
        x2d = _layer(x2d, p[l].reshape(T, -1), pos_col, B, S, g_mix[l], w_in[l], b_mgate[l],
                     g_mhead[l], g_cq[l], w_uq[l], g_ckv[l], w_ukv[l], g_qn[l], g_kn[l],
                     w_bm[l], w_ba[l], w_out[l], g_ffn[l], w_rg[l], b_rg[l], w_re[l], b_re[l],
                     w_e1[l], w_e3[l], w_e2[l], g_ple[l], w_pg[l], w_ple[l])
    return x2d.reshape(B, S, D)
```

```python
import functools
import math

import jax
import jax.numpy as jnp
from jax import lax
from jax.experimental import pallas as pl
from jax.experimental.pallas import tpu as pltpu

F32 = jnp.float32
BF16 = jnp.bfloat16

EPS = 1e-6
M_HEADS = 8
M_QK_HD = 128
M_V_HD = 256
A_HEADS = 16
A_NOPE = 128
A_ROPE = 64
A_QK_HD = A_NOPE + A_ROPE
A_V_HD = 128
A_QK_PAD = 256
Q_LORA = 1024
KV_LORA = 512
ROPE_THETA = 10000.0
N_GROUPS = 8
E_PER_GROUP = 8
N_EXPERTS = N_GROUPS * E_PER_GROUP
D_EXPERT = 512
LANES = 128

M_CHUNK = 256
MOE_BLK = 256
ATT_TQ = 1024
ATT_TK = 256
VMEM_LIMIT = 56 * 1024 * 1024
NEG = -1e30
LOG2E = 1.4426950408889634


def _cparams(sem):
    return pltpu.CompilerParams(dimension_semantics=sem, vmem_limit_bytes=VMEM_LIMIT)


def _sigmoid(x):
    return 1.0 / (1.0 + jnp.exp(-x))


def _log_sigmoid(x):
    return jnp.minimum(x, 0.0) - jnp.log(1.0 + jnp.exp(-jnp.abs(x)))


def _rms(xf, g):
    return xf * lax.rsqrt(jnp.mean(xf * xf, axis=-1, keepdims=True) + EPS) * g


def _norm_kernel(x_ref, g_ref, o_ref):
    o_ref[...] = _rms(x_ref[...], g_ref[...]).astype(o_ref.dtype)


def _rmsnorm(x, g, tm=512):
    M, D = x.shape
    tm = min(tm, M)
    return pl.pallas_call(
        _norm_kernel, grid=(M // tm,),
        in_specs=[pl.BlockSpec((tm, D), lambda i: (i, 0)),
                  pl.BlockSpec((1, D), lambda i: (0, 0))],
        out_specs=pl.BlockSpec((tm, D), lambda i: (i, 0)),
        out_shape=jax.ShapeDtypeStruct((M, D), BF16),
        compiler_params=_cparams(("parallel",)), name="rmsnorm")(x, g.reshape(1, D))


def _cache_bf16(w_ref, wbf_ref):
    @pl.when(pl.program_id(1) == 0)
    def _():
        wbf_ref[...] = w_ref[...].astype(BF16)


def _mm_t_kernel(a_ref, wt_hbm, o_ref, wst, wbf_ref, sem, *, row0, tn):
    j = pl.program_id(0)
    i = pl.program_id(1)

    def wcopy(jj):
        return pltpu.make_async_copy(
            wt_hbm.at[pl.ds(pl.multiple_of(row0 + jj * tn, 8), tn), :], wst, sem)

    @pl.when((j == 0) & (i == 0))
    def _():
        wcopy(0).start()

    @pl.when(i == 0)
    def _():
        wcopy(j).wait()
        wbf_ref[...] = wst[...].astype(BF16)

        @pl.when(j + 1 < pl.num_programs(0))
        def _():
            wcopy(j + 1).start()

    o_ref[...] = lax.dot_general(a_ref[...], wbf_ref[...], (((1,), (1,)), ((), ())),
                                 preferred_element_type=F32).astype(o_ref.dtype)


def _matmul_t(a, w_t, row0, n_cols, out_dtype, tm=1024, tn=1024, name="matmul"):
    M, K = a.shape
    tm, tn = min(tm, M), min(tn, n_cols)
    return pl.pallas_call(
        functools.partial(_mm_t_kernel, row0=row0, tn=tn), grid=(n_cols // tn, M // tm),
        in_specs=[pl.BlockSpec((tm, K), lambda j, i: (i, 0)),
                  pl.BlockSpec(memory_space=pl.ANY)],
        out_specs=pl.BlockSpec((tm, tn), lambda j, i: (i, j)),
        out_shape=jax.ShapeDtypeStruct((M, n_cols), out_dtype),
        scratch_shapes=[pltpu.VMEM((tn, K), F32), pltpu.VMEM((tn, K), BF16),
                        pltpu.SemaphoreType.DMA(())],
        compiler_params=_cparams(("arbitrary", "arbitrary")), name=name)(a, w_t)


def _inproj_small_kernel(h_ref, wq_ref, wkv_ref, ws_ref, gq_ref, gkv_ref,
                         cq_ref, ckv_ref, sm_ref):
    h = h_ref[...]
    cq = jnp.dot(h, wq_ref[...], preferred_element_type=F32)
    cq_ref[...] = _rms(cq, gq_ref[...]).astype(cq_ref.dtype)
    ckv = jnp.dot(h, wkv_ref[...], preferred_element_type=F32)
    ckv_ref[...] = _rms(ckv, gkv_ref[...]).astype(ckv_ref.dtype)
    sm_ref[...] = jnp.dot(h, ws_ref[...], preferred_element_type=F32)


def _inproj_small(h, w_cq, w_ckv, w_sm, g_cq, g_ckv, tm=512):
    M, D = h.shape
    tm = min(tm, M)
    nq, nkv, ns = w_cq.shape[1], w_ckv.shape[1], w_sm.shape[1]
    full = lambda n: pl.BlockSpec((D, n), lambda i: (0, 0))
    row = lambda n: pl.BlockSpec((tm, n), lambda i: (i, 0))
    return pl.pallas_call(
        _inproj_small_kernel, grid=(M // tm,),
        in_specs=[row(D), full(nq), full(nkv), full(ns),
                  pl.BlockSpec((1, nq), lambda i: (0, 0)),
                  pl.BlockSpec((1, nkv), lambda i: (0, 0))],
        out_specs=[row(nq), row(nkv), row(ns)],
        out_shape=[jax.ShapeDtypeStruct((M, nq), BF16),
                   jax.ShapeDtypeStruct((M, nkv), BF16),
                   jax.ShapeDtypeStruct((M, ns), F32)],
        compiler_params=_cparams(("parallel",)), name="inproj_small")(
            h, w_cq, w_ckv, w_sm, g_cq.reshape(1, nq), g_ckv.reshape(1, nkv))


def _mlstm_kernel(*refs, reverse, finish, L, goff):
    if finish:
        (q_ref, k_ref, v_ref, gc_ref, gr_ref, hb_ref, o_ref, gm_ref,
         out_ref, C_ref, n_ref, m_ref) = refs
    else:
        q_ref, k_ref, v_ref, gc_ref, gr_ref, out_ref, C_ref, n_ref, m_ref = refs
    H, dk, dv = M_HEADS, M_QK_HD, M_V_HD
    log_scale = -0.5 * math.log(dk)

    @pl.when(pl.program_id(1) == 0)
    def _():
        C_ref[...] = jnp.zeros_like(C_ref)
        n_ref[...] = jnp.zeros_like(n_ref)
        m_ref[...] = jnp.zeros_like(m_ref)

    t_idx = lax.broadcasted_iota(jnp.int32, (L, L), 0)
    s_idx = lax.broadcasted_iota(jnp.int32, (L, L), 1)
    if reverse:
        mask, mask_t = s_idx >= t_idx, t_idx >= s_idx
    else:
        mask, mask_t = s_idx <= t_idx, t_idx <= s_idx

    hs = range(H)
    gates = []
    for h in hs:
        i_col = gc_ref[:, goff + h:goff + h + 1]
        f_col = gc_ref[:, goff + H + h:goff + H + h + 1]
        i_row = gr_ref[goff + h:goff + h + 1, :]
        f_row = gr_ref[goff + H + h:goff + H + h + 1, :]
        lf_col = _log_sigmoid(f_col)
        lf_row = _log_sigmoid(f_row)
        b_col = jnp.sum(jnp.where(mask, lf_row, 0.0), axis=1, keepdims=True)
        b_row = jnp.sum(jnp.where(mask_t, lf_col, 0.0), axis=0, keepdims=True)
        g = jnp.sum(lf_row, axis=1, keepdims=True)
        gates.append((i_col, i_row, b_col, b_row, g, m_ref[h][:, 0:1]))

    stab = []
    for h in hs:
        i_col, i_row, b_col, b_row, g, m_prev = gates[h]
        dmat = jnp.where(mask, b_col - b_row + i_row, NEG)
        a = b_col + m_prev
        m_t = jnp.maximum(a, jnp.max(dmat, axis=1, keepdims=True))
        stab.append((dmat, a, m_t))

    qs = [q_ref[:, h * dk:(h + 1) * dk] for h in hs]
    ks = [k_ref[:, h * dk:(h + 1) * dk] for h in hs]
    vs = [v_ref[:, h * dv:(h + 1) * dv] for h in hs]
    qks = [lax.dot_general(qs[h], ks[h], (((1,), (1,)), ((), ())),
                           preferred_element_type=F32) for h in hs]

    intra = []
    for h in hs:
        dmat, a, m_t = stab[h]
        m_s = m_t - log_scale
        intra.append((jnp.exp(dmat - m_s) * qks[h], jnp.exp(a - m_s)))

    c_states = [C_ref[h] for h in hs]
    n_states = [n_ref[h] for h in hs]
    outs = []
    for h in hs:
        w_intra, w_inter = intra[h]
        q_c = jnp.dot(qs[h], c_states[h].astype(BF16), preferred_element_type=F32)
        num = jnp.dot(w_intra.astype(BF16), vs[h], preferred_element_type=F32) + w_inter * q_c
        q_n = jnp.sum(qs[h].astype(F32) * n_states[h], axis=1, keepdims=True)
        den = jnp.sum(w_intra, axis=1, keepdims=True) + w_inter * q_n
        outs.append(num / jnp.maximum(jnp.abs(den), jnp.exp(-stab[h][2])))

    for h in hs:
        i_col, i_row, b_col, b_row, g, m_prev = gates[h]
        w_col = g - b_col + i_col
        m_new = jnp.maximum(g + m_prev, jnp.max(w_col, axis=0, keepdims=True))
        decay = jnp.exp(g + m_prev - m_new)
        kw = ks[h].astype(F32) * jnp.exp(w_col - m_new)
        C_ref[h] = decay * c_states[h] + lax.dot_general(
            kw.astype(BF16), vs[h], (((0,), (0,)), ((), ())), preferred_element_type=F32)
        n_ref[h] = decay * n_states[h] + jnp.sum(kw, axis=0, keepdims=True)
        m_ref[h] = jnp.broadcast_to(m_new, (1, LANES))

    for h in hs:
        if finish:
            tot = outs[h] + hb_ref[:, h * dv:(h + 1) * dv]
            y = _rms(tot, gm_ref[h:h + 1, :])
            gate = _sigmoid(o_ref[:, h * dv:(h + 1) * dv].astype(F32))
            out_ref[:, h * dv:(h + 1) * dv] = (y * gate).astype(out_ref.dtype)
        else:
            out_ref[:, h * dv:(h + 1) * dv] = outs[h]


def _mlstm(z_a, g_col, g_row, B, S, *, reverse, hb=None, g_mhead=None):
    H, dk, dv = M_HEADS, M_QK_HD, M_V_HD
    L = min(M_CHUNK, S)
    nc = S // L
    T = B * S
    finish = not reverse
    if reverse:
        cidx = lambda b, c: b * nc + (nc - 1 - c)
        ridx = lambda b, c: nc - 1 - c
    else:
        cidx = lambda b, c: b * nc + c
        ridx = lambda b, c: c
    mqk, mv = H * dk, H * dv
    in_specs = [
        pl.BlockSpec((L, mqk), lambda b, c: (cidx(b, c), 0)),
        pl.BlockSpec((L, mqk), lambda b, c: (cidx(b, c), 1)),
        pl.BlockSpec((L, mv), lambda b, c: (cidx(b, c), 1)),
        pl.BlockSpec((L, 4 * H), lambda b, c: (cidx(b, c), 0)),
        pl.BlockSpec((None, 4 * H, L), lambda b, c: (b, 0, ridx(b, c))),
    ]
    args = [z_a, z_a, z_a, g_col, g_row]
    if finish:
        in_specs += [
            pl.BlockSpec((L, mv), lambda b, c: (cidx(b, c), 0)),
            pl.BlockSpec((L, mv), lambda b, c: (cidx(b, c), 2)),
            pl.BlockSpec((H, dv), lambda b, c: (0, 0)),
        ]
        args += [hb, z_a, g_mhead.reshape(H, dv)]
    kern = functools.partial(_mlstm_kernel, reverse=reverse, finish=finish, L=L,
                             goff=2 * H if reverse else 0)
    return pl.pallas_call(
        kern, grid=(B, nc), in_specs=in_specs,
        out_specs=pl.BlockSpec((L, mv), lambda b, c: (cidx(b, c), 0)),
        out_shape=jax.ShapeDtypeStruct((T, mv), BF16 if finish else F32),
        scratch_shapes=[pltpu.VMEM((H, dk, dv), F32), pltpu.VMEM((H, 1, dk), F32),
                        pltpu.VMEM((H, 1, LANES), F32)],
        compiler_params=_cparams(("parallel", "arbitrary")),
        name="mlstm_bwd" if reverse else "mlstm_fwd")(*args)


def _rope_table_kernel(pos_ref, inv_ref, cos_ref, sin_ref):
    ang = pos_ref[...].astype(F32) * inv_ref[...]
    lane = lax.broadcasted_iota(jnp.int32, ang.shape, 1)
    cos_ref[...] = jnp.cos(ang)
    sin_ref[...] = jnp.where(lane < A_ROPE // 2, -jnp.sin(ang), jnp.sin(ang))


def _rope_tables(pos_col, inv_row, tm=1024):
    T = pos_col.shape[0]
    tm = min(tm, T)
    return pl.pallas_call(
        _rope_table_kernel, grid=(T // tm,),
        in_specs=[pl.BlockSpec((tm, 1), lambda i: (i, 0)),
                  pl.BlockSpec((1, LANES), lambda i: (0, 0))],
        out_specs=[pl.BlockSpec((tm, LANES), lambda i: (i, 0))] * 2,
        out_shape=[jax.ShapeDtypeStruct((T, LANES), F32)] * 2,
        compiler_params=_cparams(("parallel",)), name="rope_tables")(pos_col, inv_row)


def _rope(r, cos_ref, sin_ref):
    lane = lax.broadcasted_iota(jnp.int32, r.shape, 1)
    half = A_ROPE // 2
    swapped = jnp.where(lane < half, pltpu.roll(r, LANES - half, 1), pltpu.roll(r, half, 1))
    return r * cos_ref[...] + swapped * sin_ref[...]


PREP_HEADS = 8


def _q_prep_kernel(c_ref, w_ref, g_ref, cos_ref, sin_ref, o_ref):
    c = c_ref[...]
    hs = range(w_ref.shape[0])
    accs = [jnp.dot(c, w_ref[h], preferred_element_type=F32) for h in hs]
    rs = [lax.rsqrt(jnp.sum(a * a, axis=1, keepdims=True) / A_QK_HD + EPS) for a in accs]
    gain = g_ref[...] * (A_QK_HD ** -0.5 * LOG2E)
    ys = [accs[h] * rs[h] * gain for h in hs]
    for h in hs:
        o_ref[h, :, 0:A_NOPE] = ys[h][:, 0:A_NOPE].astype(o_ref.dtype)
    for h in hs:
        o_ref[h, :, A_NOPE:] = _rope(ys[h][:, A_NOPE:], cos_ref, sin_ref).astype(o_ref.dtype)


def _q_prep(cq_n, w_uq_p, g_qn_p, cos_t, sin_t, tm=1024):
    T, R = cq_n.shape
    tm = min(tm, T)
    H, hb = A_HEADS, PREP_HEADS
    return pl.pallas_call(
        _q_prep_kernel, grid=(T // tm, H // hb),
        in_specs=[pl.BlockSpec((tm, R), lambda i, h: (i, 0)),
                  pl.BlockSpec((hb, R, A_QK_PAD), lambda i, h: (h, 0, 0)),
                  pl.BlockSpec((1, A_QK_PAD), lambda i, h: (0, 0)),
                  pl.BlockSpec((tm, LANES), lambda i, h: (i, 0)),
                  pl.BlockSpec((tm, LANES), lambda i, h: (i, 0))],
        out_specs=pl.BlockSpec((hb, tm, A_QK_PAD), lambda i, h: (h, i, 0)),
        out_shape=jax.ShapeDtypeStruct((H, T, A_QK_PAD), BF16),
        compiler_params=_cparams(("parallel", "parallel")), name="mla_q_prep")(
            cq_n, w_uq_p, g_qn_p, cos_t, sin_t)


def _kv_prep_kernel(c_ref, w_ref, sm_ref, g_ref, cos_ref, sin_ref, kt_ref, v_ref):
    c = c_ref[...]
    lane = lax.broadcasted_iota(jnp.int32, sm_ref.shape, 1)
    k_pe = jnp.where(lane < A_ROPE, sm_ref[...], 0.0)
    ss_pe = jnp.sum(k_pe * k_pe, axis=1, keepdims=True)
    hs = range(w_ref.shape[0])
    accs = [jnp.dot(c, w_ref[h], preferred_element_type=F32) for h in hs]
    for h in hs:
        v_ref[h, :, 0:A_V_HD] = accs[h][:, A_NOPE:].astype(v_ref.dtype)
        v_ref[h, :, A_V_HD:] = jnp.ones((c.shape[0], A_V_HD), v_ref.dtype)
    rs = [lax.rsqrt((jnp.sum(a[:, 0:A_NOPE] * a[:, 0:A_NOPE], axis=1, keepdims=True) + ss_pe)
                    / A_QK_HD + EPS) for a in accs]
    kns = [accs[h][:, 0:A_NOPE] * rs[h] * g_ref[:, 0:A_NOPE] for h in hs]
    krs = [_rope(k_pe * rs[h] * g_ref[:, A_NOPE:], cos_ref, sin_ref) for h in hs]
    for h in hs:
        kt_ref[h, 0:A_NOPE, :] = kns[h].T.astype(kt_ref.dtype)
    for h in hs:
        kt_ref[h, A_NOPE:, :] = krs[h].T.astype(kt_ref.dtype)


def _kv_prep(ckv_n, w_ukv_h, small, g_kn_p, cos_t, sin_t, tm=1024):
    T, R = ckv_n.shape
    tm = min(tm, T)
    H, hb = A_HEADS, PREP_HEADS
    return pl.pallas_call(
        _kv_prep_kernel, grid=(T // tm, H // hb),
        in_specs=[pl.BlockSpec((tm, R), lambda i, h: (i, 0)),
                  pl.BlockSpec((hb, R, A_NOPE + A_V_HD), lambda i, h: (h, 0, 0)),
                  pl.BlockSpec((tm, LANES), lambda i, h: (i, 0)),
                  pl.BlockSpec((1, A_QK_PAD), lambda i, h: (0, 0)),
                  pl.BlockSpec((tm, LANES), lambda i, h: (i, 0)),
                  pl.BlockSpec((tm, LANES), lambda i, h: (i, 0))],
        out_specs=[pl.BlockSpec((hb, A_QK_PAD, tm), lambda i, h: (h, 0, i)),
                   pl.BlockSpec((hb, tm, 2 * A_V_HD), lambda i, h: (h, i, 0))],
        out_shape=[jax.ShapeDtypeStruct((H, A_QK_PAD, T), BF16),
                   jax.ShapeDtypeStruct((H, T, 2 * A_V_HD), BF16)],
        compiler_params=_cparams(("parallel", "parallel")), name="mla_kv_prep")(
            ckv_n, w_ukv_h, small, g_kn_p, cos_t, sin_t)


def _attn_kernel(q_ref, kt_ref, v_ref, o_ref, *, tk):
    q = q_ref[...]
    S = kt_ref.shape[1]
    m = acc = None
    for j in range(S // tk):
        s = jnp.dot(q, kt_ref[:, j * tk:(j + 1) * tk], preferred_element_type=F32)
        mj = jnp.max(s, axis=1, keepdims=True)
        vj = v_ref[j * tk:(j + 1) * tk, :]
        if j == 0:
            m = mj
            acc = jnp.dot(jnp.exp2(s - m).astype(BF16), vj, preferred_element_type=F32)
        else:
            m_new = jnp.maximum(m, mj)
            acc = (jnp.exp2(m - m_new) * acc
                   + jnp.dot(jnp.exp2(s - m_new).astype(BF16), vj, preferred_element_type=F32))
            m = m_new
    o_ref[...] = (acc[:, 0:A_V_HD] / acc[:, A_V_HD:A_V_HD + 1]).astype(o_ref.dtype)


def _attention(q, kt, v, B, S):
    H, T, dq = q.shape
    tq, tk = min(ATT_TQ, S), min(ATT_TK, S)
    nq = S // tq
    return pl.pallas_call(
        functools.partial(_attn_kernel, tk=tk), grid=(B, H, nq),
        in_specs=[pl.BlockSpec((None, tq, dq), lambda b, h, i: (h, b * nq + i, 0)),
                  pl.BlockSpec((None, dq, S), lambda b, h, i: (h, 0, b)),
                  pl.BlockSpec((None, S, 2 * A_V_HD), lambda b, h, i: (h, b, 0))],
        out_specs=pl.BlockSpec((tq, A_V_HD), lambda b, h, i: (b * nq + i, h)),
        out_shape=jax.ShapeDtypeStruct((T, H * A_V_HD), BF16),
        compiler_params=_cparams(("parallel", "parallel", "parallel")), name="mla_attention")(
            q, kt, v)


def _merge_kernel(hm_ref, ha_ref, wm_ref, wa_ref, bm_ref, ba_ref, o_ref, wm_bf, wa_bf):
    _cache_bf16(wm_ref, wm_bf)
    _cache_bf16(wa_ref, wa_bf)
    ym = jnp.dot(hm_ref[...], wm_bf[...], preferred_element_type=F32)
    ya = jnp.dot(ha_ref[...], wa_bf[...], preferred_element_type=F32)
    o_ref[...] = (_sigmoid(bm_ref[...].astype(F32)) * ym
                  + _sigmoid(ba_ref[...].astype(F32)) * ya).astype(o_ref.dtype)


def _merge(hm, ha, w_bm, w_ba, z_a, br_off, tm=1024, tn=512):
    T, K = hm.shape
    D = w_bm.shape[1]
    tm, tn = min(tm, T), min(tn, D)
    ob = br_off // tn
    return pl.pallas_call(
        _merge_kernel, grid=(D // tn, T // tm),
        in_specs=[pl.BlockSpec((tm, K), lambda j, i: (i, 0)),
                  pl.BlockSpec((tm, K), lambda j, i: (i, 0)),
                  pl.BlockSpec((K, tn), lambda j, i: (0, j)),
                  pl.BlockSpec((K, tn), lambda j, i: (0, j)),
                  pl.BlockSpec((tm, tn), lambda j, i: (i, ob + j)),
                  pl.BlockSpec((tm, tn), lambda j, i: (i, ob + D // tn + j))],
        out_specs=pl.BlockSpec((tm, tn), lambda j, i: (i, j)),
        out_shape=jax.ShapeDtypeStruct((T, D), BF16),
        scratch_shapes=[pltpu.VMEM((K, tn), BF16), pltpu.VMEM((K, tn), BF16)],
        compiler_params=_cparams(("parallel", "arbitrary")), name="merge")(
            hm, ha, w_bm, w_ba, z_a, z_a)


def _mm_res_kernel(a_ref, w_ref, r_ref, o_ref, wbf_ref):
    _cache_bf16(w_ref, wbf_ref)
    o_ref[...] = r_ref[...] + jnp.dot(a_ref[...], wbf_ref[...], preferred_element_type=F32)


def _matmul_residual(a, b, res, tm=1024, tn=512):
    M, K = a.shape
    N = b.shape[1]
    tm, tn = min(tm, M), min(tn, N)
    return pl.pallas_call(
        _mm_res_kernel, grid=(N // tn, M // tm),
        in_specs=[pl.BlockSpec((tm, K), lambda j, i: (i, 0)),
                  pl.BlockSpec((K, tn), lambda j, i: (0, j)),
                  pl.BlockSpec((tm, tn), lambda j, i: (i, j))],
        out_specs=pl.BlockSpec((tm, tn), lambda j, i: (i, j)),
        out_shape=jax.ShapeDtypeStruct((M, N), F32),
        scratch_shapes=[pltpu.VMEM((K, tn), BF16)],
        compiler_params=_cparams(("parallel", "arbitrary")), name="out_proj")(a, b, res)


def _router_kernel(x_ref, g_ref, w_ref, b_ref, o_ref, xn3_ref):
    xf = _rms(x_ref[...], g_ref[...])
    nch = xf.shape[1] // (2 * LANES)
    tm = xf.shape[0]
    pitch = xn3_ref.shape[0] // tm
    for c in range(pitch):
        xn3_ref[pl.ds(c, tm, stride=pitch), :] = (
            _pack_bf16_pair(xf[:, 2 * c * LANES:(2 * c + 1) * LANES],
                            xf[:, (2 * c + 1) * LANES:(2 * c + 2) * LANES])
            if c < nch else jnp.zeros((tm, LANES), jnp.int32))
    xn = xf.astype(BF16)
    logits = jnp.dot(xn, w_ref[...], preferred_element_type=F32) + b_ref[...]
    lane = lax.broadcasted_iota(jnp.int32, logits.shape, 1)
    big = jnp.int32(1 << 20)
    lg = jnp.where(lane < N_GROUPS, logits, NEG)
    gmax = jnp.max(lg, axis=1, keepdims=True)
    g_idx = jnp.min(jnp.where(lg == gmax, lane, big), axis=1, keepdims=True)
    g_w = 1.0 / jnp.sum(jnp.exp(lg - gmax), axis=1, keepdims=True)
    lo = N_GROUPS + g_idx * E_PER_GROUP
    le = jnp.where((lane >= lo) & (lane < lo + E_PER_GROUP), logits, NEG)
    v1 = jnp.max(le, axis=1, keepdims=True)
    i1 = jnp.min(jnp.where(le == v1, lane, big), axis=1, keepdims=True)
    le2 = jnp.where(lane == i1, NEG, le)
    v2 = jnp.max(le2, axis=1, keepdims=True)
    i2 = jnp.min(jnp.where(le2 == v2, lane, big), axis=1, keepdims=True)
    e21 = jnp.exp(v2 - v1)
    w1 = g_w / (1.0 + e21)
    w2 = g_w * e21 / (1.0 + e21)
    o_ref[...] = jnp.where(
        lane == 0, (i1 - N_GROUPS).astype(F32),
        jnp.where(lane == 1, (i2 - N_GROUPS).astype(F32),
                  jnp.where(lane == 2, w1, jnp.where(lane == 3, w2, 0.0))))


def _router(x, g, w_r, b_r, tm=512):
    T, D = x.shape
    tm = min(tm, T)
    pitch = _row_pitch(D // (2 * LANES))
    return pl.pallas_call(
        _router_kernel, grid=(T // tm,),
        in_specs=[pl.BlockSpec((tm, D), lambda i: (i, 0)),
                  pl.BlockSpec((1, D), lambda i: (0, 0)),
                  pl.BlockSpec((D, LANES), lambda i: (0, 0)),
                  pl.BlockSpec((1, LANES), lambda i: (0, 0))],
        out_specs=[pl.BlockSpec((tm, LANES), lambda i: (i, 0)),
                   pl.BlockSpec((tm * pitch, LANES), lambda i: (i, 0))],
        out_shape=[jax.ShapeDtypeStruct((T, LANES), F32),
                   jax.ShapeDtypeStruct((T * pitch, LANES), jnp.int32)],
        compiler_params=_cparams(("parallel",)), name="router")(x, g.reshape(1, D), w_r, b_r)


def _pack_bf16_pair(a, b):
    ua = lax.bitcast_convert_type(a.astype(BF16).astype(F32), jnp.int32)
    ub = lax.bitcast_convert_type(b.astype(BF16).astype(F32), jnp.int32)
    return lax.shift_right_logical(ua, jnp.int32(16)) | (ub & jnp.int32(-65536))


def _unpack_bf16_pair(u):
    return [lax.bitcast_convert_type(lax.shift_left(u, jnp.int32(16)), F32),
            lax.bitcast_convert_type(u & jnp.int32(-65536), F32)]


def _row_pitch(nch):
    return (nch + 7) // 8 * 8 + 8


def _gather_rows(start_one, wait_one, n, unroll=8):
    def start():
        def body(r, carry):
            start_one(r)
            return carry
        lax.fori_loop(0, n, body, 0, unroll=unroll)

    def wait():
        def body(r, carry):
            wait_one(r)
            return carry
        lax.fori_loop(0, n, body, 0, unroll=unroll)
    return start, wait


def _weight_prefetch(i, nused, blk_e_ref, first_ref, par_ref, nxt_ref, copies):
    @pl.when(i == 0)
    def _():
        for c in copies(blk_e_ref[0], 0):
            c.start()

    @pl.when((i < nused) & (first_ref[i] == 1))
    def _():
        slot = par_ref[i]
        for c in copies(blk_e_ref[i], slot):
            c.wait()

        @pl.when(nxt_ref[i] >= 0)
        def _():
            for c in copies(nxt_ref[i], 1 - slot):
                c.start()


def _expert_up_kernel(blk_e_ref, nused_ref, first_ref, par_ref, nxt_ref, src_ref, tok_ref,
                      x_hbm, w1_hbm, w3_hbm, h_ref, xbuf, wb1, wb3, sem, wsem,
                      *, blk, n_assign, nch, pitch, kch):
    i = pl.program_id(0)
    nused = nused_ref[0]

    def row_copy(step, r, slot):
        idx = jnp.minimum(src_ref[step] + r, n_assign - 1)
        return pltpu.make_async_copy(
            x_hbm.at[pl.ds(pl.multiple_of(tok_ref[idx] * pitch, 8), nch), :],
            xbuf.at[slot, pl.ds(pl.multiple_of(r * pitch, 8), nch), :], sem.at[slot])

    def gather(step, slot):
        return _gather_rows(lambda r: row_copy(step, r, slot).start(),
                            lambda r: row_copy(step, r, slot).wait(), blk)

    def weight_copies(e, slot):
        return [pltpu.make_async_copy(w1_hbm.at[e], wb1.at[slot], wsem.at[0, slot]),
                pltpu.make_async_copy(w3_hbm.at[e], wb3.at[slot], wsem.at[1, slot])]

    @pl.when(i == 0)
    def _():
        gather(0, 0)[0]()

    @pl.when(i + 1 < nused)
    def _():
        gather(i + 1, (i + 1) % 2)[0]()

    _weight_prefetch(i, nused, blk_e_ref, first_ref, par_ref, nxt_ref, weight_copies)

    @pl.when(i < nused)
    def _():
        slot = i % 2
        wslot = par_ref[i]
        gather(i, slot)[1]()
        parts = []
        for c in range(nch):
            parts += _unpack_bf16_pair(xbuf[slot, pl.ds(c, blk, stride=pitch), :])
        xn = jnp.concatenate([t.astype(BF16) for t in parts], axis=1)
        h1 = h3 = None
        for k in range(xn.shape[1] // kch):
            xk = xn[:, k * kch:(k + 1) * kch]
            d1 = jnp.dot(xk, wb1[wslot, k * kch:(k + 1) * kch, :].astype(BF16),
                         preferred_element_type=F32)
            d3 = jnp.dot(xk, wb3[wslot, k * kch:(k + 1) * kch, :].astype(BF16),
                         preferred_element_type=F32)
            h1 = d1 if h1 is None else h1 + d1
            h3 = d3 if h3 is None else h3 + d3
        h_ref[...] = (h1 * _sigmoid(h1) * h3).astype(h_ref.dtype)

    @pl.when(i >= nused)
    def _():
        h_ref[...] = jnp.zeros_like(h_ref)


def _expert_down_kernel(blk_e_ref, nused_ref, first_ref, par_ref, nxt_ref, h_ref, w2_hbm, y_ref,
                        wb2, wsem, *, blk, nch, pitch, ych):
    i = pl.program_id(0)
    nused = nused_ref[0]

    def weight_copies(e, slot):
        return [pltpu.make_async_copy(w2_hbm.at[e], wb2.at[slot], wsem.at[slot])]

    _weight_prefetch(i, nused, blk_e_ref, first_ref, par_ref, nxt_ref, weight_copies)

    @pl.when(i < nused)
    def _():
        wslot = par_ref[i]
        hdn = h_ref[...]
        for j in range(nch * 2 * LANES // ych):
            yj = jnp.dot(hdn, wb2[wslot, :, j * ych:(j + 1) * ych].astype(BF16),
                         preferred_element_type=F32)
            for c in range(ych // (2 * LANES)):
                y_ref[pl.ds(j * (ych // (2 * LANES)) + c, blk, stride=pitch), :] = (
                    _pack_bf16_pair(yj[:, 2 * c * LANES:(2 * c + 1) * LANES],
                                    yj[:, (2 * c + 1) * LANES:(2 * c + 2) * LANES]))
        for c in range(nch, pitch):
            y_ref[pl.ds(c, blk, stride=pitch), :] = jnp.zeros((blk, LANES), jnp.int32)

    @pl.when(i >= nused)
    def _():
        y_ref[...] = jnp.zeros_like(y_ref)


def _experts(xn3, w1, w3, w2, blk_e, nused, src_start, tok_sorted, blk):
    E, D, De = w1.shape
    nch = D // (2 * LANES)
    pitch = _row_pitch(nch)
    nb = blk_e.shape[0]
    n_assign = tok_sorted.shape[0]
    b = jnp.arange(nb, dtype=jnp.int32)
    first = (b < nused[0]) & (blk_e != jnp.concatenate([jnp.full((1,), -1, jnp.int32), blk_e[:-1]]))
    par = ((jnp.cumsum(first.astype(jnp.int32)) - 1) % 2).astype(jnp.int32)
    nxt_first = lax.cummin(jnp.where(first, b, nb), axis=0, reverse=True)
    nxt_first = jnp.concatenate([nxt_first[1:], jnp.full((1,), nb, jnp.int32)])
    nxt = jnp.where(nxt_first < nb, blk_e[jnp.minimum(nxt_first, nb - 1)], -1).astype(jnp.int32)
    first = first.astype(jnp.int32)
    up_spec = pltpu.PrefetchScalarGridSpec(
        num_scalar_prefetch=7, grid=(nb,),
        in_specs=[pl.BlockSpec(memory_space=pl.ANY)] * 3,
        out_specs=pl.BlockSpec((blk, De), lambda i, *_: (i, 0)),
        scratch_shapes=[pltpu.VMEM((2, blk * pitch, LANES), jnp.int32),
                        pltpu.VMEM((2, D, De), F32), pltpu.VMEM((2, D, De), F32),
                        pltpu.SemaphoreType.DMA((2,)), pltpu.SemaphoreType.DMA((2, 2))])
    hdn = pl.pallas_call(
        functools.partial(_expert_up_kernel, blk=blk, n_assign=n_assign, nch=nch, pitch=pitch,
                          kch=min(1024, D)),
        grid_spec=up_spec,
        out_shape=jax.ShapeDtypeStruct((nb * blk, De), BF16),
        compiler_params=_cparams(("arbitrary",)), name="experts_up")(
            blk_e, nused, first, par, nxt, src_start, tok_sorted, xn3, w1, w3)
    down_spec = pltpu.PrefetchScalarGridSpec(
        num_scalar_prefetch=5, grid=(nb,),
        in_specs=[pl.BlockSpec((blk, De), lambda i, *_: (i, 0)),
                  pl.BlockSpec(memory_space=pl.ANY)],
        out_specs=pl.BlockSpec((blk * pitch, LANES), lambda i, *_: (i, 0)),
        scratch_shapes=[pltpu.VMEM((2, De, D), F32), pltpu.SemaphoreType.DMA((2,))])
    return pl.pallas_call(
        functools.partial(_expert_down_kernel, blk=blk, nch=nch, pitch=pitch, ych=min(1024, D)),
        grid_spec=down_spec,
        out_shape=jax.ShapeDtypeStruct((nb * blk * pitch, LANES), jnp.int32),
        compiler_params=_cparams(("arbitrary",)), name="experts_down")(
            blk_e, nused, first, par, nxt, hdn, w2)


def _combine_kernel(pos_ref, y_hbm, x_ref, rt_ref, g_ref, x2_ref, xn_ref, ybuf, sem, *, tc, nch,
                    pitch):
    i = pl.program_id(0)
    n = pl.num_programs(0)

    def row_copy(step, r, slot):
        return pltpu.make_async_copy(
            y_hbm.at[pl.ds(pl.multiple_of(pos_ref[step * 2 * tc + r] * pitch, 8), nch), :],
            ybuf.at[slot, pl.ds(pl.multiple_of(r * pitch, 8), nch), :], sem.at[slot])

    def gather(step, slot):
        return _gather_rows(lambda r: row_copy(step, r, slot).start(),
                            lambda r: row_copy(step, r, slot).wait(), 2 * tc)

    @pl.when(i == 0)
    def _():
        gather(0, 0)[0]()

    @pl.when(i + 1 < n)
    def _():
        gather(i + 1, (i + 1) % 2)[0]()

    slot = i % 2
    gather(i, slot)[1]()
    y0, y1 = [], []
    for c in range(nch):
        y0 += _unpack_bf16_pair(ybuf[slot, pl.ds(c, tc, stride=pitch), :])
        y1 += _unpack_bf16_pair(ybuf[slot, pl.ds(tc * pitch + c, tc, stride=pitch), :])
    y0 = jnp.concatenate(y0, axis=1)
    y1 = jnp.concatenate(y1, axis=1)
    x2 = x_ref[...] + rt_ref[:, 2:3] * y0 + rt_ref[:, 3:4] * y1
    x2_ref[...] = x2
    xn_ref[...] = _rms(x2, g_ref[...]).astype(xn_ref.dtype)


def _combine(y, x, route, g, pos, tc=256):
    T, D = x.shape
    tc = min(tc, T)
    nch = D // (2 * LANES)
    pitch = _row_pitch(nch)
    grid_spec = pltpu.PrefetchScalarGridSpec(
        num_scalar_prefetch=1, grid=(T // tc,),
        in_specs=[pl.BlockSpec(memory_space=pl.ANY),
                  pl.BlockSpec((tc, D), lambda i, ps: (i, 0)),
                  pl.BlockSpec((tc, LANES), lambda i, ps: (i, 0)),
                  pl.BlockSpec((1, D), lambda i, ps: (0, 0))],
        out_specs=[pl.BlockSpec((tc, D), lambda i, ps: (i, 0)),
                   pl.BlockSpec((tc, D), lambda i, ps: (i, 0))],
        scratch_shapes=[pltpu.VMEM((2, 2 * tc * pitch, LANES), jnp.int32),
                        pltpu.SemaphoreType.DMA((2,))])
    pos_tiled = pos.reshape(T // tc, tc, 2).transpose(0, 2, 1).reshape(-1)
    return pl.pallas_call(
        functools.partial(_combine_kernel, tc=tc, nch=nch, pitch=pitch), grid_spec=grid_spec,
        out_shape=[jax.ShapeDtypeStruct((T, D), F32), jax.ShapeDtypeStruct((T, D), BF16)],
        compiler_params=_cparams(("arbitrary",)), name="combine")(
            pos_tiled, y, x, route, g.reshape(1, D))


def _ple_kernel(xn_ref, wg_ref, p_ref, wp_ref, x_ref, o_ref, wg_bf, wp_bf):
    _cache_bf16(wg_ref, wg_bf)
    _cache_bf16(wp_ref, wp_bf)
    gate = _sigmoid(jnp.dot(xn_ref[...], wg_bf[...], preferred_element_type=F32))
    emb = jnp.dot(p_ref[...].astype(BF16), wp_bf[...], preferred_element_type=F32)
    o_ref[...] = x_ref[...] + gate * emb


def _ple(xn, w_pg, p, w_ple, x, tm=1024, tn=512):
    T, D = xn.shape
    Pd = p.shape[1]
    tm, tn = min(tm, T), min(tn, D)
    return pl.pallas_call(
        _ple_kernel, grid=(D // tn, T // tm),
        in_specs=[pl.BlockSpec((tm, D), lambda j, i: (i, 0)),
                  pl.BlockSpec((D, tn), lambda j, i: (0, j)),
                  pl.BlockSpec((tm, Pd), lambda j, i: (i, 0)),
                  pl.BlockSpec((Pd, tn), lambda j, i: (0, j)),
                  pl.BlockSpec((tm, tn), lambda j, i: (i, j))],
        out_specs=pl.BlockSpec((tm, tn), lambda j, i: (i, j)),
        out_shape=jax.ShapeDtypeStruct((T, D), F32),
        scratch_shapes=[pltpu.VMEM((D, tn), BF16), pltpu.VMEM((Pd, tn), BF16)],
        compiler_params=_cparams(("parallel", "arbitrary")), name="ple")(xn, w_pg, p, w_ple, x)


def _dispatch_tables(route, blk):
    T = route.shape[0]
    A = 2 * T
    eid = route[:, 0:2].astype(jnp.int32).reshape(-1)
    iota = jnp.arange(A, dtype=jnp.int32)
    experts = jnp.arange(N_EXPERTS, dtype=jnp.int32)
    _, order = lax.sort((eid, iota), num_keys=1)
    _, inv = lax.sort((order, iota), num_keys=1)
    onehot = eid[:, None] == experts[None, :]
    counts = jnp.sum(onehot, axis=0, dtype=jnp.int32)
    starts = jnp.cumsum(counts) - counts
    nblk = (counts + blk - 1) // blk
    bends = jnp.cumsum(nblk)
    bstarts = bends - nblk
    nb = (A + N_EXPERTS * blk) // blk
    b = jnp.arange(nb, dtype=jnp.int32)
    blk_e = jnp.minimum(jnp.sum(b[:, None] >= bends[None, :], axis=1),
                        N_EXPERTS - 1).astype(jnp.int32)
    shift = starts - bstarts * blk
    src_start = b * blk + jnp.sum(
        jnp.where(blk_e[:, None] == experts[None, :], shift[None, :], 0), axis=1)
    nused = bends[-1].astype(jnp.int32).reshape(1)
    pos = inv - jnp.sum(jnp.where(onehot, shift[None, :], 0), axis=1)
    return blk_e, nused, src_start.astype(jnp.int32), order // 2, pos.reshape(T, 2)


def _layer(x2d, p2d, pos_col, B, S, g_mix, w_in, b_mgate, g_mhead, g_cq, w_uq, g_ckv, w_ukv,
           g_qn, g_kn, w_bm, w_ba, w_out, g_ffn, w_rg, b_rg, w_re, b_re, w_e1, w_e3, w_e2,
           g_ple, w_pg, w_ple):
    T, D = x2d.shape
    H = M_HEADS
    mqk, mv = H * M_QK_HD, H * M_V_HD
    o_gate = 2 * mqk + 2 * mv
    o_cq = o_gate + 4 * H
    o_ckv = o_cq + Q_LORA
    o_kpe = o_ckv + KV_LORA
    o_br = o_kpe + A_ROPE
    w_t = jnp.swapaxes(w_in, 0, 1)
    w_mid_t = lax.optimization_barrier(w_t[o_gate:o_br])
    w_cq = w_mid_t[o_cq - o_gate:o_ckv - o_gate].T.astype(BF16)
    w_ckv = w_mid_t[o_ckv - o_gate:o_kpe - o_gate].T.astype(BF16)
    w_sm = jnp.concatenate(
        [w_mid_t[o_kpe - o_gate:], w_mid_t[:4 * H],
         jnp.zeros((LANES - A_ROPE - 4 * H, D), F32)], axis=0).T.astype(BF16)

    h = _rmsnorm(x2d, g_mix)
    z_a = _matmul_t(h, w_t, 0, o_gate, BF16, name="inproj_mlstm")
    z_br = _matmul_t(h, w_t, o_br, 2 * D, BF16, name="inproj_gates")
    cq_n, ckv_n, small = _inproj_small(h, w_cq, w_ckv, w_sm, g_cq, g_ckv)

    g_col = small[:, A_ROPE:A_ROPE + 4 * H] + b_mgate[None, :]
    g_row = g_col.reshape(B, S, 4 * H).transpose(0, 2, 1)
    hb = _mlstm(z_a, g_col, g_row, B, S, reverse=True)
    hm = _mlstm(z_a, g_col, g_row, B, S, reverse=False, hb=hb, g_mhead=g_mhead)

    pad_q = A_QK_PAD - A_QK_HD
    w_uq_p = jnp.pad(w_uq.reshape(Q_LORA, A_HEADS, A_QK_HD).transpose(1, 0, 2),
                     ((0, 0), (0, 0), (0, pad_q))).astype(BF16)
    w_ukv_h = w_ukv.reshape(KV_LORA, A_HEADS, A_NOPE + A_V_HD).transpose(1, 0, 2).astype(BF16)
    g_qn_p = jnp.pad(g_qn, (0, pad_q)).reshape(1, A_QK_PAD)
    g_kn_p = jnp.pad(g_kn, (0, pad_q)).reshape(1, A_QK_PAD)
    inv = ROPE_THETA ** (-jnp.arange(0, A_ROPE, 2, dtype=F32) / A_ROPE)
    inv_row = jnp.concatenate([inv, inv, jnp.zeros((LANES - A_ROPE,), F32)]).reshape(1, LANES)
    cos_t, sin_t = _rope_tables(pos_col, inv_row)
    q = _q_prep(cq_n, w_uq_p, g_qn_p, cos_t, sin_t)
    kt, v = _kv_prep(ckv_n, w_ukv_h, small, g_kn_p, cos_t, sin_t)
    ha = _attention(q, kt, v, B, S)

    mix = _merge(hm, ha, w_bm, w_ba, z_br, 0)
    x1 = _matmul_residual(mix, w_out, x2d)

    w_r = jnp.concatenate([w_rg, w_re, jnp.zeros((D, LANES - N_GROUPS - N_EXPERTS), F32)],
                          axis=1).astype(BF16)
    b_r = jnp.concatenate([b_rg, b_re, jnp.zeros((LANES - N_GROUPS - N_EXPERTS,), F32)]
                          ).reshape(1, LANES)
    route, xn3 = _router(x1, g_ffn, w_r, b_r)
    blk = min(MOE_BLK, T)
    blk_e, nused, src_start, tok_sorted, pos = _dispatch_tables(route, blk)
    y = _experts(xn3, w_e1, w_e3, w_e2, blk_e, nused, src_start, tok_sorted, blk)
    x2, xn2 = _combine(y, x1, route, g_ple, pos)

    return _ple(xn2, w_pg, p2d, w_ple, x2)


def kernel(x, p, positions, g_mix, w_in, b_mgate, g_mhead, g_cq, w_uq, g_ckv, w_ukv, g_qn, g_kn,
           w_bm, w_ba, w_out, g_ffn, w_rg, b_rg, w_re, b_re, w_e1, w_e3, w_e2, g_ple, w_pg, w_ple):
    B, S, D = x.shape
    T = B * S
    x2d = x.reshape(T, D)
    pos_col = positions.reshape(T, 1).astype(jnp.int32)
    for l in range(p.shape[0]):
        x2d = _layer(x2d, p[l].reshape(T, -1), pos_col, B, S, g_mix[l], w_in[l], b_mgate[l],
                     g_mhead[l], g_cq[l], w_uq[l], g_ckv[l], w_ukv[l], g_qn[l], g_kn[l],
                     w_bm[l], w_ba[l], w_out[l], g_ffn[l], w_rg[l], b_rg[l], w_re[l], b_re[l],
                     w_e1[l], w_e3[l], w_e2[l], g_ple[l], w_pg[l], w_ple[l])
    return x2d.reshape(B, S, D)
```

```python
import functools
import math

import jax
import jax.numpy as jnp
from jax import lax
from jax.experimental import pallas as pl
from jax.experimental.pallas import tpu as pltpu

F32 = jnp.float32
BF16 = jnp.bfloat16

EPS = 1e-6
M_HEADS = 8
M_QK_HD = 128
M_V_HD = 256
A_HEADS = 16
A_NOPE = 128
A_ROPE = 64
A_QK_HD = A_NOPE + A_ROPE
A_V_HD = 128
A_QK_PAD = 256
Q_LORA = 1024
KV_LORA = 512
ROPE_THETA = 10000.0
N_GROUPS = 8
E_PER_GROUP = 8
N_EXPERTS = N_GROUPS * E_PER_GROUP
D_EXPERT = 512
LANES = 128

M_CHUNK = 256
MOE_BLK = 256
ATT_TQ = 1024
ATT_TK = 256
VMEM_LIMIT = 56 * 1024 * 1024
NEG = -1e30
LOG2E = 1.4426950408889634


def _cparams(sem):
    return pltpu.CompilerParams(dimension_semantics=sem, vmem_limit_bytes=VMEM_LIMIT)


def _sigmoid(x):
    return 1.0 / (1.0 + jnp.exp(-x))


def _log_sigmoid(x):
    return jnp.minimum(x, 0.0) - jnp.log(1.0 + jnp.exp(-jnp.abs(x)))


def _rms(xf, g):
    return xf * lax.rsqrt(jnp.mean(xf * xf, axis=-1, keepdims=True) + EPS) * g


def _norm_kernel(x_ref, g_ref, o_ref):
    o_ref[...] = _rms(x_ref[...], g_ref[...]).astype(o_ref.dtype)


def _rmsnorm(x, g, tm=512):
    M, D = x.shape
    tm = min(tm, M)
    return pl.pallas_call(
        _norm_kernel, grid=(M // tm,),
        in_specs=[pl.BlockSpec((tm, D), lambda i: (i, 0)),
                  pl.BlockSpec((1, D), lambda i: (0, 0))],
        out_specs=pl.BlockSpec((tm, D), lambda i: (i, 0)),
        out_shape=jax.ShapeDtypeStruct((M, D), BF16),
        compiler_params=_cparams(("parallel",)), name="rmsnorm")(x, g.reshape(1, D))


def _cache_bf16(w_ref, wbf_ref):
    @pl.when(pl.program_id(1) == 0)
    def _():
        wbf_ref[...] = w_ref[...].astype(BF16)


def _mm_t_kernel(a_ref, wt_hbm, o_ref, wst, wbf_ref, sem, *, row0, tn):
    j = pl.program_id(0)
    i = pl.program_id(1)

    def wcopy(jj):
        return pltpu.make_async_copy(
            wt_hbm.at[pl.ds(pl.multiple_of(row0 + jj * tn, 8), tn), :], wst, sem)

    @pl.when((j == 0) & (i == 0))
    def _():
        wcopy(0).start()

    @pl.when(i == 0)
    def _():
        wcopy(j).wait()
        wbf_ref[...] = wst[...].astype(BF16)

        @pl.when(j + 1 < pl.num_programs(0))
        def _():
            wcopy(j + 1).start()

    o_ref[...] = lax.dot_general(a_ref[...], wbf_ref[...], (((1,), (1,)), ((), ())),
                                 preferred_element_type=F32).astype(o_ref.dtype)


def _matmul_t(a, w_t, row0, n_cols, out_dtype, tm=1024, tn=1024, name="matmul"):
    M, K = a.shape
    tm, tn = min(tm, M), min(tn, n_cols)
    return pl.pallas_call(
        functools.partial(_mm_t_kernel, row0=row0, tn=tn), grid=(n_cols // tn, M // tm),
        in_specs=[pl.BlockSpec((tm, K), lambda j, i: (i, 0)),
                  pl.BlockSpec(memory_space=pl.ANY)],
        out_specs=pl.BlockSpec((tm, tn), lambda j, i: (i, j)),
        out_shape=jax.ShapeDtypeStruct((M, n_cols), out_dtype),
        scratch_shapes=[pltpu.VMEM((tn, K), F32), pltpu.VMEM((tn, K), BF16),
                        pltpu.SemaphoreType.DMA(())],
        compiler_params=_cparams(("arbitrary", "arbitrary")), name=name)(a, w_t)


def _inproj_small_kernel(h_ref, wq_ref, wkv_ref, ws_ref, gq_ref, gkv_ref,
                         cq_ref, ckv_ref, sm_ref):
    h = h_ref[...]
    cq = jnp.dot(h, wq_ref[...], preferred_element_type=F32)
    cq_ref[...] = _rms(cq, gq_ref[...]).astype(cq_ref.dtype)
    ckv = jnp.dot(h, wkv_ref[...], preferred_element_type=F32)
    ckv_ref[...] = _rms(ckv, gkv_ref[...]).astype(ckv_ref.dtype)
    sm_ref[...] = jnp.dot(h, ws_ref[...], preferred_element_type=F32)


def _inproj_small(h, w_cq, w_ckv, w_sm, g_cq, g_ckv, tm=512):
    M, D = h.shape
    tm = min(tm, M)
    nq, nkv, ns = w_cq.shape[1], w_ckv.shape[1], w_sm.shape[1]
    full = lambda n: pl.BlockSpec((D, n), lambda i: (0, 0))
    row = lambda n: pl.BlockSpec((tm, n), lambda i: (i, 0))
    return pl.pallas_call(
        _inproj_small_kernel, grid=(M // tm,),
        in_specs=[row(D), full(nq), full(nkv), full(ns),
                  pl.BlockSpec((1, nq), lambda i: (0, 0)),
                  pl.BlockSpec((1, nkv), lambda i: (0, 0))],
        out_specs=[row(nq), row(nkv), row(ns)],
        out_shape=[jax.ShapeDtypeStruct((M, nq), BF16),
                   jax.ShapeDtypeStruct((M, nkv), BF16),
                   jax.ShapeDtypeStruct((M, ns), F32)],
        compiler_params=_cparams(("parallel",)), name="inproj_small")(
            h, w_cq, w_ckv, w_sm, g_cq.reshape(1, nq), g_ckv.reshape(1, nkv))


def _mlstm_kernel(*refs, reverse, finish, L, goff):
    if finish:
        (q_ref, k_ref, v_ref, gc_ref, gr_ref, hb_ref, o_ref, gm_ref,
         out_ref, C_ref, n_ref, m_ref) = refs
    else:
        q_ref, k_ref, v_ref, gc_ref, gr_ref, out_ref, C_ref, n_ref, m_ref = refs
    H, dk, dv = M_HEADS, M_QK_HD, M_V_HD
    log_scale = -0.5 * math.log(dk)

    @pl.when(pl.program_id(1) == 0)
    def _():
        C_ref[...] = jnp.zeros_like(C_ref)
        n_ref[...] = jnp.zeros_like(n_ref)
        m_ref[...] = jnp.zeros_like(m_ref)

    t_idx = lax.broadcasted_iota(jnp.int32, (L, L), 0)
    s_idx = lax.broadcasted_iota(jnp.int32, (L, L), 1)
    if reverse:
        mask, mask_t = s_idx >= t_idx, t_idx >= s_idx
    else:
        mask, mask_t = s_idx <= t_idx, t_idx <= s_idx

    hs = range(H)
    gates = []
    for h in hs:
        i_col = gc_ref[:, goff + h:goff + h + 1]
        f_col = gc_ref[:, goff + H + h:goff + H + h + 1]
        i_row = gr_ref[goff + h:goff + h + 1, :]
        f_row = gr_ref[goff + H + h:goff + H + h + 1, :]
        lf_col = _log_sigmoid(f_col)
        lf_row = _log_sigmoid(f_row)
        b_col = jnp.sum(jnp.where(mask, lf_row, 0.0), axis=1, keepdims=True)
        b_row = jnp.sum(jnp.where(mask_t, lf_col, 0.0), axis=0, keepdims=True)
        g = jnp.sum(lf_row, axis=1, keepdims=True)
        gates.append((i_col, i_row, b_col, b_row, g, m_ref[h][:, 0:1]))

    stab = []
    for h in hs:
        i_col, i_row, b_col, b_row, g, m_prev = gates[h]
        dmat = jnp.where(mask, b_col - b_row + i_row, NEG)
        a = b_col + m_prev
        m_t = jnp.maximum(a, jnp.max(dmat, axis=1, keepdims=True))
        stab.append((dmat, a, m_t))

    qs = [q_ref[:, h * dk:(h + 1) * dk] for h in hs]
    ks = [k_ref[:, h * dk:(h + 1) * dk] for h in hs]
    vs = [v_ref[:, h * dv:(h + 1) * dv] for h in hs]
    qks = [lax.dot_general(qs[h], ks[h], (((1,), (1,)), ((), ())),
                           preferred_element_type=F32) for h in hs]

    intra = []
    for h in hs:
        dmat, a, m_t = stab[h]
        m_s = m_t - log_scale
        intra.append((jnp.exp(dmat - m_s) * qks[h], jnp.exp(a - m_s)))

    c_states = [C_ref[h] for h in hs]
    n_states = [n_ref[h] for h in hs]
    outs = []
    for h in hs:
        w_intra, w_inter = intra[h]
        q_c = jnp.dot(qs[h], c_states[h].astype(BF16), preferred_element_type=F32)
        num = jnp.dot(w_intra.astype(BF16), vs[h], preferred_element_type=F32) + w_inter * q_c
        q_n = jnp.sum(qs[h].astype(F32) * n_states[h], axis=1, keepdims=True)
        den = jnp.sum(w_intra, axis=1, keepdims=True) + w_inter * q_n
        outs.append(num / jnp.maximum(jnp.abs(den), jnp.exp(-stab[h][2])))

    for h in hs:
        i_col, i_row, b_col, b_row, g, m_prev = gates[h]
        w_col = g - b_col + i_col
        m_new = jnp.maximum(g + m_prev, jnp.max(w_col, axis=0, keepdims=True))
        decay = jnp.exp(g + m_prev - m_new)
        kw = ks[h].astype(F32) * jnp.exp(w_col - m_new)
        C_ref[h] = decay * c_states[h] + lax.dot_general(
            kw.astype(BF16), vs[h], (((0,), (0,)), ((), ())), preferred_element_type=F32)
        n_ref[h] = decay * n_states[h] + jnp.sum(kw, axis=0, keepdims=True)
        m_ref[h] = jnp.broadcast_to(m_new, (1, LANES))

    for h in hs:
        if finish:
            tot = outs[h] + hb_ref[:, h * dv:(h + 1) * dv]
            y = _rms(tot, gm_ref[h:h + 1, :])
            gate = _sigmoid(o_ref[:, h * dv:(h + 1) * dv].astype(F32))
            out_ref[:, h * dv:(h + 1) * dv] = (y * gate).astype(out_ref.dtype)
        else:
            out_ref[:, h * dv:(h + 1) * dv] = outs[h]


def _mlstm(z_a, g_col, g_row, B, S, *, reverse, hb=None, g_mhead=None):
    H, dk, dv = M_HEADS, M_QK_HD, M_V_HD
    L = min(M_CHUNK, S)
    nc = S // L
    T = B * S
    finish = not reverse
    if reverse:
        cidx = lambda b, c: b * nc + (nc - 1 - c)
        ridx = lambda b, c: nc - 1 - c
    else:
        cidx = lambda b, c: b * nc + c
        ridx = lambda b, c: c
    mqk, mv = H * dk, H * dv
    in_specs = [
        pl.BlockSpec((L, mqk), lambda b, c: (cidx(b, c), 0)),
        pl.BlockSpec((L, mqk), lambda b, c: (cidx(b, c), 1)),
        pl.BlockSpec((L, mv), lambda b, c: (cidx(b, c), 1)),
        pl.BlockSpec((L, 4 * H), lambda b, c: (cidx(b, c), 0)),
        pl.BlockSpec((None, 4 * H, L), lambda b, c: (b, 0, ridx(b, c))),
    ]
    args = [z_a, z_a, z_a, g_col, g_row]
    if finish:
        in_specs += [
            pl.BlockSpec((L, mv), lambda b, c: (cidx(b, c), 0)),
            pl.BlockSpec((L, mv), lambda b, c: (cidx(b, c), 2)),
            pl.BlockSpec((H, dv), lambda b, c: (0, 0)),
        ]
        args += [hb, z_a, g_mhead.reshape(H, dv)]
    kern = functools.partial(_mlstm_kernel, reverse=reverse, finish=finish, L=L,
                             goff=2 * H if reverse else 0)
    return pl.pallas_call(
        kern, grid=(B, nc), in_specs=in_specs,
        out_specs=pl.BlockSpec((L, mv), lambda b, c: (cidx(b, c), 0)),
        out_shape=jax.ShapeDtypeStruct((T, mv), BF16 if finish else F32),
        scratch_shapes=[pltpu.VMEM((H, dk, dv), F32), pltpu.VMEM((H, 1, dk), F32),
                        pltpu.VMEM((H, 1, LANES), F32)],
        compiler_params=_cparams(("parallel", "arbitrary")),
        name="mlstm_bwd" if reverse else "mlstm_fwd")(*args)


def _rope_table_kernel(pos_ref, inv_ref, cos_ref, sin_ref):
    ang = pos_ref[...].astype(F32) * inv_ref[...]
    lane = lax.broadcasted_iota(jnp.int32, ang.shape, 1)
    cos_ref[...] = jnp.cos(ang)
    sin_ref[...] = jnp.where(lane < A_ROPE // 2, -jnp.sin(ang), jnp.sin(ang))


def _rope_tables(pos_col, inv_row, tm=1024):
    T = pos_col.shape[0]
    tm = min(tm, T)
    return pl.pallas_call(
        _rope_table_kernel, grid=(T // tm,),
        in_specs=[pl.BlockSpec((tm, 1), lambda i: (i, 0)),
                  pl.BlockSpec((1, LANES), lambda i: (0, 0))],
        out_specs=[pl.BlockSpec((tm, LANES), lambda i: (i, 0))] * 2,
        out_shape=[jax.ShapeDtypeStruct((T, LANES), F32)] * 2,
        compiler_params=_cparams(("parallel",)), name="rope_tables")(pos_col, inv_row)


def _rope(r, cos_ref, sin_ref):
    lane = lax.broadcasted_iota(jnp.int32, r.shape, 1)
    half = A_ROPE // 2
    swapped = jnp.where(lane < half, pltpu.roll(r, LANES - half, 1), pltpu.roll(r, half, 1))
    return r * cos_ref[...] + swapped * sin_ref[...]


PREP_HEADS = 8


def _q_prep_kernel(c_ref, w_ref, g_ref, cos_ref, sin_ref, o_ref):
    c = c_ref[...]
    hs = range(w_ref.shape[0])
    accs = [jnp.dot(c, w_ref[h], preferred_element_type=F32) for h in hs]
    rs = [lax.rsqrt(jnp.sum(a * a, axis=1, keepdims=True) / A_QK_HD + EPS) for a in accs]
    gain = g_ref[...] * (A_QK_HD ** -0.5 * LOG2E)
    ys = [accs[h] * rs[h] * gain for h in hs]
    for h in hs:
        o_ref[h, :, 0:A_NOPE] = ys[h][:, 0:A_NOPE].astype(o_ref.dtype)
    for h in hs:
        o_ref[h, :, A_NOPE:] = _rope(ys[h][:, A_NOPE:], cos_ref, sin_ref).astype(o_ref.dtype)


def _q_prep(cq_n, w_uq_p, g_qn_p, cos_t, sin_t, tm=1024):
    T, R = cq_n.shape
    tm = min(tm, T)
    H, hb = A_HEADS, PREP_HEADS
    return pl.pallas_call(
        _q_prep_kernel, grid=(T // tm, H // hb),
        in_specs=[pl.BlockSpec((tm, R), lambda i, h: (i, 0)),
                  pl.BlockSpec((hb, R, A_QK_PAD), lambda i, h: (h, 0, 0)),
                  pl.BlockSpec((1, A_QK_PAD), lambda i, h: (0, 0)),
                  pl.BlockSpec((tm, LANES), lambda i, h: (i, 0)),
                  pl.BlockSpec((tm, LANES), lambda i, h: (i, 0))],
        out_specs=pl.BlockSpec((hb, tm, A_QK_PAD), lambda i, h: (h, i, 0)),
        out_shape=jax.ShapeDtypeStruct((H, T, A_QK_PAD), BF16),
        compiler_params=_cparams(("parallel", "parallel")), name="mla_q_prep")(
            cq_n, w_uq_p, g_qn_p, cos_t, sin_t)


def _kv_prep_kernel(c_ref, w_ref, sm_ref, g_ref, cos_ref, sin_ref, kt_ref, v_ref):
    c = c_ref[...]
    lane = lax.broadcasted_iota(jnp.int32, sm_ref.shape, 1)
    k_pe = jnp.where(lane < A_ROPE, sm_ref[...], 0.0)
    ss_pe = jnp.sum(k_pe * k_pe, axis=1, keepdims=True)
    hs = range(w_ref.shape[0])
    accs = [jnp.dot(c, w_ref[h], preferred_element_type=F32) for h in hs]
    for h in hs:
        v_ref[h, :, 0:A_V_HD] = accs[h][:, A_NOPE:].astype(v_ref.dtype)
        v_ref[h, :, A_V_HD:] = jnp.ones((c.shape[0], A_V_HD), v_ref.dtype)
    rs = [lax.rsqrt((jnp.sum(a[:, 0:A_NOPE] * a[:, 0:A_NOPE], axis=1, keepdims=True) + ss_pe)
                    / A_QK_HD + EPS) for a in accs]
    kns = [accs[h][:, 0:A_NOPE] * rs[h] * g_ref[:, 0:A_NOPE] for h in hs]
    krs = [_rope(k_pe * rs[h] * g_ref[:, A_NOPE:], cos_ref, sin_ref) for h in hs]
    for h in hs:
        kt_ref[h, 0:A_NOPE, :] = kns[h].T.astype(kt_ref.dtype)
    for h in hs:
        kt_ref[h, A_NOPE:, :] = krs[h].T.astype(kt_ref.dtype)


def _kv_prep(ckv_n, w_ukv_h, small, g_kn_p, cos_t, sin_t, tm=1024):
    T, R = ckv_n.shape
    tm = min(tm, T)
    H, hb = A_HEADS, PREP_HEADS
    return pl.pallas_call(
        _kv_prep_kernel, grid=(T // tm, H // hb),
        in_specs=[pl.BlockSpec((tm, R), lambda i, h: (i, 0)),
                  pl.BlockSpec((hb, R, A_NOPE + A_V_HD), lambda i, h: (h, 0, 0)),
                  pl.BlockSpec((tm, LANES), lambda i, h: (i, 0)),
                  pl.BlockSpec((1, A_QK_PAD), lambda i, h: (0, 0)),
                  pl.BlockSpec((tm, LANES), lambda i, h: (i, 0)),
                  pl.BlockSpec((tm, LANES), lambda i, h: (i, 0))],
        out_specs=[pl.BlockSpec((hb, A_QK_PAD, tm), lambda i, h: (h, 0, i)),
                   pl.BlockSpec((hb, tm, 2 * A_V_HD), lambda i, h: (h, i, 0))],
        out_shape=[jax.ShapeDtypeStruct((H, A_QK_PAD, T), BF16),
                   jax.ShapeDtypeStruct((H, T, 2 * A_V_HD), BF16)],
        compiler_params=_cparams(("parallel", "parallel")), name="mla_kv_prep")(
            ckv_n, w_ukv_h, small, g_kn_p, cos_t, sin_t)


def _attn_kernel(q_ref, kt_ref, v_ref, o_ref, *, tk):
    q = q_ref[...]
    S = kt_ref.shape[1]
    m = acc = None
    for j in range(S // tk):
        s = jnp.dot(q, kt_ref[:, j * tk:(j + 1) * tk], preferred_element_type=F32)
        mj = jnp.max(s, axis=1, keepdims=True)
        vj = v_ref[j * tk:(j + 1) * tk, :]
        if j == 0:
            m = mj
            acc = jnp.dot(jnp.exp2(s - m).astype(BF16), vj, preferred_element_type=F32)
        else:
            m_new = jnp.maximum(m, mj)
            acc = (jnp.exp2(m - m_new) * acc
                   + jnp.dot(jnp.exp2(s - m_new).astype(BF16), vj, preferred_element_type=F32))
            m = m_new
    o_ref[...] = (acc[:, 0:A_V_HD] / acc[:, A_V_HD:A_V_HD + 1]).astype(o_ref.dtype)


def _attention(q, kt, v, B, S):
    H, T, dq = q.shape
    tq, tk = min(ATT_TQ, S), min(ATT_TK, S)
    nq = S // tq
    return pl.pallas_call(
        functools.partial(_attn_kernel, tk=tk), grid=(B, H, nq),
        in_specs=[pl.BlockSpec((None, tq, dq), lambda b, h, i: (h, b * nq + i, 0)),
                  pl.BlockSpec((None, dq, S), lambda b, h, i: (h, 0, b)),
                  pl.BlockSpec((None, S, 2 * A_V_HD), lambda b, h, i: (h, b, 0))],
        out_specs=pl.BlockSpec((tq, A_V_HD), lambda b, h, i: (b * nq + i, h)),
        out_shape=jax.ShapeDtypeStruct((T, H * A_V_HD), BF16),
        compiler_params=_cparams(("parallel", "parallel", "parallel")), name="mla_attention")(
            q, kt, v)


def _merge_kernel(hm_ref, ha_ref, wm_ref, wa_ref, bm_ref, ba_ref, o_ref, wm_bf, wa_bf):
    _cache_bf16(wm_ref, wm_bf)
    _cache_bf16(wa_ref, wa_bf)
    ym = jnp.dot(hm_ref[...], wm_bf[...], preferred_element_type=F32)
    ya = jnp.dot(ha_ref[...], wa_bf[...], preferred_element_type=F32)
    o_ref[...] = (_sigmoid(bm_ref[...].astype(F32)) * ym
                  + _sigmoid(ba_ref[...].astype(F32)) * ya).astype(o_ref.dtype)


def _merge(hm, ha, w_bm, w_ba, z_a, br_off, tm=1024, tn=512):
    T, K = hm.shape
    D = w_bm.shape[1]
    tm, tn = min(tm, T), min(tn, D)
    ob = br_off // tn
    return pl.pallas_call(
        _merge_kernel, grid=(D // tn, T // tm),
        in_specs=[pl.BlockSpec((tm, K), lambda j, i: (i, 0)),
                  pl.BlockSpec((tm, K), lambda j, i: (i, 0)),
                  pl.BlockSpec((K, tn), lambda j, i: (0, j)),
                  pl.BlockSpec((K, tn), lambda j, i: (0, j)),
                  pl.BlockSpec((tm, tn), lambda j, i: (i, ob + j)),
                  pl.BlockSpec((tm, tn), lambda j, i: (i, ob + D // tn + j))],
        out_specs=pl.BlockSpec((tm, tn), lambda j, i: (i, j)),
        out_shape=jax.ShapeDtypeStruct((T, D), BF16),
        scratch_shapes=[pltpu.VMEM((K, tn), BF16), pltpu.VMEM((K, tn), BF16)],
        compiler_params=_cparams(("parallel", "arbitrary")), name="merge")(
            hm, ha, w_bm, w_ba, z_a, z_a)


def _mm_res_kernel(a_ref, w_ref, r_ref, o_ref, wbf_ref):
    _cache_bf16(w_ref, wbf_ref)
    o_ref[...] = r_ref[...] + jnp.dot(a_ref[...], wbf_ref[...], preferred_element_type=F32)


def _matmul_residual(a, b, res, tm=1024, tn=512):
    M, K = a.shape
    N = b.shape[1]
    tm, tn = min(tm, M), min(tn, N)
    return pl.pallas_call(
        _mm_res_kernel, grid=(N // tn, M // tm),
        in_specs=[pl.BlockSpec((tm, K), lambda j, i: (i, 0)),
                  pl.BlockSpec((K, tn), lambda j, i: (0, j)),
                  pl.BlockSpec((tm, tn), lambda j, i: (i, j))],
        out_specs=pl.BlockSpec((tm, tn), lambda j, i: (i, j)),
        out_shape=jax.ShapeDtypeStruct((M, N), F32),
        scratch_shapes=[pltpu.VMEM((K, tn), BF16)],
        compiler_params=_cparams(("parallel", "arbitrary")), name="out_proj")(a, b, res)


def _router_kernel(x_ref, g_ref, w_ref, b_ref, o_ref, xn3_ref):
    xf = _rms(x_ref[...], g_ref[...])
    nch = xf.shape[1] // (2 * LANES)
    tm = xf.shape[0]
    pitch = xn3_ref.shape[0] // tm
    for c in range(pitch):
        xn3_ref[pl.ds(c, tm, stride=pitch), :] = (
            _pack_bf16_pair(xf[:, 2 * c * LANES:(2 * c + 1) * LANES],
                            xf[:, (2 * c + 1) * LANES:(2 * c + 2) * LANES])
            if c < nch else jnp.zeros((tm, LANES), jnp.int32))
    xn = xf.astype(BF16)
    logits = jnp.dot(xn, w_ref[...], preferred_element_type=F32) + b_ref[...]
    lane = lax.broadcasted_iota(jnp.int32, logits.shape, 1)
    big = jnp.int32(1 << 20)
    lg = jnp.where(lane < N_GROUPS, logits, NEG)
    gmax = jnp.max(lg, axis=1, keepdims=True)
    g_idx = jnp.min(jnp.where(lg == gmax, lane, big), axis=1, keepdims=True)
    g_w = 1.0 / jnp.sum(jnp.exp(lg - gmax), axis=1, keepdims=True)
    lo = N_GROUPS + g_idx * E_PER_GROUP
    le = jnp.where((lane >= lo) & (lane < lo + E_PER_GROUP), logits, NEG)
    v1 = jnp.max(le, axis=1, keepdims=True)
    i1 = jnp.min(jnp.where(le == v1, lane, big), axis=1, keepdims=True)
    le2 = jnp.where(lane == i1, NEG, le)
    v2 = jnp.max(le2, axis=1, keepdims=True)
    i2 = jnp.min(jnp.where(le2 == v2, lane, big), axis=1, keepdims=True)
    e21 = jnp.exp(v2 - v1)
    w1 = g_w / (1.0 + e21)
    w2 = g_w * e21 / (1.0 + e21)
    o_ref[...] = jnp.where(
        lane == 0, (i1 - N_GROUPS).astype(F32),
        jnp.where(lane == 1, (i2 - N_GROUPS).astype(F32),
                  jnp.where(lane == 2, w1, jnp.where(lane == 3, w2, 0.0))))


def _router(x, g, w_r, b_r, tm=512):
    T, D = x.shape
    tm = min(tm, T)
    pitch = _row_pitch(D // (2 * LANES))
    return pl.pallas_call(
        _router_kernel, grid=(T // tm,),
        in_specs=[pl.BlockSpec((tm, D), lambda i: (i, 0)),
                  pl.BlockSpec((1, D), lambda i: (0, 0)),
                  pl.BlockSpec((D, LANES), lambda i: (0, 0)),
                  pl.BlockSpec((1, LANES), lambda i: (0, 0))],
        out_specs=[pl.BlockSpec((tm, LANES), lambda i: (i, 0)),
                   pl.BlockSpec((tm * pitch, LANES), lambda i: (i, 0))],
        out_shape=[jax.ShapeDtypeStruct((T, LANES), F32),
                   jax.ShapeDtypeStruct((T * pitch, LANES), jnp.int32)],
        compiler_params=_cparams(("parallel",)), name="router")(x, g.reshape(1, D), w_r, b_r)


def _pack_bf16_pair(a, b):
    ua = lax.bitcast_convert_type(a.astype(BF16).astype(F32), jnp.int32)
    ub = lax.bitcast_convert_type(b.astype(BF16).astype(F32), jnp.int32)
    return lax.shift_right_logical(ua, jnp.int32(16)) | (ub & jnp.int32(-65536))


def _unpack_bf16_pair(u):
    return [lax.bitcast_convert_type(lax.shift_left(u, jnp.int32(16)), F32),
            lax.bitcast_convert_type(u & jnp.int32(-65536), F32)]


def _row_pitch(nch):
    return (nch + 7) // 8 * 8 + 8


def _gather_rows(start_one, wait_one, n, unroll=8):
    def start():
        def body(r, carry):
            start_one(r)
            return carry
        lax.fori_loop(0, n, body, 0, unroll=unroll)

    def wait():
        def body(r, carry):
            wait_one(r)
            return carry
        lax.fori_loop(0, n, body, 0, unroll=unroll)
    return start, wait


def _weight_prefetch(i, nused, blk_e_ref, first_ref, par_ref, nxt_ref, copies):
    @pl.when(i == 0)
    def _():
        for c in copies(blk_e_ref[0], 0):
            c.start()

    @pl.when((i < nused) & (first_ref[i] == 1))
    def _():
        slot = par_ref[i]
        for c in copies(blk_e_ref[i], slot):
            c.wait()

        @pl.when(nxt_ref[i] >= 0)
        def _():
            for c in copies(nxt_ref[i], 1 - slot):
                c.start()


def _expert_up_kernel(blk_e_ref, nused_ref, first_ref, par_ref, nxt_ref, src_ref, tok_ref,
                      x_hbm, w1_hbm, w3_hbm, h_ref, xbuf, wb1, wb3, sem, wsem,
                      *, blk, nch, pitch, kch):
    i = pl.program_id(0)
    nused = nused_ref[0]

    def row_copy(base, r, slot):
        return pltpu.make_async_copy(
            x_hbm.at[pl.ds(pl.multiple_of(tok_ref[base + r], 8), nch), :],
            xbuf.at[slot, pl.ds(pl.multiple_of(r * pitch, 8), nch), :], sem.at[slot])

    def gather(step, slot):
        base = src_ref[step]
        return _gather_rows(lambda r: row_copy(base, r, slot).start(),
                            lambda r: row_copy(base, r, slot).wait(), blk)

    def weight_copies(e, slot):
        return [pltpu.make_async_copy(w1_hbm.at[e], wb1.at[slot], wsem.at[0, slot]),
                pltpu.make_async_copy(w3_hbm.at[e], wb3.at[slot], wsem.at[1, slot])]

    @pl.when(i == 0)
    def _():
        gather(0, 0)[0]()

    @pl.when(i + 1 < nused)
    def _():
        gather(i + 1, (i + 1) % 2)[0]()

    _weight_prefetch(i, nused, blk_e_ref, first_ref, par_ref, nxt_ref, weight_copies)

    @pl.when(i < nused)
    def _():
        slot = i % 2
        wslot = par_ref[i]
        gather(i, slot)[1]()
        parts = []
        for c in range(nch):
            parts += _unpack_bf16_pair(xbuf[slot, pl.ds(c, blk, stride=pitch), :])
        xn = jnp.concatenate([t.astype(BF16) for t in parts], axis=1)
        h1 = h3 = None
        for k in range(xn.shape[1] // kch):
            xk = xn[:, k * kch:(k + 1) * kch]
            d1 = jnp.dot(xk, wb1[wslot, k * kch:(k + 1) * kch, :].astype(BF16),
                         preferred_element_type=F32)
            d3 = jnp.dot(xk, wb3[wslot, k * kch:(k + 1) * kch, :].astype(BF16),
                         preferred_element_type=F32)
            h1 = d1 if h1 is None else h1 + d1
            h3 = d3 if h3 is None else h3 + d3
        h_ref[...] = (h1 * _sigmoid(h1) * h3).astype(h_ref.dtype)

    @pl.when(i >= nused)
    def _():
        h_ref[...] = jnp.zeros_like(h_ref)


def _expert_down_kernel(blk_e_ref, nused_ref, first_ref, par_ref, nxt_ref, h_ref, w2_hbm, y_ref,
                        wb2, wsem, *, blk, nch, pitch, ych):
    i = pl.program_id(0)
    nused = nused_ref[0]

    def weight_copies(e, slot):
        return [pltpu.make_async_copy(w2_hbm.at[e], wb2.at[slot], wsem.at[slot])]

    _weight_prefetch(i, nused, blk_e_ref, first_ref, par_ref, nxt_ref, weight_copies)

    @pl.when(i < nused)
    def _():
        wslot = par_ref[i]
        hdn = h_ref[...]
        for j in range(nch * 2 * LANES // ych):
            yj = jnp.dot(hdn, wb2[wslot, :, j * ych:(j + 1) * ych].astype(BF16),
                         preferred_element_type=F32)
            for c in range(ych // (2 * LANES)):
                y_ref[pl.ds(j * (ych // (2 * LANES)) + c, blk, stride=pitch), :] = (
                    _pack_bf16_pair(yj[:, 2 * c * LANES:(2 * c + 1) * LANES],
                                    yj[:, (2 * c + 1) * LANES:(2 * c + 2) * LANES]))
        for c in range(nch, pitch):
            y_ref[pl.ds(c, blk, stride=pitch), :] = jnp.zeros((blk, LANES), jnp.int32)

    @pl.when(i >= nused)
    def _():
        y_ref[...] = jnp.zeros_like(y_ref)


def _experts(xn3, w1, w3, w2, blk_e, nused, src_start, tok_sorted, blk):
    E, D, De = w1.shape
    nch = D // (2 * LANES)
    pitch = _row_pitch(nch)
    nb = blk_e.shape[0]
    tok_rows = jnp.concatenate([tok_sorted, jnp.broadcast_to(tok_sorted[-1:], (blk,))]) * pitch
    b = jnp.arange(nb, dtype=jnp.int32)
    first = (b < nused[0]) & (blk_e != jnp.concatenate([jnp.full((1,), -1, jnp.int32), blk_e[:-1]]))
    par = ((jnp.cumsum(first.astype(jnp.int32)) - 1) % 2).astype(jnp.int32)
    nxt_first = lax.cummin(jnp.where(first, b, nb), axis=0, reverse=True)
    nxt_first = jnp.concatenate([nxt_first[1:], jnp.full((1,), nb, jnp.int32)])
    nxt = jnp.where(nxt_first < nb, blk_e[jnp.minimum(nxt_first, nb - 1)], -1).astype(jnp.int32)
    first = first.astype(jnp.int32)
    up_spec = pltpu.PrefetchScalarGridSpec(
        num_scalar_prefetch=7, grid=(nb,),
        in_specs=[pl.BlockSpec(memory_space=pl.ANY)] * 3,
        out_specs=pl.BlockSpec((blk, De), lambda i, *_: (i, 0)),
        scratch_shapes=[pltpu.VMEM((2, blk * pitch, LANES), jnp.int32),
                        pltpu.VMEM((2, D, De), F32), pltpu.VMEM((2, D, De), F32),
                        pltpu.SemaphoreType.DMA((2,)), pltpu.SemaphoreType.DMA((2, 2))])
    hdn = pl.pallas_call(
        functools.partial(_expert_up_kernel, blk=blk, nch=nch, pitch=pitch,
                          kch=min(1024, D)),
        grid_spec=up_spec,
        out_shape=jax.ShapeDtypeStruct((nb * blk, De), BF16),
        compiler_params=_cparams(("arbitrary",)), name="experts_up")(
            blk_e, nused, first, par, nxt, src_start, tok_rows, xn3, w1, w3)
    down_spec = pltpu.PrefetchScalarGridSpec(
        num_scalar_prefetch=5, grid=(nb,),
        in_specs=[pl.BlockSpec((blk, De), lambda i, *_: (i, 0)),
                  pl.BlockSpec(memory_space=pl.ANY)],
        out_specs=pl.BlockSpec((blk * pitch, LANES), lambda i, *_: (i, 0)),
        scratch_shapes=[pltpu.VMEM((2, De, D), F32), pltpu.SemaphoreType.DMA((2,))])
    return pl.pallas_call(
        functools.partial(_expert_down_kernel, blk=blk, nch=nch, pitch=pitch, ych=min(1024, D)),
        grid_spec=down_spec,
        out_shape=jax.ShapeDtypeStruct((nb * blk * pitch, LANES), jnp.int32),
        compiler_params=_cparams(("arbitrary",)), name="experts_down")(
            blk_e, nused, first, par, nxt, hdn, w2)


def _combine_kernel(pos_ref, y_hbm, x_ref, rt_ref, g_ref, x2_ref, xn_ref, ybuf, sem, *, tc, nch,
                    pitch):
    i = pl.program_id(0)
    n = pl.num_programs(0)

    def row_copy(step, r, slot):
        return pltpu.make_async_copy(
            y_hbm.at[pl.ds(pl.multiple_of(pos_ref[step * 2 * tc + r], 8), nch), :],
            ybuf.at[slot, pl.ds(pl.multiple_of(r * pitch, 8), nch), :], sem.at[slot])

    def gather(step, slot):
        return _gather_rows(lambda r: row_copy(step, r, slot).start(),
                            lambda r: row_copy(step, r, slot).wait(), 2 * tc)

    @pl.when(i == 0)
    def _():
        gather(0, 0)[0]()

    @pl.when(i + 1 < n)
    def _():
        gather(i + 1, (i + 1) % 2)[0]()

    slot = i % 2
    gather(i, slot)[1]()
    y0, y1 = [], []
    for c in range(nch):
        y0 += _unpack_bf16_pair(ybuf[slot, pl.ds(c, tc, stride=pitch), :])
        y1 += _unpack_bf16_pair(ybuf[slot, pl.ds(tc * pitch + c, tc, stride=pitch), :])
    y0 = jnp.concatenate(y0, axis=1)
    y1 = jnp.concatenate(y1, axis=1)
    x2 = x_ref[...] + rt_ref[:, 2:3] * y0 + rt_ref[:, 3:4] * y1
    x2_ref[...] = x2
    xn_ref[...] = _rms(x2, g_ref[...]).astype(xn_ref.dtype)


def _combine(y, x, route, g, pos, tc=256):
    T, D = x.shape
    tc = min(tc, T)
    nch = D // (2 * LANES)
    pitch = _row_pitch(nch)
    grid_spec = pltpu.PrefetchScalarGridSpec(
        num_scalar_prefetch=1, grid=(T // tc,),
        in_specs=[pl.BlockSpec(memory_space=pl.ANY),
                  pl.BlockSpec((tc, D), lambda i, ps: (i, 0)),
                  pl.BlockSpec((tc, LANES), lambda i, ps: (i, 0)),
                  pl.BlockSpec((1, D), lambda i, ps: (0, 0))],
        out_specs=[pl.BlockSpec((tc, D), lambda i, ps: (i, 0)),
                   pl.BlockSpec((tc, D), lambda i, ps: (i, 0))],
        scratch_shapes=[pltpu.VMEM((2, 2 * tc * pitch, LANES), jnp.int32),
                        pltpu.SemaphoreType.DMA((2,))])
    pos_tiled = pos.reshape(T // tc, tc, 2).transpose(0, 2, 1).reshape(-1) * pitch
    return pl.pallas_call(
        functools.partial(_combine_kernel, tc=tc, nch=nch, pitch=pitch), grid_spec=grid_spec,
        out_shape=[jax.ShapeDtypeStruct((T, D), F32), jax.ShapeDtypeStruct((T, D), BF16)],
        compiler_params=_cparams(("arbitrary",)), name="combine")(
            pos_tiled, y, x, route, g.reshape(1, D))


def _ple_kernel(xn_ref, wg_ref, p_ref, wp_ref, x_ref, o_ref, wg_bf, wp_bf):
    _cache_bf16(wg_ref, wg_bf)
    _cache_bf16(wp_ref, wp_bf)
    gate = _sigmoid(jnp.dot(xn_ref[...], wg_bf[...], preferred_element_type=F32))
    emb = jnp.dot(p_ref[...].astype(BF16), wp_bf[...], preferred_element_type=F32)
    o_ref[...] = x_ref[...] + gate * emb


def _ple(xn, w_pg, p, w_ple, x, tm=1024, tn=512):
    T, D = xn.shape
    Pd = p.shape[1]
    tm, tn = min(tm, T), min(tn, D)
    return pl.pallas_call(
        _ple_kernel, grid=(D // tn, T // tm),
        in_specs=[pl.BlockSpec((tm, D), lambda j, i: (i, 0)),
                  pl.BlockSpec((D, tn), lambda j, i: (0, j)),
                  pl.BlockSpec((tm, Pd), lambda j, i: (i, 0)),
                  pl.BlockSpec((Pd, tn), lambda j, i: (0, j)),
                  pl.BlockSpec((tm, tn), lambda j, i: (i, j))],
        out_specs=pl.BlockSpec((tm, tn), lambda j, i: (i, j)),
        out_shape=jax.ShapeDtypeStruct((T, D), F32),
        scratch_shapes=[pltpu.VMEM((D, tn), BF16), pltpu.VMEM((Pd, tn), BF16)],
        compiler_params=_cparams(("parallel", "arbitrary")), name="ple")(xn, w_pg, p, w_ple, x)


def _dispatch_tables(route, blk):
    T = route.shape[0]
    A = 2 * T
    eid = route[:, 0:2].astype(jnp.int32).reshape(-1)
    iota = jnp.arange(A, dtype=jnp.int32)
    experts = jnp.arange(N_EXPERTS, dtype=jnp.int32)
    _, order = lax.sort((eid, iota), num_keys=1)
    _, inv = lax.sort((order, iota), num_keys=1)
    onehot = eid[:, None] == experts[None, :]
    counts = jnp.sum(onehot, axis=0, dtype=jnp.int32)
    starts = jnp.cumsum(counts) - counts
    nblk = (counts + blk - 1) // blk
    bends = jnp.cumsum(nblk)
    bstarts = bends - nblk
    nb = (A + N_EXPERTS * blk) // blk
    b = jnp.arange(nb, dtype=jnp.int32)
    blk_e = jnp.minimum(jnp.sum(b[:, None] >= bends[None, :], axis=1),
                        N_EXPERTS - 1).astype(jnp.int32)
    shift = starts - bstarts * blk
    src_start = b * blk + jnp.sum(
        jnp.where(blk_e[:, None] == experts[None, :], shift[None, :], 0), axis=1)
    nused = bends[-1].astype(jnp.int32).reshape(1)
    pos = inv - jnp.sum(jnp.where(onehot, shift[None, :], 0), axis=1)
    return blk_e, nused, src_start.astype(jnp.int32), order // 2, pos.reshape(T, 2)


def _layer(x2d, p2d, pos_col, B, S, g_mix, w_in, b_mgate, g_mhead, g_cq, w_uq, g_ckv, w_ukv,
           g_qn, g_kn, w_bm, w_ba, w_out, g_ffn, w_rg, b_rg, w_re, b_re, w_e1, w_e3, w_e2,
           g_ple, w_pg, w_ple):
    T, D = x2d.shape
    H = M_HEADS
    mqk, mv = H * M_QK_HD, H * M_V_HD
    o_gate = 2 * mqk + 2 * mv
    o_cq = o_gate + 4 * H
    o_ckv = o_cq + Q_LORA
    o_kpe = o_ckv + KV_LORA
    o_br = o_kpe + A_ROPE
    w_t = jnp.swapaxes(w_in, 0, 1)
    w_mid_t = lax.optimization_barrier(w_t[o_gate:o_br])
    w_cq = w_mid_t[o_cq - o_gate:o_ckv - o_gate].T.astype(BF16)
    w_ckv = w_mid_t[o_ckv - o_gate:o_kpe - o_gate].T.astype(BF16)
    w_sm = jnp.concatenate(
        [w_mid_t[o_kpe - o_gate:], w_mid_t[:4 * H],
         jnp.zeros((LANES - A_ROPE - 4 * H, D), F32)], axis=0).T.astype(BF16)

    h = _rmsnorm(x2d, g_mix)
    z_a = _matmul_t(h, w_t, 0, o_gate, BF16, name="inproj_mlstm")
    z_br = _matmul_t(h, w_t, o_br, 2 * D, BF16, name="inproj_gates")
    cq_n, ckv_n, small = _inproj_small(h, w_cq, w_ckv, w_sm, g_cq, g_ckv)

    g_col = small[:, A_ROPE:A_ROPE + 4 * H] + b_mgate[None, :]
    g_row = g_col.reshape(B, S, 4 * H).transpose(0, 2, 1)
    hb = _mlstm(z_a, g_col, g_row, B, S, reverse=True)
    hm = _mlstm(z_a, g_col, g_row, B, S, reverse=False, hb=hb, g_mhead=g_mhead)

    pad_q = A_QK_PAD - A_QK_HD
    w_uq_p = jnp.pad(w_uq.reshape(Q_LORA, A_HEADS, A_QK_HD).transpose(1, 0, 2),
                     ((0, 0), (0, 0), (0, pad_q))).astype(BF16)
    w_ukv_h = w_ukv.reshape(KV_LORA, A_HEADS, A_NOPE + A_V_HD).transpose(1, 0, 2).astype(BF16)
    g_qn_p = jnp.pad(g_qn, (0, pad_q)).reshape(1, A_QK_PAD)
    g_kn_p = jnp.pad(g_kn, (0, pad_q)).reshape(1, A_QK_PAD)
    inv = ROPE_THETA ** (-jnp.arange(0, A_ROPE, 2, dtype=F32) / A_ROPE)
    inv_row = jnp.concatenate([inv, inv, jnp.zeros((LANES - A_ROPE,), F32)]).reshape(1, LANES)
    cos_t, sin_t = _rope_tables(pos_col, inv_row)
    q = _q_prep(cq_n, w_uq_p, g_qn_p, cos_t, sin_t)
    kt, v = _kv_prep(ckv_n, w_ukv_h, small, g_kn_p, cos_t, sin_t)
    ha = _attention(q, kt, v, B, S)

    mix = _merge(hm, ha, w_bm, w_ba, z_br, 0)
    x1 = _matmul_residual(mix, w_out, x2d)

    w_r = jnp.concatenate([w_rg, w_re, jnp.zeros((D, LANES - N_GROUPS - N_EXPERTS), F32)],
                          axis=1).astype(BF16)
    b_r = jnp.concatenate([b_rg, b_re, jnp.zeros((LANES - N_GROUPS - N_EXPERTS,), F32)]
                          ).reshape(1, LANES)
    route, xn3 = _router(x1, g_ffn, w_r, b_r)
    blk = min(MOE_BLK, T)
    blk_e, nused, src_start, tok_sorted, pos = _dispatch_tables(route, blk)
    y = _experts(xn3, w_e1, w_e3, w_e2, blk_e, nused, src_start, tok_sorted, blk)
    x2, xn2 = _combine(y, x1, route, g_ple, pos)

    return _ple(xn2, w_pg, p2d, w_ple, x2)


def kernel(x, p, positions, g_mix, w_in, b_mgate, g_mhead, g_cq, w_uq, g_ckv, w_ukv, g_qn, g_kn,
           w_bm, w_ba, w_out, g_ffn, w_rg, b_rg, w_re, b_re, w_e1, w_e3, w_e2, g_ple, w_pg, w_ple):
    B, S, D = x.shape
    T = B * S
    x2d = x.reshape(T, D)
    pos_col = positions.reshape(T, 1).astype(jnp.int32)
    for l in range(p.shape[0]):
        x2d = _layer(x2d, p[l].reshape(T, -1), pos_col, B, S, g_mix[l], w_in[l], b_mgate[l],
                     g_mhead[l], g_cq[l], w_uq[l], g_ckv[l], w_ukv[l], g_qn[l], g_kn[l],
                     w_bm[l], w_ba[l], w_out[l], g_ffn[l], w_rg[l], b_rg[l], w_re[l], b_re[l],
                     w_e1[l], w_e3[l], w_e2[l], g_ple[l], w_pg[l], w_ple[l])
    return x2d.reshape(B, S, D)
```

```python
import functools
import math

import jax
import jax.numpy as jnp
from jax import lax
from jax.experimental import pallas as pl
from jax.experimental.pallas import tpu as pltpu

F32 = jnp.float32
BF16 = jnp.bfloat16

EPS = 1e-6
M_HEADS = 8
M_QK_HD = 128
M_V_HD = 256
A_HEADS = 16
A_NOPE = 128
A_ROPE = 64
A_QK_HD = A_NOPE + A_ROPE
A_V_HD = 128
A_QK_PAD = 256
Q_LORA = 1024
KV_LORA = 512
ROPE_THETA = 10000.0
N_GROUPS = 8
E_PER_GROUP = 8
N_EXPERTS = N_GROUPS * E_PER_GROUP
D_EXPERT = 512
LANES = 128

M_CHUNK = 256
MOE_BLK = 256
ATT_TQ = 1024
ATT_TK = 256
VMEM_LIMIT = 56 * 1024 * 1024
NEG = -1e30
LOG2E = 1.4426950408889634


def _cparams(sem):
    return pltpu.CompilerParams(dimension_semantics=sem, vmem_limit_bytes=VMEM_LIMIT)


def _sigmoid(x):
    return 1.0 / (1.0 + jnp.exp(-x))


def _log_sigmoid(x):
    return jnp.minimum(x, 0.0) - jnp.log(1.0 + jnp.exp(-jnp.abs(x)))


def _rms(xf, g):
    return xf * lax.rsqrt(jnp.mean(xf * xf, axis=-1, keepdims=True) + EPS) * g


def _norm_kernel(x_ref, g_ref, o_ref):
    o_ref[...] = _rms(x_ref[...], g_ref[...]).astype(o_ref.dtype)


def _rmsnorm(x, g, tm=512):
    M, D = x.shape
    tm = min(tm, M)
    return pl.pallas_call(
        _norm_kernel, grid=(M // tm,),
        in_specs=[pl.BlockSpec((tm, D), lambda i: (i, 0)),
                  pl.BlockSpec((1, D), lambda i: (0, 0))],
        out_specs=pl.BlockSpec((tm, D), lambda i: (i, 0)),
        out_shape=jax.ShapeDtypeStruct((M, D), BF16),
        compiler_params=_cparams(("parallel",)), name="rmsnorm")(x, g.reshape(1, D))


def _cache_bf16(w_ref, wbf_ref):
    @pl.when(pl.program_id(1) == 0)
    def _():
        wbf_ref[...] = w_ref[...].astype(BF16)


def _mm_t_kernel(a_ref, wt_hbm, o_ref, wst, wbf_ref, sem, *, row0, tn):
    j = pl.program_id(0)
    i = pl.program_id(1)

    def wcopy(jj):
        return pltpu.make_async_copy(
            wt_hbm.at[pl.ds(pl.multiple_of(row0 + jj * tn, 8), tn), :], wst, sem)

    @pl.when((j == 0) & (i == 0))
    def _():
        wcopy(0).start()

    @pl.when(i == 0)
    def _():
        wcopy(j).wait()
        wbf_ref[...] = wst[...].astype(BF16)

        @pl.when(j + 1 < pl.num_programs(0))
        def _():
            wcopy(j + 1).start()

    o_ref[...] = lax.dot_general(a_ref[...], wbf_ref[...], (((1,), (1,)), ((), ())),
                                 preferred_element_type=F32).astype(o_ref.dtype)


def _matmul_t(a, w_t, row0, n_cols, out_dtype, tm=1024, tn=1024, name="matmul"):
    M, K = a.shape
    tm, tn = min(tm, M), min(tn, n_cols)
    return pl.pallas_call(
        functools.partial(_mm_t_kernel, row0=row0, tn=tn), grid=(n_cols // tn, M // tm),
        in_specs=[pl.BlockSpec((tm, K), lambda j, i: (i, 0)),
                  pl.BlockSpec(memory_space=pl.ANY)],
        out_specs=pl.BlockSpec((tm, tn), lambda j, i: (i, j)),
        out_shape=jax.ShapeDtypeStruct((M, n_cols), out_dtype),
        scratch_shapes=[pltpu.VMEM((tn, K), F32), pltpu.VMEM((tn, K), BF16),
                        pltpu.SemaphoreType.DMA(())],
        compiler_params=_cparams(("arbitrary", "arbitrary")), name=name)(a, w_t)


def _inproj_small_kernel(h_ref, wq_ref, wkv_ref, ws_ref, gq_ref, gkv_ref,
                         cq_ref, ckv_ref, sm_ref):
    h = h_ref[...]
    cq = jnp.dot(h, wq_ref[...], preferred_element_type=F32)
    cq_ref[...] = _rms(cq, gq_ref[...]).astype(cq_ref.dtype)
    ckv = jnp.dot(h, wkv_ref[...], preferred_element_type=F32)
    ckv_ref[...] = _rms(ckv, gkv_ref[...]).astype(ckv_ref.dtype)
    sm_ref[...] = jnp.dot(h, ws_ref[...], preferred_element_type=F32)


def _inproj_small(h, w_cq, w_ckv, w_sm, g_cq, g_ckv, tm=512):
    M, D = h.shape
    tm = min(tm, M)
    nq, nkv, ns = w_cq.shape[1], w_ckv.shape[1], w_sm.shape[1]
    full = lambda n: pl.BlockSpec((D, n), lambda i: (0, 0))
    row = lambda n: pl.BlockSpec((tm, n), lambda i: (i, 0))
    return pl.pallas_call(
        _inproj_small_kernel, grid=(M // tm,),
        in_specs=[row(D), full(nq), full(nkv), full(ns),
                  pl.BlockSpec((1, nq), lambda i: (0, 0)),
                  pl.BlockSpec((1, nkv), lambda i: (0, 0))],
        out_specs=[row(nq), row(nkv), row(ns)],
        out_shape=[jax.ShapeDtypeStruct((M, nq), BF16),
                   jax.ShapeDtypeStruct((M, nkv), BF16),
                   jax.ShapeDtypeStruct((M, ns), F32)],
        compiler_params=_cparams(("parallel",)), name="inproj_small")(
            h, w_cq, w_ckv, w_sm, g_cq.reshape(1, nq), g_ckv.reshape(1, nkv))


def _mlstm_kernel(*refs, reverse, finish, L, goff):
    if finish:
        (q_ref, k_ref, v_ref, gc_ref, gr_ref, hb_ref, o_ref, gm_ref,
         out_ref, C_ref, n_ref, m_ref) = refs
    else:
        q_ref, k_ref, v_ref, gc_ref, gr_ref, out_ref, C_ref, n_ref, m_ref = refs
    H, dk, dv = M_HEADS, M_QK_HD, M_V_HD
    log_scale = -0.5 * math.log(dk)

    @pl.when(pl.program_id(1) == 0)
    def _():
        C_ref[...] = jnp.zeros_like(C_ref)
        n_ref[...] = jnp.zeros_like(n_ref)
        m_ref[...] = jnp.zeros_like(m_ref)

    t_idx = lax.broadcasted_iota(jnp.int32, (L, L), 0)
    s_idx = lax.broadcasted_iota(jnp.int32, (L, L), 1)
    if reverse:
        mask, mask_t = s_idx >= t_idx, t_idx >= s_idx
    else:
        mask, mask_t = s_idx <= t_idx, t_idx <= s_idx

    hs = range(H)
    gates = []
    for h in hs:
        i_col = gc_ref[:, goff + h:goff + h + 1]
        f_col = gc_ref[:, goff + H + h:goff + H + h + 1]
        i_row = gr_ref[goff + h:goff + h + 1, :]
        f_row = gr_ref[goff + H + h:goff + H + h + 1, :]
        lf_col = _log_sigmoid(f_col)
        lf_row = _log_sigmoid(f_row)
        b_col = jnp.sum(jnp.where(mask, lf_row, 0.0), axis=1, keepdims=True)
        b_row = jnp.sum(jnp.where(mask_t, lf_col, 0.0), axis=0, keepdims=True)
        g = jnp.sum(lf_row, axis=1, keepdims=True)
        gates.append((i_col, i_row, b_col, b_row, g, m_ref[h][:, 0:1]))

    stab = []
    for h in hs:
        i_col, i_row, b_col, b_row, g, m_prev = gates[h]
        dmat = jnp.where(mask, b_col - b_row + i_row, NEG)
        a = b_col + m_prev
        m_t = jnp.maximum(a, jnp.max(dmat, axis=1, keepdims=True))
        stab.append((dmat, a, m_t))

    qs = [q_ref[:, h * dk:(h + 1) * dk] for h in hs]
    ks = [k_ref[:, h * dk:(h + 1) * dk] for h in hs]
    vs = [v_ref[:, h * dv:(h + 1) * dv] for h in hs]
    qks = [lax.dot_general(qs[h], ks[h], (((1,), (1,)), ((), ())),
                           preferred_element_type=F32) for h in hs]

    intra = []
    for h in hs:
        dmat, a, m_t = stab[h]
        m_s = m_t - log_scale
        intra.append((jnp.exp(dmat - m_s) * qks[h], jnp.exp(a - m_s)))

    c_states = [C_ref[h] for h in hs]
    n_states = [n_ref[h] for h in hs]
    outs = []
    for h in hs:
        w_intra, w_inter = intra[h]
        q_c = jnp.dot(qs[h], c_states[h].astype(BF16), preferred_element_type=F32)
        num = jnp.dot(w_intra.astype(BF16), vs[h], preferred_element_type=F32) + w_inter * q_c
        q_n = jnp.sum(qs[h].astype(F32) * n_states[h], axis=1, keepdims=True)
        den = jnp.sum(w_intra, axis=1, keepdims=True) + w_inter * q_n
        outs.append(num / jnp.maximum(jnp.abs(den), jnp.exp(-stab[h][2])))

    for h in hs:
        i_col, i_row, b_col, b_row, g, m_prev = gates[h]
        w_col = g - b_col + i_col
        m_new = jnp.maximum(g + m_prev, jnp.max(w_col, axis=0, keepdims=True))
        decay = jnp.exp(g + m_prev - m_new)
        kw = ks[h].astype(F32) * jnp.exp(w_col - m_new)
        C_ref[h] = decay * c_states[h] + lax.dot_general(
            kw.astype(BF16), vs[h], (((0,), (0,)), ((), ())), preferred_element_type=F32)
        n_ref[h] = decay * n_states[h] + jnp.sum(kw, axis=0, keepdims=True)
        m_ref[h] = jnp.broadcast_to(m_new, (1, LANES))

    for h in hs:
        if finish:
            tot = outs[h] + hb_ref[:, h * dv:(h + 1) * dv]
            y = _rms(tot, gm_ref[h:h + 1, :])
            gate = _sigmoid(o_ref[:, h * dv:(h + 1) * dv].astype(F32))
            out_ref[:, h * dv:(h + 1) * dv] = (y * gate).astype(out_ref.dtype)
        else:
            out_ref[:, h * dv:(h + 1) * dv] = outs[h]


def _mlstm(z_a, g_col, g_row, B, S, *, reverse, hb=None, g_mhead=None):
    H, dk, dv = M_HEADS, M_QK_HD, M_V_HD
    L = min(M_CHUNK, S)
    nc = S // L
    T = B * S
    finish = not reverse
    if reverse:
        cidx = lambda b, c: b * nc + (nc - 1 - c)
        ridx = lambda b, c: nc - 1 - c
    else:
        cidx = lambda b, c: b * nc + c
        ridx = lambda b, c: c
    mqk, mv = H * dk, H * dv
    in_specs = [
        pl.BlockSpec((L, mqk), lambda b, c: (cidx(b, c), 0)),
        pl.BlockSpec((L, mqk), lambda b, c: (cidx(b, c), 1)),
        pl.BlockSpec((L, mv), lambda b, c: (cidx(b, c), 1)),
        pl.BlockSpec((L, 4 * H), lambda b, c: (cidx(b, c), 0)),
        pl.BlockSpec((None, 4 * H, L), lambda b, c: (b, 0, ridx(b, c))),
    ]
    args = [z_a, z_a, z_a, g_col, g_row]
    if finish:
        in_specs += [
            pl.BlockSpec((L, mv), lambda b, c: (cidx(b, c), 0)),
            pl.BlockSpec((L, mv), lambda b, c: (cidx(b, c), 2)),
            pl.BlockSpec((H, dv), lambda b, c: (0, 0)),
        ]
        args += [hb, z_a, g_mhead.reshape(H, dv)]
    kern = functools.partial(_mlstm_kernel, reverse=reverse, finish=finish, L=L,
                             goff=2 * H if reverse else 0)
    return pl.pallas_call(
        kern, grid=(B, nc), in_specs=in_specs,
        out_specs=pl.BlockSpec((L, mv), lambda b, c: (cidx(b, c), 0)),
        out_shape=jax.ShapeDtypeStruct((T, mv), BF16 if finish else F32),
        scratch_shapes=[pltpu.VMEM((H, dk, dv), F32), pltpu.VMEM((H, 1, dk), F32),
                        pltpu.VMEM((H, 1, LANES), F32)],
        compiler_params=_cparams(("parallel", "arbitrary")),
        name="mlstm_bwd" if reverse else "mlstm_fwd")(*args)


def _rope_table_kernel(pos_ref, inv_ref, cos_ref, sin_ref):
    ang = pos_ref[...].astype(F32) * inv_ref[...]
    lane = lax.broadcasted_iota(jnp.int32, ang.shape, 1)
    cos_ref[...] = jnp.cos(ang)
    sin_ref[...] = jnp.where(lane < A_ROPE // 2, -jnp.sin(ang), jnp.sin(ang))


def _rope_tables(pos_col, inv_row, tm=1024):
    T = pos_col.shape[0]
    tm = min(tm, T)
    return pl.pallas_call(
        _rope_table_kernel, grid=(T // tm,),
        in_specs=[pl.BlockSpec((tm, 1), lambda i: (i, 0)),
                  pl.BlockSpec((1, LANES), lambda i: (0, 0))],
        out_specs=[pl.BlockSpec((tm, LANES), lambda i: (i, 0))] * 2,
        out_shape=[jax.ShapeDtypeStruct((T, LANES), F32)] * 2,
        compiler_params=_cparams(("parallel",)), name="rope_tables")(pos_col, inv_row)


def _rope(r, cos_ref, sin_ref):
    lane = lax.broadcasted_iota(jnp.int32, r.shape, 1)
    half = A_ROPE // 2
    swapped = jnp.where(lane < half, pltpu.roll(r, LANES - half, 1), pltpu.roll(r, half, 1))
    return r * cos_ref[...] + swapped * sin_ref[...]


PREP_HEADS = 8


def _q_prep_kernel(c_ref, w_ref, g_ref, cos_ref, sin_ref, o_ref):
    c = c_ref[...]
    hs = range(w_ref.shape[0])
    accs = [jnp.dot(c, w_ref[h], preferred_element_type=F32) for h in hs]
    rs = [lax.rsqrt(jnp.sum(a * a, axis=1, keepdims=True) / A_QK_HD + EPS) for a in accs]
    gain = g_ref[...] * (A_QK_HD ** -0.5 * LOG2E)
    ys = [accs[h] * rs[h] * gain for h in hs]
    for h in hs:
        o_ref[h, :, 0:A_NOPE] = ys[h][:, 0:A_NOPE].astype(o_ref.dtype)
    for h in hs:
        o_ref[h, :, A_NOPE:] = _rope(ys[h][:, A_NOPE:], cos_ref, sin_ref).astype(o_ref.dtype)


def _q_prep(cq_n, w_uq_p, g_qn_p, cos_t, sin_t, tm=1024):
    T, R = cq_n.shape
    tm = min(tm, T)
    H, hb = A_HEADS, PREP_HEADS
    return pl.pallas_call(
        _q_prep_kernel, grid=(T // tm, H // hb),
        in_specs=[pl.BlockSpec((tm, R), lambda i, h: (i, 0)),
                  pl.BlockSpec((hb, R, A_QK_PAD), lambda i, h: (h, 0, 0)),
                  pl.BlockSpec((1, A_QK_PAD), lambda i, h: (0, 0)),
                  pl.BlockSpec((tm, LANES), lambda i, h: (i, 0)),
                  pl.BlockSpec((tm, LANES), lambda i, h: (i, 0))],
        out_specs=pl.BlockSpec((hb, tm, A_QK_PAD), lambda i, h: (h, i, 0)),
        out_shape=jax.ShapeDtypeStruct((H, T, A_QK_PAD), BF16),
        compiler_params=_cparams(("parallel", "parallel")), name="mla_q_prep")(
            cq_n, w_uq_p, g_qn_p, cos_t, sin_t)


def _kv_prep_kernel(c_ref, w_ref, sm_ref, g_ref, cos_ref, sin_ref, kt_ref, v_ref):
    c = c_ref[...]
    lane = lax.broadcasted_iota(jnp.int32, sm_ref.shape, 1)
    k_pe = jnp.where(lane < A_ROPE, sm_ref[...], 0.0)
    ss_pe = jnp.sum(k_pe * k_pe, axis=1, keepdims=True)
    hs = range(w_ref.shape[0])
    accs = [jnp.dot(c, w_ref[h], preferred_element_type=F32) for h in hs]
    for h in hs:
        v_ref[h, :, 0:A_V_HD] = accs[h][:, A_NOPE:].astype(v_ref.dtype)
        v_ref[h, :, A_V_HD:] = jnp.ones((c.shape[0], A_V_HD), v_ref.dtype)
    rs = [lax.rsqrt((jnp.sum(a[:, 0:A_NOPE] * a[:, 0:A_NOPE], axis=1, keepdims=True) + ss_pe)
                    / A_QK_HD + EPS) for a in accs]
    kns = [accs[h][:, 0:A_NOPE] * rs[h] * g_ref[:, 0:A_NOPE] for h in hs]
    krs = [_rope(k_pe * rs[h] * g_ref[:, A_NOPE:], cos_ref, sin_ref) for h in hs]
    for h in hs:
        kt_ref[h, 0:A_NOPE, :] = kns[h].T.astype(kt_ref.dtype)
    for h in hs:
        kt_ref[h, A_NOPE:, :] = krs[h].T.astype(kt_ref.dtype)


def _kv_prep(ckv_n, w_ukv_h, small, g_kn_p, cos_t, sin_t, tm=1024):
    T, R = ckv_n.shape
    tm = min(tm, T)
    H, hb = A_HEADS, PREP_HEADS
    return pl.pallas_call(
        _kv_prep_kernel, grid=(T // tm, H // hb),
        in_specs=[pl.BlockSpec((tm, R), lambda i, h: (i, 0)),
                  pl.BlockSpec((hb, R, A_NOPE + A_V_HD), lambda i, h: (h, 0, 0)),
                  pl.BlockSpec((tm, LANES), lambda i, h: (i, 0)),
                  pl.BlockSpec((1, A_QK_PAD), lambda i, h: (0, 0)),
                  pl.BlockSpec((tm, LANES), lambda i, h: (i, 0)),
                  pl.BlockSpec((tm, LANES), lambda i, h: (i, 0))],
        out_specs=[pl.BlockSpec((hb, A_QK_PAD, tm), lambda i, h: (h, 0, i)),
                   pl.BlockSpec((hb, tm, 2 * A_V_HD), lambda i, h: (h, i, 0))],
        out_shape=[jax.ShapeDtypeStruct((H, A_QK_PAD, T), BF16),
                   jax.ShapeDtypeStruct((H, T, 2 * A_V_HD), BF16)],
        compiler_params=_cparams(("parallel", "parallel")), name="mla_kv_prep")(
            ckv_n, w_ukv_h, small, g_kn_p, cos_t, sin_t)


def _attn_kernel(q_ref, kt_ref, v_ref, o_ref, *, tk):
    q = q_ref[...]
    S = kt_ref.shape[1]
    m = acc = None
    for j in range(S // tk):
        s = jnp.dot(q, kt_ref[:, j * tk:(j + 1) * tk], preferred_element_type=F32)
        mj = jnp.max(s, axis=1, keepdims=True)
        vj = v_ref[j * tk:(j + 1) * tk, :]
        if j == 0:
            m = mj
            acc = jnp.dot(jnp.exp2(s - m).astype(BF16), vj, preferred_element_type=F32)
        else:
            m_new = jnp.maximum(m, mj)
            acc = (jnp.exp2(m - m_new) * acc
                   + jnp.dot(jnp.exp2(s - m_new).astype(BF16), vj, preferred_element_type=F32))
            m = m_new
    o_ref[...] = (acc[:, 0:A_V_HD] / acc[:, A_V_HD:A_V_HD + 1]).astype(o_ref.dtype)


def _attention(q, kt, v, B, S):
    H, T, dq = q.shape
    tq, tk = min(ATT_TQ, S), min(ATT_TK, S)
    nq = S // tq
    return pl.pallas_call(
        functools.partial(_attn_kernel, tk=tk), grid=(B, H, nq),
        in_specs=[pl.BlockSpec((None, tq, dq), lambda b, h, i: (h, b * nq + i, 0)),
                  pl.BlockSpec((None, dq, S), lambda b, h, i: (h, 0, b)),
                  pl.BlockSpec((None, S, 2 * A_V_HD), lambda b, h, i: (h, b, 0))],
        out_specs=pl.BlockSpec((tq, A_V_HD), lambda b, h, i: (b * nq + i, h)),
        out_shape=jax.ShapeDtypeStruct((T, H * A_V_HD), BF16),
        compiler_params=_cparams(("parallel", "parallel", "parallel")), name="mla_attention")(
            q, kt, v)


def _merge_kernel(hm_ref, ha_ref, wm_ref, wa_ref, bm_ref, ba_ref, o_ref, wm_bf, wa_bf):
    _cache_bf16(wm_ref, wm_bf)
    _cache_bf16(wa_ref, wa_bf)
    ym = jnp.dot(hm_ref[...], wm_bf[...], preferred_element_type=F32)
    ya = jnp.dot(ha_ref[...], wa_bf[...], preferred_element_type=F32)
    o_ref[...] = (_sigmoid(bm_ref[...].astype(F32)) * ym
                  + _sigmoid(ba_ref[...].astype(F32)) * ya).astype(o_ref.dtype)


def _merge(hm, ha, w_bm, w_ba, z_a, br_off, tm=1024, tn=512):
    T, K = hm.shape
    D = w_bm.shape[1]
    tm, tn = min(tm, T), min(tn, D)
    ob = br_off // tn
    return pl.pallas_call(
        _merge_kernel, grid=(D // tn, T // tm),
        in_specs=[pl.BlockSpec((tm, K), lambda j, i: (i, 0)),
                  pl.BlockSpec((tm, K), lambda j, i: (i, 0)),
                  pl.BlockSpec((K, tn), lambda j, i: (0, j)),
                  pl.BlockSpec((K, tn), lambda j, i: (0, j)),
                  pl.BlockSpec((tm, tn), lambda j, i: (i, ob + j)),
                  pl.BlockSpec((tm, tn), lambda j, i: (i, ob + D // tn + j))],
        out_specs=pl.BlockSpec((tm, tn), lambda j, i: (i, j)),
        out_shape=jax.ShapeDtypeStruct((T, D), BF16),
        scratch_shapes=[pltpu.VMEM((K, tn), BF16), pltpu.VMEM((K, tn), BF16)],
        compiler_params=_cparams(("parallel", "arbitrary")), name="merge")(
            hm, ha, w_bm, w_ba, z_a, z_a)


def _mm_res_kernel(a_ref, w_ref, r_ref, o_ref, wbf_ref):
    _cache_bf16(w_ref, wbf_ref)
    o_ref[...] = r_ref[...] + jnp.dot(a_ref[...], wbf_ref[...], preferred_element_type=F32)


def _matmul_residual(a, b, res, tm=1024, tn=512):
    M, K = a.shape
    N = b.shape[1]
    tm, tn = min(tm, M), min(tn, N)
    return pl.pallas_call(
        _mm_res_kernel, grid=(N // tn, M // tm),
        in_specs=[pl.BlockSpec((tm, K), lambda j, i: (i, 0)),
                  pl.BlockSpec((K, tn), lambda j, i: (0, j)),
                  pl.BlockSpec((tm, tn), lambda j, i: (i, j))],
        out_specs=pl.BlockSpec((tm, tn), lambda j, i: (i, j)),
        out_shape=jax.ShapeDtypeStruct((M, N), F32),
        scratch_shapes=[pltpu.VMEM((K, tn), BF16)],
        compiler_params=_cparams(("parallel", "arbitrary")), name="out_proj")(a, b, res)


def _router_kernel(x_ref, g_ref, w_ref, b_ref, o_ref, xn3_ref):
    xf = _rms(x_ref[...], g_ref[...])
    nch = xf.shape[1] // (2 * LANES)
    tm = xf.shape[0]
    pitch = xn3_ref.shape[0] // tm
    for c in range(pitch):
        xn3_ref[pl.ds(c, tm, stride=pitch), :] = (
            _pack_bf16_pair(xf[:, 2 * c * LANES:(2 * c + 1) * LANES],
                            xf[:, (2 * c + 1) * LANES:(2 * c + 2) * LANES])
            if c < nch else jnp.zeros((tm, LANES), jnp.int32))
    xn = xf.astype(BF16)
    logits = jnp.dot(xn, w_ref[...], preferred_element_type=F32) + b_ref[...]
    lane = lax.broadcasted_iota(jnp.int32, logits.shape, 1)
    big = jnp.int32(1 << 20)
    lg = jnp.where(lane < N_GROUPS, logits, NEG)
    gmax = jnp.max(lg, axis=1, keepdims=True)
    g_idx = jnp.min(jnp.where(lg == gmax, lane, big), axis=1, keepdims=True)
    g_w = 1.0 / jnp.sum(jnp.exp(lg - gmax), axis=1, keepdims=True)
    lo = N_GROUPS + g_idx * E_PER_GROUP
    le = jnp.where((lane >= lo) & (lane < lo + E_PER_GROUP), logits, NEG)
    v1 = jnp.max(le, axis=1, keepdims=True)
    i1 = jnp.min(jnp.where(le == v1, lane, big), axis=1, keepdims=True)
    le2 = jnp.where(lane == i1, NEG, le)
    v2 = jnp.max(le2, axis=1, keepdims=True)
    i2 = jnp.min(jnp.where(le2 == v2, lane, big), axis=1, keepdims=True)
    e21 = jnp.exp(v2 - v1)
    w1 = g_w / (1.0 + e21)
    w2 = g_w * e21 / (1.0 + e21)
    o_ref[...] = jnp.where(
        lane == 0, (i1 - N_GROUPS).astype(F32),
        jnp.where(lane == 1, (i2 - N_GROUPS).astype(F32),
                  jnp.where(lane == 2, w1, jnp.where(lane == 3, w2, 0.0))))


def _router(x, g, w_r, b_r, tm=512):
    T, D = x.shape
    tm = min(tm, T)
    pitch = _row_pitch(D // (2 * LANES))
    return pl.pallas_call(
        _router_kernel, grid=(T // tm,),
        in_specs=[pl.BlockSpec((tm, D), lambda i: (i, 0)),
                  pl.BlockSpec((1, D), lambda i: (0, 0)),
                  pl.BlockSpec((D, LANES), lambda i: (0, 0)),
                  pl.BlockSpec((1, LANES), lambda i: (0, 0))],
        out_specs=[pl.BlockSpec((tm, LANES), lambda i: (i, 0)),
                   pl.BlockSpec((tm * pitch, LANES), lambda i: (i, 0))],
        out_shape=[jax.ShapeDtypeStruct((T, LANES), F32),
                   jax.ShapeDtypeStruct((T * pitch, LANES), jnp.int32)],
        compiler_params=_cparams(("parallel",)), name="router")(x, g.reshape(1, D), w_r, b_r)


def _pack_bf16_pair(a, b):
    ua = lax.bitcast_convert_type(a.astype(BF16).astype(F32), jnp.int32)
    ub = lax.bitcast_convert_type(b.astype(BF16).astype(F32), jnp.int32)
    return lax.shift_right_logical(ua, jnp.int32(16)) | (ub & jnp.int32(-65536))


def _unpack_bf16_pair(u):
    return [lax.bitcast_convert_type(lax.shift_left(u, jnp.int32(16)), F32),
            lax.bitcast_convert_type(u & jnp.int32(-65536), F32)]


def _row_pitch(nch):
    return (nch + 7) // 8 * 8 + 8


def _gather_rows(start_one, wait_one, n, unroll=8, split=False):
    def start():
        if split:
            def body(r, carry):
                start_one(2 * r, 0)
                start_one(2 * r + 1, 1)
                return carry
            lax.fori_loop(0, n // 2, body, 0, unroll=unroll // 2)
        else:
            def body(r, carry):
                start_one(r, 0)
                return carry
            lax.fori_loop(0, n, body, 0, unroll=unroll)

    def wait():
        def body(r, carry):
            wait_one(r)
            return carry
        lax.fori_loop(0, n, body, 0, unroll=unroll)
    return start, wait


def _weight_prefetch(i, nused, blk_e_ref, first_ref, par_ref, nxt_ref, copies, priority=0):
    @pl.when(i == 0)
    def _():
        for c in copies(blk_e_ref[0], 0):
            c.start(priority=priority)

    @pl.when((i < nused) & (first_ref[i] == 1))
    def _():
        slot = par_ref[i]
        for c in copies(blk_e_ref[i], slot):
            c.wait()

        @pl.when(nxt_ref[i] >= 0)
        def _():
            for c in copies(nxt_ref[i], 1 - slot):
                c.start(priority=priority)


def _expert_up_kernel(blk_e_ref, nused_ref, first_ref, par_ref, nxt_ref, src_ref, tok_ref,
                      x_hbm, w1_hbm, w3_hbm, h_ref, xbuf, wb1, wb3, sem, wsem,
                      *, blk, nch, pitch, kch):
    i = pl.program_id(0)
    nused = nused_ref[0]

    def row_copy(base, r, slot):
        return pltpu.make_async_copy(
            x_hbm.at[pl.ds(pl.multiple_of(tok_ref[base + r], 8), nch), :],
            xbuf.at[slot, pl.ds(pl.multiple_of(r * pitch, 8), nch), :], sem.at[slot])

    def gather(step, slot):
        base = src_ref[step]
        return _gather_rows(lambda r, p: row_copy(base, r, slot).start(priority=p),
                            lambda r: row_copy(base, r, slot).wait(), blk)

    def weight_copies(e, slot):
        return [pltpu.make_async_copy(w1_hbm.at[e], wb1.at[slot], wsem.at[0, slot]),
                pltpu.make_async_copy(w3_hbm.at[e], wb3.at[slot], wsem.at[1, slot])]

    @pl.when(i == 0)
    def _():
        gather(0, 0)[0]()

    @pl.when(i + 1 < nused)
    def _():
        gather(i + 1, (i + 1) % 2)[0]()

    _weight_prefetch(i, nused, blk_e_ref, first_ref, par_ref, nxt_ref, weight_copies, priority=1)

    @pl.when(i < nused)
    def _():
        slot = i % 2
        wslot = par_ref[i]
        gather(i, slot)[1]()
        parts = []
        for c in range(nch):
            parts += _unpack_bf16_pair(xbuf[slot, pl.ds(c, blk, stride=pitch), :])
        xn = jnp.concatenate([t.astype(BF16) for t in parts], axis=1)
        h1 = h3 = None
        for k in range(xn.shape[1] // kch):
            xk = xn[:, k * kch:(k + 1) * kch]
            d1 = jnp.dot(xk, wb1[wslot, k * kch:(k + 1) * kch, :].astype(BF16),
                         preferred_element_type=F32)
            d3 = jnp.dot(xk, wb3[wslot, k * kch:(k + 1) * kch, :].astype(BF16),
                         preferred_element_type=F32)
            h1 = d1 if h1 is None else h1 + d1
            h3 = d3 if h3 is None else h3 + d3
        h_ref[...] = (h1 * _sigmoid(h1) * h3).astype(h_ref.dtype)

    @pl.when(i >= nused)
    def _():
        h_ref[...] = jnp.zeros_like(h_ref)


def _expert_down_kernel(blk_e_ref, nused_ref, first_ref, par_ref, nxt_ref, h_ref, w2_hbm, y_ref,
                        wb2, wsem, *, blk, nch, pitch, ych):
    i = pl.program_id(0)
    nused = nused_ref[0]

    def weight_copies(e, slot):
        return [pltpu.make_async_copy(w2_hbm.at[e], wb2.at[slot], wsem.at[slot])]

    _weight_prefetch(i, nused, blk_e_ref, first_ref, par_ref, nxt_ref, weight_copies)

    @pl.when(i < nused)
    def _():
        wslot = par_ref[i]
        hdn = h_ref[...]
        for j in range(nch * 2 * LANES // ych):
            yj = jnp.dot(hdn, wb2[wslot, :, j * ych:(j + 1) * ych].astype(BF16),
                         preferred_element_type=F32)
            for c in range(ych // (2 * LANES)):
                y_ref[pl.ds(j * (ych // (2 * LANES)) + c, blk, stride=pitch), :] = (
                    _pack_bf16_pair(yj[:, 2 * c * LANES:(2 * c + 1) * LANES],
                                    yj[:, (2 * c + 1) * LANES:(2 * c + 2) * LANES]))
        for c in range(nch, pitch):
            y_ref[pl.ds(c, blk, stride=pitch), :] = jnp.zeros((blk, LANES), jnp.int32)

    @pl.when(i >= nused)
    def _():
        y_ref[...] = jnp.zeros_like(y_ref)


def _experts(xn3, w1, w3, w2, blk_e, nused, src_start, tok_sorted, blk):
    E, D, De = w1.shape
    nch = D // (2 * LANES)
    pitch = _row_pitch(nch)
    nb = blk_e.shape[0]
    tok_rows = jnp.concatenate([tok_sorted, jnp.broadcast_to(tok_sorted[-1:], (blk,))]) * pitch
    b = jnp.arange(nb, dtype=jnp.int32)
    first = (b < nused[0]) & (blk_e != jnp.concatenate([jnp.full((1,), -1, jnp.int32), blk_e[:-1]]))
    par = ((jnp.cumsum(first.astype(jnp.int32)) - 1) % 2).astype(jnp.int32)
    nxt_first = lax.cummin(jnp.where(first, b, nb), axis=0, reverse=True)
    nxt_first = jnp.concatenate([nxt_first[1:], jnp.full((1,), nb, jnp.int32)])
    nxt = jnp.where(nxt_first < nb, blk_e[jnp.minimum(nxt_first, nb - 1)], -1).astype(jnp.int32)
    first = first.astype(jnp.int32)
    up_spec = pltpu.PrefetchScalarGridSpec(
        num_scalar_prefetch=7, grid=(nb,),
        in_specs=[pl.BlockSpec(memory_space=pl.ANY)] * 3,
        out_specs=pl.BlockSpec((blk, De), lambda i, *_: (i, 0)),
        scratch_shapes=[pltpu.VMEM((2, blk * pitch, LANES), jnp.int32),
                        pltpu.VMEM((2, D, De), F32), pltpu.VMEM((2, D, De), F32),
                        pltpu.SemaphoreType.DMA((2,)), pltpu.SemaphoreType.DMA((2, 2))])
    hdn = pl.pallas_call(
        functools.partial(_expert_up_kernel, blk=blk, nch=nch, pitch=pitch,
                          kch=min(1024, D)),
        grid_spec=up_spec,
        out_shape=jax.ShapeDtypeStruct((nb * blk, De), BF16),
        compiler_params=_cparams(("arbitrary",)), name="experts_up")(
            blk_e, nused, first, par, nxt, src_start, tok_rows, xn3, w1, w3)
    down_spec = pltpu.PrefetchScalarGridSpec(
        num_scalar_prefetch=5, grid=(nb,),
        in_specs=[pl.BlockSpec((blk, De), lambda i, *_: (i, 0)),
                  pl.BlockSpec(memory_space=pl.ANY)],
        out_specs=pl.BlockSpec((blk * pitch, LANES), lambda i, *_: (i, 0)),
        scratch_shapes=[pltpu.VMEM((2, De, D), F32), pltpu.SemaphoreType.DMA((2,))])
    return pl.pallas_call(
        functools.partial(_expert_down_kernel, blk=blk, nch=nch, pitch=pitch, ych=min(1024, D)),
        grid_spec=down_spec,
        out_shape=jax.ShapeDtypeStruct((nb * blk * pitch, LANES), jnp.int32),
        compiler_params=_cparams(("arbitrary",)), name="experts_down")(
            blk_e, nused, first, par, nxt, hdn, w2)


def _combine_kernel(pos_ref, y_hbm, x_ref, rt_ref, g_ref, x2_ref, xn_ref, ybuf, sem, *, tc, nch,
                    pitch):
    i = pl.program_id(0)
    n = pl.num_programs(0)

    def row_copy(step, r, slot):
        return pltpu.make_async_copy(
            y_hbm.at[pl.ds(pl.multiple_of(pos_ref[step * 2 * tc + r], 8), nch), :],
            ybuf.at[slot, pl.ds(pl.multiple_of(r * pitch, 8), nch), :], sem.at[slot])

    def gather(step, slot):
        return _gather_rows(lambda r, p: row_copy(step, r, slot).start(priority=p),
                            lambda r: row_copy(step, r, slot).wait(), 2 * tc, split=True)

    @pl.when(i == 0)
    def _():
        gather(0, 0)[0]()

    @pl.when(i + 1 < n)
    def _():
        gather(i + 1, (i + 1) % 2)[0]()

    slot = i % 2
    gather(i, slot)[1]()
    y0, y1 = [], []
    for c in range(nch):
        y0 += _unpack_bf16_pair(ybuf[slot, pl.ds(c, tc, stride=pitch), :])
        y1 += _unpack_bf16_pair(ybuf[slot, pl.ds(tc * pitch + c, tc, stride=pitch), :])
    y0 = jnp.concatenate(y0, axis=1)
    y1 = jnp.concatenate(y1, axis=1)
    x2 = x_ref[...] + rt_ref[:, 2:3] * y0 + rt_ref[:, 3:4] * y1
    x2_ref[...] = x2
    xn_ref[...] = _rms(x2, g_ref[...]).astype(xn_ref.dtype)


def _combine(y, x, route, g, pos, tc=256):
    T, D = x.shape
    tc = min(tc, T)
    nch = D // (2 * LANES)
    pitch = _row_pitch(nch)
    grid_spec = pltpu.PrefetchScalarGridSpec(
        num_scalar_prefetch=1, grid=(T // tc,),
        in_specs=[pl.BlockSpec(memory_space=pl.ANY),
                  pl.BlockSpec((tc, D), lambda i, ps: (i, 0)),
                  pl.BlockSpec((tc, LANES), lambda i, ps: (i, 0)),
                  pl.BlockSpec((1, D), lambda i, ps: (0, 0))],
        out_specs=[pl.BlockSpec((tc, D), lambda i, ps: (i, 0)),
                   pl.BlockSpec((tc, D), lambda i, ps: (i, 0))],
        scratch_shapes=[pltpu.VMEM((2, 2 * tc * pitch, LANES), jnp.int32),
                        pltpu.SemaphoreType.DMA((2,))])
    pos_tiled = pos.reshape(T // tc, tc, 2).transpose(0, 2, 1).reshape(-1) * pitch
    return pl.pallas_call(
        functools.partial(_combine_kernel, tc=tc, nch=nch, pitch=pitch), grid_spec=grid_spec,
        out_shape=[jax.ShapeDtypeStruct((T, D), F32), jax.ShapeDtypeStruct((T, D), BF16)],
        compiler_params=_cparams(("arbitrary",)), name="combine")(
            pos_tiled, y, x, route, g.reshape(1, D))


def _ple_kernel(xn_ref, wg_ref, p_ref, wp_ref, x_ref, o_ref, wg_bf, wp_bf):
    _cache_bf16(wg_ref, wg_bf)
    _cache_bf16(wp_ref, wp_bf)
    gate = _sigmoid(jnp.dot(xn_ref[...], wg_bf[...], preferred_element_type=F32))
    emb = jnp.dot(p_ref[...].astype(BF16), wp_bf[...], preferred_element_type=F32)
    o_ref[...] = x_ref[...] + gate * emb


def _ple(xn, w_pg, p, w_ple, x, tm=1024, tn=512):
    T, D = xn.shape
    Pd = p.shape[1]
    tm, tn = min(tm, T), min(tn, D)
    return pl.pallas_call(
        _ple_kernel, grid=(D // tn, T // tm),
        in_specs=[pl.BlockSpec((tm, D), lambda j, i: (i, 0)),
                  pl.BlockSpec((D, tn), lambda j, i: (0, j)),
                  pl.BlockSpec((tm, Pd), lambda j, i: (i, 0)),
                  pl.BlockSpec((Pd, tn), lambda j, i: (0, j)),
                  pl.BlockSpec((tm, tn), lambda j, i: (i, j))],
        out_specs=pl.BlockSpec((tm, tn), lambda j, i: (i, j)),
        out_shape=jax.ShapeDtypeStruct((T, D), F32),
        scratch_shapes=[pltpu.VMEM((D, tn), BF16), pltpu.VMEM((Pd, tn), BF16)],
        compiler_params=_cparams(("parallel", "arbitrary")), name="ple")(xn, w_pg, p, w_ple, x)


def _dispatch_tables(route, blk):
    T = route.shape[0]
    A = 2 * T
    eid = route[:, 0:2].astype(jnp.int32).reshape(-1)
    iota = jnp.arange(A, dtype=jnp.int32)
    experts = jnp.arange(N_EXPERTS, dtype=jnp.int32)
    _, order = lax.sort((eid, iota), num_keys=1)
    _, inv = lax.sort((order, iota), num_keys=1)
    onehot = eid[:, None] == experts[None, :]
    counts = jnp.sum(onehot, axis=0, dtype=jnp.int32)
    starts = jnp.cumsum(counts) - counts
    nblk = (counts + blk - 1) // blk
    bends = jnp.cumsum(nblk)
    bstarts = bends - nblk
    nb = (A + N_EXPERTS * blk) // blk
    b = jnp.arange(nb, dtype=jnp.int32)
    blk_e = jnp.minimum(jnp.sum(b[:, None] >= bends[None, :], axis=1),
                        N_EXPERTS - 1).astype(jnp.int32)
    shift = starts - bstarts * blk
    src_start = b * blk + jnp.sum(
        jnp.where(blk_e[:, None] == experts[None, :], shift[None, :], 0), axis=1)
    nused = bends[-1].astype(jnp.int32).reshape(1)
    pos = inv - jnp.sum(jnp.where(onehot, shift[None, :], 0), axis=1)
    return blk_e, nused, src_start.astype(jnp.int32), order // 2, pos.reshape(T, 2)


def _layer(x2d, p2d, pos_col, B, S, g_mix, w_in, b_mgate, g_mhead, g_cq, w_uq, g_ckv, w_ukv,
           g_qn, g_kn, w_bm, w_ba, w_out, g_ffn, w_rg, b_rg, w_re, b_re, w_e1, w_e3, w_e2,
           g_ple, w_pg, w_ple):
    T, D = x2d.shape
    H = M_HEADS
    mqk, mv = H * M_QK_HD, H * M_V_HD
    o_gate = 2 * mqk + 2 * mv
    o_cq = o_gate + 4 * H
    o_ckv = o_cq + Q_LORA
    o_kpe = o_ckv + KV_LORA
    o_br = o_kpe + A_ROPE
    w_t = jnp.swapaxes(w_in, 0, 1)
    w_mid_t = lax.optimization_barrier(w_t[o_gate:o_br])
    w_cq = w_mid_t[o_cq - o_gate:o_ckv - o_gate].T.astype(BF16)
    w_ckv = w_mid_t[o_ckv - o_gate:o_kpe - o_gate].T.astype(BF16)
    w_sm = jnp.concatenate(
        [w_mid_t[o_kpe - o_gate:], w_mid_t[:4 * H],
         jnp.zeros((LANES - A_ROPE - 4 * H, D), F32)], axis=0).T.astype(BF16)

    h = _rmsnorm(x2d, g_mix)
    z_a = _matmul_t(h, w_t, 0, o_gate, BF16, name="inproj_mlstm")
    z_br = _matmul_t(h, w_t, o_br, 2 * D, BF16, name="inproj_gates")
    cq_n, ckv_n, small = _inproj_small(h, w_cq, w_ckv, w_sm, g_cq, g_ckv)

    g_col = small[:, A_ROPE:A_ROPE + 4 * H] + b_mgate[None, :]
    g_row = g_col.reshape(B, S, 4 * H).transpose(0, 2, 1)
    hb = _mlstm(z_a, g_col, g_row, B, S, reverse=True)
    hm = _mlstm(z_a, g_col, g_row, B, S, reverse=False, hb=hb, g_mhead=g_mhead)

    pad_q = A_QK_PAD - A_QK_HD
    w_uq_p = jnp.pad(w_uq.reshape(Q_LORA, A_HEADS, A_QK_HD).transpose(1, 0, 2),
                     ((0, 0), (0, 0), (0, pad_q))).astype(BF16)
    w_ukv_h = w_ukv.reshape(KV_LORA, A_HEADS, A_NOPE + A_V_HD).transpose(1, 0, 2).astype(BF16)
    g_qn_p = jnp.pad(g_qn, (0, pad_q)).reshape(1, A_QK_PAD)
    g_kn_p = jnp.pad(g_kn, (0, pad_q)).reshape(1, A_QK_PAD)
    inv = ROPE_THETA ** (-jnp.arange(0, A_ROPE, 2, dtype=F32) / A_ROPE)
    inv_row = jnp.concatenate([inv, inv, jnp.zeros((LANES - A_ROPE,), F32)]).reshape(1, LANES)
    cos_t, sin_t = _rope_tables(pos_col, inv_row)
    q = _q_prep(cq_n, w_uq_p, g_qn_p, cos_t, sin_t)
    kt, v = _kv_prep(ckv_n, w_ukv_h, small, g_kn_p, cos_t, sin_t)
    ha = _attention(q, kt, v, B, S)

    mix = _merge(hm, ha, w_bm, w_ba, z_br, 0)
    x1 = _matmul_residual(mix, w_out, x2d)

    w_r = jnp.concatenate([w_rg, w_re, jnp.zeros((D, LANES - N_GROUPS - N_EXPERTS), F32)],
                          axis=1).astype(BF16)
    b_r = jnp.concatenate([b_rg, b_re, jnp.zeros((LANES - N_GROUPS - N_EXPERTS,), F32)]
                          ).reshape(1, LANES)
    route, xn3 = _router(x1, g_ffn, w_r, b_r)
    blk = min(MOE_BLK, T)
    blk_e, nused, src_start, tok_sorted, pos = _dispatch_tables(route, blk)
    y = _experts(xn3, w_e1, w_e3, w_e2, blk_e, nused, src_start, tok_sorted, blk)
    x2, xn2 = _combine(y, x1, route, g_ple, pos)

    return _ple(xn2, w_pg, p2d, w_ple, x2)


def kernel(x, p, positions, g_mix, w_in, b_mgate, g_mhead, g_cq, w_uq, g_ckv, w_ukv, g_qn, g_kn,
           w_bm, w_ba, w_out, g_ffn, w_rg, b_rg, w_re, b_re, w_e1, w_e3, w_e2, g_ple, w_pg, w_ple):
    B, S, D = x.shape
    T = B * S
    x2d = x.reshape(T, D)
    pos_col = positions.reshape(T, 1).astype(jnp.int32)
    for l in range(p.shape[0]):
        x2d = _layer(x2d, p[l].reshape(T, -1), pos_col, B, S, g_mix[l], w_in[l], b_mgate[l],
                     g_mhead[l], g_cq[l], w_uq[l], g_ckv[l], w_ukv[l], g_qn[l], g_kn[l],
                     w_bm[l], w_ba[l], w_out[l], g_ffn[l], w_rg[l], b_rg[l], w_re[l], b_re[l],
                     w_e1[l], w_e3[l], w_e2[l], g_ple[l], w_pg[l], w_ple[l])
    return x2d.reshape(B, S, D)
```
